```python
import jax, jax.numpy as jnp
from jax import lax
import numpy as np

D_MODEL = 2048
BATCH = 4
SEQ = 8192
DEPTH = 4

FOX_W = 3 * D_MODEL // 8
RWKV_W = 5 * D_MODEL // 16
RET_W = D_MODEL - FOX_W - RWKV_W
FOX_HEAD_DIM = 128
FOX_HEADS = FOX_W // FOX_HEAD_DIM
RWKV_HEAD_DIM = 64
RWKV_HEADS = RWKV_W // RWKV_HEAD_DIM
RET_HEAD_DIM = 128
RET_HEADS = RET_W // RET_HEAD_DIM
DECAY_LORA = 64
ICLR_LORA = 64
GATE_LORA = 128
D_FF = 4 * D_MODEL
BLOCK_Q = 128
RET_CHUNK = 128
ROPE_BASE = 10000.0
LN_EPS = 1e-5
RWKV_GN_EPS = 64e-5
RET_GN_EPS = 1e-5
ALPHA = (2 * DEPTH) ** 0.25
BETA = (8 * DEPTH) ** -0.25

FOX_SIZES = (FOX_W, FOX_W, FOX_W, FOX_HEADS)
RWKV_SIZES = (RWKV_W, RWKV_W, RWKV_W, DECAY_LORA, ICLR_LORA, GATE_LORA)
RET_SIZES = (RET_W, RET_W, RET_W, RET_W)
FOX_COLS = sum(FOX_SIZES)
RWKV_COLS = sum(RWKV_SIZES)
RET_COLS = sum(RET_SIZES)
P_IN = FOX_COLS + RWKV_COLS + RET_COLS

kernel_name = 'hybrid_fox_rwkv7_retention_deepnorm'


def _split(h, sizes):
    out, o = [], 0
    for s in sizes:
        out.append(h[..., o:o + s])
        o += s
    return out


def layer_norm(x, g, b):
    xf = x.astype(jnp.float32)
    mu = jnp.mean(xf, -1, keepdims=True)
    var = jnp.mean(jnp.square(xf - mu), -1, keepdims=True)
    return ((xf - mu) * lax.rsqrt(var + LN_EPS) * g + b).astype(x.dtype)


def head_norm(y, g, b, eps):
    mu = jnp.mean(y, -1, keepdims=True)
    var = jnp.mean(jnp.square(y - mu), -1, keepdims=True)
    hd = y.shape[-2:]
    return (y - mu) * lax.rsqrt(var + eps) * g.reshape(hd) + b.reshape(hd)


def rotary(x):
    half = x.shape[-1] // 2
    inv = 1.0 / (ROPE_BASE ** (jnp.arange(half, dtype=jnp.float32) / half))
    ang = jnp.arange(x.shape[1], dtype=jnp.float32)[:, None] * inv[None, :]
    cos = jnp.cos(ang)[None, :, None, :]
    sin = jnp.sin(ang)[None, :, None, :]
    x1, x2 = x[..., :half], x[..., half:]
    return jnp.concatenate([x1 * cos - x2 * sin, x1 * sin + x2 * cos], -1)


def forgetting_attention(q, k, v, f_logit):
    B, S, H, d = q.shape
    c = jnp.cumsum(jax.nn.log_sigmoid(f_logit), axis=1).transpose(0, 2, 1)
    qh = (q * d ** -0.5).transpose(0, 2, 1, 3)
    kh = k.transpose(0, 2, 1, 3)
    vh = v.transpose(0, 2, 1, 3)
    kpos = jnp.arange(S)

    def one_block(i):
        start = i * BLOCK_Q
        qb = lax.dynamic_slice_in_dim(qh, start, BLOCK_Q, axis=2)
        cb = lax.dynamic_slice_in_dim(c, start, BLOCK_Q, axis=2)
        s = jnp.einsum('bhqd,bhkd->bhqk', qb, kh) + cb[..., None] - c[:, :, None, :]
        qpos = start + jnp.arange(BLOCK_Q)
        s = jnp.where(kpos[None, :] <= qpos[:, None], s, -jnp.inf)
        p = jax.nn.softmax(s, axis=-1)
        return jnp.einsum('bhqk,bhkd->bhqd', p, vh)

    out = lax.map(one_block, jnp.arange(S // BLOCK_Q))
    return out.transpose(1, 0, 3, 2, 4).reshape(B, S, H * d)


def rwkv7_time_mix(h, mu, w0, w_up, a0, a_up, g_up, k_k, k_a, r_k, gn_w, gn_b):
    B, S, _ = h.shape
    H, N = RWKV_HEADS, RWKV_HEAD_DIM
    h_prev = jnp.pad(h, ((0, 0), (1, 0), (0, 0)))[:, :-1]
    h = h + (h_prev - h) * mu
    r, k, v, wd, ad, gd = _split(h, RWKV_SIZES)
    w = -jax.nn.softplus(-(w0 + jnp.tanh(wd) @ w_up)) - 0.5
    decay = jnp.exp(-jnp.exp(w))
    a = jax.nn.sigmoid(a0 + ad @ a_up)
    g = jax.nn.sigmoid(gd) @ g_up
    hs = lambda t: t.reshape(B, S, H, N)
    kk = hs(k * k_k)
    kk = kk / jnp.maximum(jnp.sqrt(jnp.sum(kk * kk, -1, keepdims=True)), 1e-12)
    k = k * (1.0 + (a - 1.0) * k_a)
    r, k, v, a, decay = hs(r), hs(k), hs(v), hs(a), hs(decay)
    a_vec = -kk
    b_vec = kk * a

    def step(state, inp):
        r_t, w_t, k_t, v_t, a_t, b_t = inp
        sa = jnp.einsum('bhvk,bhk->bhv', state, a_t)
        state = (state * w_t[:, :, None, :] + sa[..., None] * b_t[:, :, None, :]
                 + v_t[..., None] * k_t[:, :, None, :])
        return state, jnp.einsum('bhvk,bhk->bhv', state, r_t)

    tm = lambda t: jnp.moveaxis(t, 1, 0)
    state0 = jnp.zeros((B, H, N, N), r.dtype)
    _, y = lax.scan(step, state0, (tm(r), tm(decay), tm(k), tm(v), tm(a_vec), tm(b_vec)))
    y = jnp.moveaxis(y, 0, 1)
    y = head_norm(y, gn_w, gn_b, RWKV_GN_EPS)
    y = y + jnp.sum(r * k * r_k, -1, keepdims=True) * v
    return y.reshape(B, S, H * N) * g


def retention(q, k, v, g, gn_w, gn_b):
    B, S, H, d = q.shape
    C = RET_CHUNK
    n = S // C
    log_g = jnp.log(1.0 - 2.0 ** (-5.0 - jnp.arange(H, dtype=jnp.float32)))
    pos = jnp.arange(C, dtype=jnp.float32)
    rel = pos[:, None] - pos[None, :]
    inner_decay = jnp.where(rel >= 0, jnp.exp(log_g[:, None, None] * jnp.maximum(rel, 0.0)), 0.0)
    q_decay = jnp.exp(log_g[:, None] * (pos + 1.0))[..., None]
    k_decay = jnp.exp(log_g[:, None] * (C - 1.0 - pos))[..., None]
    chunk_decay = jnp.exp(log_g * C)[:, None, None]
    q = rotary(q)
    k = rotary(k) * d ** -0.5
    ch = lambda t: t.reshape(B, n, C, H, d).transpose(1, 0, 3, 2, 4)

    def step(R, inp):
        qc, kc, vc = inp
        inner = jnp.einsum('bhqd,bhkd->bhqk', qc, kc) * inner_decay
        o = (jnp.einsum('bhqk,bhkd->bhqd', inner, vc)
             + jnp.einsum('bhqd,bhde->bhqe', qc, R) * q_decay)
        R = chunk_decay * R + jnp.einsum('bhkd,bhke->bhde', kc * k_decay, vc)
        return R, o

    R0 = jnp.zeros((B, H, d, d), q.dtype)
    _, o = lax.scan(step, R0, (ch(q), ch(k), ch(v)))
    o = o.transpose(1, 0, 3, 2, 4).reshape(B, S, H, d)
    o = head_norm(o, gn_w, gn_b, RET_GN_EPS).reshape(B, S, H * d)
    return jax.nn.silu(g) * o


def setup_inputs(seed: int = 0) -> dict:
    key = jax.random.key(seed)
    ks = jax.random.split(key, 24)
    f32 = jnp.float32
    nrm = lambda k, shape, s: jax.random.normal(k, shape, f32) * s
    uni = lambda k, shape, lo, hi: jax.random.uniform(k, shape, f32, lo, hi)
    L = DEPTH
    return {
        'x': nrm(ks[0], (BATCH, SEQ, D_MODEL), 1.0),
        'w_in': nrm(ks[1], (L, D_MODEL, P_IN), D_MODEL ** -0.5),
        'fox_forget_bias': uni(ks[2], (L, FOX_HEADS), 1.0, 4.0),
        'rwkv_mu': uni(ks[3], (L, RWKV_COLS), 0.0, 1.0),
        'rwkv_w0': uni(ks[4], (L, RWKV_W), -5.0, 0.0),
        'rwkv_w_up': nrm(ks[5], (L, DECAY_LORA, RWKV_W), 0.1),
        'rwkv_a0': nrm(ks[6], (L, RWKV_W), 0.1),
        'rwkv_a_up': nrm(ks[7], (L, ICLR_LORA, RWKV_W), ICLR_LORA ** -0.5),
        'rwkv_g_up': nrm(ks[8], (L, GATE_LORA, RWKV_W), GATE_LORA ** -0.5),
        'rwkv_k_k': 0.85 + nrm(ks[9], (L, RWKV_W), 0.05),
        'rwkv_k_a': 1.0 + nrm(ks[10], (L, RWKV_W), 0.05),
        'rwkv_r_k': nrm(ks[11], (L, RWKV_HEADS, RWKV_HEAD_DIM), 0.1),
        'rwkv_gn_w': 1.0 + nrm(ks[12], (L, RWKV_W), 0.05),
        'rwkv_gn_b': nrm(ks[13], (L, RWKV_W), 0.02),
        'ret_gn_w': 1.0 + nrm(ks[14], (L, RET_W), 0.05),
        'ret_gn_b': nrm(ks[15], (L, RET_W), 0.02),
        'w_out': nrm(ks[16], (L, D_MODEL, D_MODEL), BETA * D_MODEL ** -0.5),
        'ln1_g': 1.0 + nrm(ks[17], (L, D_MODEL), 0.05),
        'ln1_b': nrm(ks[18], (L, D_MODEL), 0.02),
        'w_up': nrm(ks[19], (L, D_MODEL, D_FF), BETA * D_MODEL ** -0.5),
        'w_down': nrm(ks[20], (L, D_FF, D_MODEL), BETA * D_FF ** -0.5),
        'ln2_g': 1.0 + nrm(ks[21], (L, D_MODEL), 0.05),
        'ln2_b': nrm(ks[22], (L, D_MODEL), 0.02),
    }


def reference(x, w_in, fox_forget_bias, rwkv_mu, rwkv_w0, rwkv_w_up, rwkv_a0, rwkv_a_up,
              rwkv_g_up, rwkv_k_k, rwkv_k_a, rwkv_r_k, rwkv_gn_w, rwkv_gn_b, ret_gn_w, ret_gn_b,
              w_out, ln1_g, ln1_b, w_up, w_down, ln2_g, ln2_b):
    B, S, _ = x.shape
    for l in range(DEPTH):
        h = jnp.einsum('bsd,dp->bsp', x, w_in[l]).astype(jnp.float32)
        h_fox, h_rwkv, h_ret = _split(h, (FOX_COLS, RWKV_COLS, RET_COLS))

        fq, fk, fv, ff = _split(h_fox, FOX_SIZES)
        fh = lambda t: t.reshape(B, S, FOX_HEADS, FOX_HEAD_DIM)
        y_fox = forgetting_attention(fh(fq), fh(fk), fh(fv), ff + fox_forget_bias[l])

        y_rwkv = rwkv7_time_mix(h_rwkv, rwkv_mu[l], rwkv_w0[l], rwkv_w_up[l], rwkv_a0[l],
                                rwkv_a_up[l], rwkv_g_up[l], rwkv_k_k[l], rwkv_k_a[l],
                                rwkv_r_k[l], rwkv_gn_w[l], rwkv_gn_b[l])

        rq, rk, rv, rg = _split(h_ret, RET_SIZES)
        rh = lambda t: t.reshape(B, S, RET_HEADS, RET_HEAD_DIM)
        y_ret = retention(rh(rq), rh(rk), rh(rv), rg, ret_gn_w[l], ret_gn_b[l])

        mix = jnp.concatenate([y_fox, y_rwkv, y_ret], axis=-1).astype(x.dtype)
        x = layer_norm(ALPHA * x + mix @ w_out[l], ln1_g[l], ln1_b[l])

        hid = jnp.square(jax.nn.relu(x @ w_up[l]))
        x = layer_norm(ALPHA * x + hid @ w_down[l], ln2_g[l], ln2_b[l])
    return x
```

```python
import functools

import jax
import jax.numpy as jnp
from jax import lax
from jax.experimental import pallas as pl
from jax.experimental.pallas import tpu as pltpu

F32 = jnp.float32
BF16 = jnp.bfloat16

D_MODEL = 2048
DEPTH = 4
FOX_W = 768
RWKV_W = 640
RET_W = 640
FOX_HEAD_DIM = 128
FOX_HEADS = 6
RWKV_HEAD_DIM = 64
RWKV_HEADS = 10
RET_HEAD_DIM = 128
RET_HEADS = 5
DECAY_LORA = 64
ICLR_LORA = 64
GATE_LORA = 128
D_FF = 4 * D_MODEL
RET_CHUNK = 128
ROPE_BASE = 10000.0
LN_EPS = 1e-5
RWKV_GN_EPS = 64e-5
RET_GN_EPS = 1e-5
ALPHA = (2 * DEPTH) ** 0.25

FOX_COLS = 3 * FOX_W + FOX_HEADS
RWKV_COLS = 3 * RWKV_W + DECAY_LORA + ICLR_LORA + GATE_LORA
RET_COLS = 4 * RET_W

LANE = 128
H_RWKV0 = 0
H_FOX0 = RWKV_COLS
H_RET0 = H_FOX0 + 3 * FOX_W
H_FF0 = H_RET0 + RET_COLS
H_COLS = H_FF0 + LANE

RWKV_CHUNK = 64
NEG_BIG = -1e30

VMEM_LIMIT = 56 * 1024 * 1024


def _cparams(sem):
    return pltpu.CompilerParams(dimension_semantics=sem, vmem_limit_bytes=VMEM_LIMIT)


def _dot(a, b):
    return jnp.dot(a, b, preferred_element_type=F32)


def _dot_nt(a, b):
    return lax.dot_general(a, b, (((1,), (1,)), ((), ())), preferred_element_type=F32)


def _dot_tn(a, b):
    return lax.dot_general(a, b, (((0,), (0,)), ((), ())), preferred_element_type=F32)


def _split3(x):
    hi = x.astype(BF16)
    r1 = x - hi.astype(F32)
    mid = r1.astype(BF16)
    lo = (r1 - mid.astype(F32)).astype(BF16)
    return hi, mid, lo


def _split2(x):
    hi = x.astype(BF16)
    lo = (x - hi.astype(F32)).astype(BF16)
    return hi, lo


def _mask_dot(mask_bf16, x):
    hi, mid, lo = _split3(x)
    return _dot(mask_bf16, hi) + _dot(mask_bf16, mid) + _dot(mask_bf16, lo)


def _softplus(z):
    return jnp.maximum(z, 0.0) + jnp.log1p(jnp.exp(-jnp.abs(z)))


def _sigmoid(z):
    return 1.0 / (1.0 + jnp.exp(-z))


def _layer_norm_rows(y, g, b):
    mu = jnp.mean(y, axis=-1, keepdims=True)
    yc = y - mu
    var = jnp.mean(yc * yc, axis=-1, keepdims=True)
    return yc * lax.rsqrt(var + LN_EPS) * g + b


def _inproj_kernel(x_ref, w_ref, o_ref):
    o_ref[...] = _dot(x_ref[...].astype(BF16), w_ref[...])


def _in_proj(x2d, w):
    M, K = x2d.shape
    N = w.shape[1]
    tm = min(512, M)
    tn = 1024
    return pl.pallas_call(
        _inproj_kernel,
        out_shape=jax.ShapeDtypeStruct((M, N), F32),
        grid=(M // tm, N // tn),
        in_specs=[pl.BlockSpec((tm, K), lambda i, j: (i, 0)),
                  pl.BlockSpec((K, tn), lambda i, j: (0, j))],
        out_specs=pl.BlockSpec((tm, tn), lambda i, j: (i, j)),
        compiler_params=_cparams(("parallel", "arbitrary")),
    )(x2d, w)


def _fox_c_kernel(f_ref, bias_ref, c_ref, *, rows):
    S = f_ref.shape[0]
    r = lax.broadcasted_iota(jnp.int32, (rows, rows), 0)
    c = lax.broadcasted_iota(jnp.int32, (rows, rows), 1)
    tri = (c <= r).astype(BF16)

    def body(i, carry):
        start = pl.multiple_of(i * rows, rows)
        z = f_ref[pl.ds(start, rows), :] + bias_ref[...]
        ls = -_softplus(-z)
        cs = _mask_dot(tri, ls) + carry
        c_ref[:, pl.ds(start, rows)] = cs.T[0:8, :]
        return cs[rows - 1:rows, :]

    lax.fori_loop(0, S // rows, body, jnp.zeros((1, LANE), F32))


def _fox_c(h, bias_pad, B, S):
    rows = min(256, S)
    return pl.pallas_call(
        functools.partial(_fox_c_kernel, rows=rows),
        out_shape=jax.ShapeDtypeStruct((B, 8, S), F32),
        grid=(B,),
        in_specs=[pl.BlockSpec((S, LANE), lambda b: (b, H_FF0 // LANE)),
                  pl.BlockSpec((1, LANE), lambda b: (0, 0))],
        out_specs=pl.BlockSpec((None, 8, S), lambda b: (b, 0, 0)),
        compiler_params=_cparams(("parallel",)),
    )(h, bias_pad)


def _fox_attn_kernel(q_ref, k_ref, v_ref, c_ref, o_ref, *, tq, tk):
    i = pl.program_id(2)
    d = q_ref.shape[1]
    q = (q_ref[...] * (d ** -0.5)).astype(BF16)
    q0 = pl.multiple_of(i * tq, tq)
    c0 = c_ref[:, pl.ds(q0, tq)][:, 0:1]

    def step(start, carry, masked):
        m, l, acc = carry
        kb = k_ref[pl.ds(start, tk), :].astype(BF16)
        vb = v_ref[pl.ds(start, tk), :].astype(BF16)
        s = _dot_nt(q, kb) + (c0 - c_ref[:, pl.ds(start, tk)])
        if masked:
            row = lax.broadcasted_iota(jnp.int32, (tq, tk), 0)
            col = lax.broadcasted_iota(jnp.int32, (tq, tk), 1)
            s = jnp.where(col <= row, s, NEG_BIG)
        m_new = jnp.maximum(m, jnp.max(s, axis=1, keepdims=True))
        alpha = jnp.exp(m - m_new)
        p = jnp.exp(s - m_new)
        l = alpha * l + jnp.sum(p, axis=1, keepdims=True)
        acc = alpha * acc + _dot(p.astype(BF16), vb)
        return m_new, l, acc

    def body(j, carry):
        return step(pl.multiple_of(j * tk, tk), carry, False)

    init = (jnp.full((tq, 1), NEG_BIG, F32), jnp.zeros((tq, 1), F32), jnp.zeros((tq, d), F32))
    carry = lax.fori_loop(0, i, body, init)
    m, l, acc = step(q0, carry, True)
    o_ref[...] = (acc / l).astype(o_ref.dtype)


def _fox_attention(h, c3, B, S):
    tq = tk = min(512, S)
    nq = S // tq
    qb, kb, vb = (H_FOX0 // LANE, (H_FOX0 + FOX_W) // LANE, (H_FOX0 + 2 * FOX_W) // LANE)
    return pl.pallas_call(
        functools.partial(_fox_attn_kernel, tq=tq, tk=tk),
        out_shape=jax.ShapeDtypeStruct((B * S, FOX_W), BF16),
        grid=(B, FOX_HEADS, nq),
        in_specs=[pl.BlockSpec((tq, LANE), lambda b, hd, i: (b * nq + i, qb + hd)),
                  pl.BlockSpec((S, LANE), lambda b, hd, i: (b, kb + hd)),
                  pl.BlockSpec((S, LANE), lambda b, hd, i: (b, vb + hd)),
                  pl.BlockSpec((None, 1, S), lambda b, hd, i: (b * FOX_HEADS + hd, 0, 0))],
        out_specs=pl.BlockSpec((tq, LANE), lambda b, hd, i: (b * nq + i, hd)),
        compiler_params=_cparams(("parallel", "parallel", "arbitrary")),
    )(h, h, h, c3)


def _rwkv_pre_kernel(h_ref, hprev_ref, mu_ref, w0_ref, wup_ref, a0_ref, aup_ref, gup_ref,
                     rkv_ref, lw_ref, a_ref, g_ref, *, tm, S):
    i = pl.program_id(0)
    h = h_ref[...]
    prev = hprev_ref[7:8, :]
    prev = jnp.where((i * tm) % S == 0, 0.0, prev)
    rows = lax.broadcasted_iota(jnp.int32, h.shape, 0)
    hp = jnp.where(rows == 0, prev, pltpu.roll(h, 1, 0))
    hs = h + (hp - h) * mu_ref[...]
    W3 = 3 * RWKV_W
    rkv_ref[...] = hs[:, :W3]
    wd = hs[:, W3:W3 + DECAY_LORA]
    ad = hs[:, W3 + DECAY_LORA:W3 + DECAY_LORA + ICLR_LORA]
    gd = hs[:, W3 + DECAY_LORA + ICLR_LORA:]
    z = w0_ref[...] + _dot(jnp.tanh(wd).astype(BF16), wup_ref[...])
    w = -_softplus(-z) - 0.5
    lw_ref[...] = -jnp.exp(w)
    a_ref[...] = _sigmoid(a0_ref[...] + _dot(ad.astype(BF16), aup_ref[...]))
    g_ref[...] = _dot(_sigmoid(gd).astype(BF16), gup_ref[...])


def _rwkv_pre(h, mu, w0, wup, a0, aup, gup, S):
    M = h.shape[0]
    tm = min(256, S)
    row = lambda i: (i, 0)
    const = lambda i: (0, 0)
    return pl.pallas_call(
        functools.partial(_rwkv_pre_kernel, tm=tm, S=S),
        out_shape=(jax.ShapeDtypeStruct((M, 3 * RWKV_W), F32),
                   jax.ShapeDtypeStruct((M, RWKV_W), F32),
                   jax.ShapeDtypeStruct((M, RWKV_W), F32),
                   jax.ShapeDtypeStruct((M, RWKV_W), F32)),
        grid=(M // tm,),
        in_specs=[pl.BlockSpec((tm, RWKV_COLS), row),
                  pl.BlockSpec((8, RWKV_COLS), lambda i: (jnp.maximum(i * (tm // 8) - 1, 0), 0)),
                  pl.BlockSpec((1, RWKV_COLS), const),
                  pl.BlockSpec((1, RWKV_W), const),
                  pl.BlockSpec((DECAY_LORA, RWKV_W), const),
                  pl.BlockSpec((1, RWKV_W), const),
                  pl.BlockSpec((ICLR_LORA, RWKV_W), const),
                  pl.BlockSpec((GATE_LORA, RWKV_W), const)],
        out_specs=(pl.BlockSpec((tm, 3 * RWKV_W), row),
                   pl.BlockSpec((tm, RWKV_W), row),
                   pl.BlockSpec((tm, RWKV_W), row),
                   pl.BlockSpec((tm, RWKV_W), row)),
        compiler_params=_cparams(("parallel",)),
    )(h, h, mu, w0, wup, a0, aup, gup)


def _rwkv_scan_kernel(rkv_ref, lw_ref, a_ref, g_ref, kk_ref, ka_ref, rk_ref, gw_ref, gb_ref,
                      o_ref, st_ref):
    C = RWKV_CHUNK
    N = RWKV_HEAD_DIM

    @pl.when(pl.program_id(1) == 0)
    def _():
        st_ref[...] = jnp.zeros(st_ref.shape, F32)

    rkv = rkv_ref[...]
    r = rkv[:, :RWKV_W]
    k = rkv[:, RWKV_W:2 * RWKV_W]
    v = rkv[:, 2 * RWKV_W:]
    lw = lw_ref[...]
    a = a_ref[...]
    g = g_ref[...]

    ri = lax.broadcasted_iota(jnp.int32, (C, C), 0)
    ci = lax.broadcasted_iota(jnp.int32, (C, C), 1)
    incl = ci <= ri
    strict = ci < ri

    cum = _mask_dot(incl.astype(BF16), lw)
    cl = cum[C - 1:C, :]
    e_in = jnp.exp(cum)
    e_ex = jnp.exp(cum - lw)
    e_neg = jnp.exp(-cum)
    e_tail = jnp.exp(cl - cum)
    pc = jnp.exp(cl)

    kk = k * kk_ref[...]
    k2 = k * (1.0 + (a - 1.0) * ka_ref[...])
    rkr = r * k2 * rk_ref[...]
    gw = gw_ref[...]
    gb = gb_ref[...]

    for hd in range(RWKV_HEADS):
        sl = slice(hd * N, (hd + 1) * N)
        kkh = kk[:, sl]
        nrm = jnp.sqrt(jnp.sum(kkh * kkh, axis=1, keepdims=True))
        kkh = kkh / jnp.maximum(nrm, 1e-12)
        bv = kkh * a[:, sl]
        vh = v[:, sl]
        vb = vh.astype(BF16)
        At = -kkh * e_ex[:, sl]
        Rt = r[:, sl] * e_in[:, sl]
        Bt = (bv * e_neg[:, sl]).astype(BF16)
        Kt = (k2[:, sl] * e_neg[:, sl]).astype(BF16)
        Bp = (bv * e_tail[:, sl]).astype(BF16)
        Kp = (k2[:, sl] * e_tail[:, sl]).astype(BF16)

        AR = jnp.concatenate([At, Rt], axis=0).astype(BF16)
        GB = _dot_nt(AR, Bt)
        GK = _dot_nt(AR, Kt)
        Lab = jnp.where(strict, GB[:C], 0.0)
        Lrb = jnp.where(incl, GB[C:], 0.0).astype(BF16)
        Lak = jnp.where(strict, GK[:C], 0.0).astype(BF16)
        Lrk = jnp.where(incl, GK[C:], 0.0).astype(BF16)

        Xw = At
        Xu = _dot(Lak, vb)
        P = Lab
        steps = C.bit_length() - 1
        for it in range(steps):
            Pb = P.astype(BF16)
            Xw = Xw + _dot(Pb, Xw.astype(BF16))
            Xu = Xu + _dot(Pb, Xu.astype(BF16))
            if it + 1 < steps:
                P = _dot(Pb, Pb)
        Wb = Xw.astype(BF16)
        Ub = Xu.astype(BF16)

        Qp = (Rt + _dot(Lrb, Wb)).astype(BF16)
        Y0 = _dot(Lrb, Ub) + _dot(Lrk, vb)

        S0 = st_ref[hd]
        s_hi, s_lo = _split2(S0)
        y = _dot_nt(Qp, s_hi) + _dot_nt(Qp, s_lo) + Y0

        Mw = _dot_tn(Wb, Bp).astype(BF16)
        Nn = _dot_tn(Ub, Bp) + _dot_tn(vb, Kp)
        st_ref[hd] = S0 * pc[:, sl] + _dot(s_hi, Mw) + _dot(s_lo, Mw) + Nn

        mu = jnp.mean(y, axis=1, keepdims=True)
        yc = y - mu
        var = jnp.mean(yc * yc, axis=1, keepdims=True)
        yn = yc * lax.rsqrt(var + RWKV_GN_EPS) * gw[:, sl] + gb[:, sl]
        bonus = jnp.sum(rkr[:, sl], axis=1, keepdims=True) * vh
        o_ref[:, sl] = ((yn + bonus) * g[:, sl]).astype(o_ref.dtype)


def _rwkv_scan(rkv, lw, a, g, k_k, k_a, r_k, gn_w, gn_b, B, S):
    C = RWKV_CHUNK
    nc = S // C
    row = lambda b, c: (b * nc + c, 0)
    const = lambda b, c: (0, 0)
    vec = pl.BlockSpec((1, RWKV_W), const)
    return pl.pallas_call(
        _rwkv_scan_kernel,
        out_shape=jax.ShapeDtypeStruct((B * S, RWKV_W), BF16),
        grid=(B, nc),
        in_specs=[pl.BlockSpec((C, 3 * RWKV_W), row),
                  pl.BlockSpec((C, RWKV_W), row),
                  pl.BlockSpec((C, RWKV_W), row),
                  pl.BlockSpec((C, RWKV_W), row),
                  vec, vec, vec, vec, vec],
        out_specs=pl.BlockSpec((C, RWKV_W), row),
        scratch_shapes=[pltpu.VMEM((RWKV_HEADS, RWKV_HEAD_DIM, RWKV_HEAD_DIM), F32)],
        compiler_params=_cparams(("parallel", "arbitrary")),
    )(rkv, lw, a, g, k_k, k_a, r_k, gn_w, gn_b)


def _ret_kernel(q_ref, k_ref, v_ref, g_ref, cos_ref, sin_ref, dmat_ref, qd_ref, kd_ref, cd_ref,
                gw_ref, gb_ref, o_ref, R_ref, *, nchunk):
    C = RET_CHUNK
    d = RET_HEAD_DIM

    @pl.when(pl.program_id(2) == 0)
    def _():
        R_ref[...] = jnp.zeros(R_ref.shape, F32)

    dmat = dmat_ref[...]
    qd = qd_ref[...]
    kd = kd_ref[...]
    cd = cd_ref[0:1, :]
    gw = gw_ref[...]
    gb = gb_ref[...]
    for c in range(nchunk):
        rs = slice(c * C, (c + 1) * C)
        cos = cos_ref[rs, :]
        sin = sin_ref[rs, :]
        q = q_ref[rs, :]
        k = k_ref[rs, :]
        qr = q * cos + pltpu.roll(q, d // 2, 1) * sin
        kr = (k * cos + pltpu.roll(k, d // 2, 1) * sin) * (d ** -0.5)
        vb = v_ref[rs, :].astype(BF16)
        qb = qr.astype(BF16)
        kb = kr.astype(BF16)
        inner = (_dot_nt(qb, kb) * dmat).astype(BF16)
        R = R_ref[...]
        r_hi, r_lo = _split2(R)
        o = _dot(inner, vb) + (_dot(qb, r_hi) + _dot(qb, r_lo)) * qd
        R_ref[...] = cd * R + _dot_tn((kr * kd).astype(BF16), vb)
        mu = jnp.mean(o, axis=1, keepdims=True)
        oc = o - mu
        var = jnp.mean(oc * oc, axis=1, keepdims=True)
        on = oc * lax.rsqrt(var + RET_GN_EPS) * gw + gb
        gate = g_ref[rs, :]
        o_ref[rs, :] = (gate * _sigmoid(gate) * on).astype(o_ref.dtype)


def _ret_tables(S):
    C, d, H = RET_CHUNK, RET_HEAD_DIM, RET_HEADS
    half = d // 2
    inv = 1.0 / (ROPE_BASE ** (jnp.arange(half, dtype=F32) / half))
    ang = jnp.arange(S, dtype=F32)[:, None] * inv[None, :]
    cos = jnp.cos(ang)
    sin = jnp.sin(ang)
    cos2 = jnp.concatenate([cos, cos], axis=1)
    sin2 = jnp.concatenate([-sin, sin], axis=1)
    log_g = jnp.log(1.0 - 2.0 ** (-5.0 - jnp.arange(H, dtype=F32)))
    pos = jnp.arange(C, dtype=F32)
    rel = pos[:, None] - pos[None, :]
    dmat = jnp.where(rel >= 0, jnp.exp(log_g[:, None, None] * jnp.maximum(rel, 0.0)), 0.0)
    qd = jnp.broadcast_to(jnp.exp(log_g[:, None] * (pos + 1.0))[..., None], (H, C, d))
    kd = jnp.broadcast_to(jnp.exp(log_g[:, None] * (C - 1.0 - pos))[..., None], (H, C, d))
    cd = jnp.broadcast_to(jnp.exp(log_g * C)[:, None, None], (H, 8, d))
    return cos2, sin2, dmat, qd, kd, cd


def _retention(h, tables, gn_w, gn_b, B, S):
    cos2, sin2, dmat, qd, kd, cd = tables
    C = RET_CHUNK
    T = min(512, S)
    nt = S // T
    W = RET_W // LANE
    qb0 = H_RET0 // LANE
    col = lambda off: (lambda b, hd, t: (b * nt + t, qb0 + off * W + hd))
    tab = pl.BlockSpec((T, LANE), lambda b, hd, t: (t, 0))
    per_head = lambda rows: pl.BlockSpec((None, rows, LANE), lambda b, hd, t: (hd, 0, 0))
    vec = pl.BlockSpec((1, LANE), lambda b, hd, t: (0, hd))
    return pl.pallas_call(
        functools.partial(_ret_kernel, nchunk=T // C),
        out_shape=jax.ShapeDtypeStruct((B * S, RET_W), BF16),
        grid=(B, RET_HEADS, nt),
        in_specs=[pl.BlockSpec((T, LANE), col(0)), pl.BlockSpec((T, LANE), col(1)),
                  pl.BlockSpec((T, LANE), col(2)), pl.BlockSpec((T, LANE), col(3)),
                  tab, tab, per_head(C), per_head(C), per_head(C), per_head(8), vec, vec],
        out_specs=pl.BlockSpec((T, LANE), lambda b, hd, t: (b * nt + t, hd)),
        scratch_shapes=[pltpu.VMEM((RET_HEAD_DIM, RET_HEAD_DIM), F32)],
        compiler_params=_cparams(("parallel", "parallel", "arbitrary")),
    )(h, h, h, h, cos2, sin2, dmat, qd, kd, cd, gn_w, gn_b)


def _outproj_kernel(x_ref, yf_ref, yr_ref, yt_ref, wf_ref, wr_ref, wt_ref, g_ref, b_ref, o_ref):
    y = (_dot(yf_ref[...], wf_ref[...]) + _dot(yr_ref[...], wr_ref[...])
         + _dot(yt_ref[...], wt_ref[...]))
    o_ref[...] = _layer_norm_rows(ALPHA * x_ref[...] + y, g_ref[...], b_ref[...])


def _out_proj_ln(x2d, y_fox, y_rwkv, y_ret, w_f, w_r, w_t, g, b):
    M, D = x2d.shape
    tm = min(256, M)
    row = lambda i: (i, 0)
    const = lambda i: (0, 0)
    return pl.pallas_call(
        _outproj_kernel,
        out_shape=jax.ShapeDtypeStruct((M, D), F32),
        grid=(M // tm,),
        in_specs=[pl.BlockSpec((tm, D), row), pl.BlockSpec((tm, FOX_W), row),
                  pl.BlockSpec((tm, RWKV_W), row), pl.BlockSpec((tm, RET_W), row),
                  pl.BlockSpec((FOX_W, D), const), pl.BlockSpec((RWKV_W, D), const),
                  pl.BlockSpec((RET_W, D), const),
                  pl.BlockSpec((1, D), const), pl.BlockSpec((1, D), const)],
        out_specs=pl.BlockSpec((tm, D), row),
        compiler_params=_cparams(("parallel",)),
    )(x2d, y_fox, y_rwkv, y_ret, w_f, w_r, w_t, g, b)


def _ffn_kernel(x_ref, wu_ref, wd_ref, g_ref, b_ref, o_ref, acc_ref, xb_ref):
    f = pl.program_id(1)

    @pl.when(f == 0)
    def _():
        xb_ref[...] = x_ref[...].astype(BF16)
        acc_ref[...] = jnp.zeros(acc_ref.shape, F32)

    hid = jnp.maximum(_dot(xb_ref[...], wu_ref[...]), 0.0)
    acc_ref[...] += _dot((hid * hid).astype(BF16), wd_ref[...])

    @pl.when(f == pl.num_programs(1) - 1)
    def _():
        o_ref[...] = _layer_norm_rows(ALPHA * x_ref[...] + acc_ref[...], g_ref[...], b_ref[...])


def _ffn_ln(x2d, w_up, w_down, g, b):
    M, D = x2d.shape
    F = w_up.shape[1]
    tm = min(512, M)
    tf = 512
    return pl.pallas_call(
        _ffn_kernel,
        out_shape=jax.ShapeDtypeStruct((M, D), F32),
        grid=(M // tm, F // tf),
        in_specs=[pl.BlockSpec((tm, D), lambda i, f: (i, 0)),
                  pl.BlockSpec((D, tf), lambda i, f: (0, f)),
                  pl.BlockSpec((tf, D), lambda i, f: (f, 0)),
                  pl.BlockSpec((1, D), lambda i, f: (0, 0)),
                  pl.BlockSpec((1, D), lambda i, f: (0, 0))],
        out_specs=pl.BlockSpec((tm, D), lambda i, f: (i, 0)),
        scratch_shapes=[pltpu.VMEM((tm, D), F32), pltpu.VMEM((tm, D), BF16)],
        compiler_params=_cparams(("parallel", "arbitrary")),
    )(x2d, w_up, w_down, g, b)


def _permute_w_in(w_in):
    L, D, _ = w_in.shape
    w_fox = w_in[:, :, :FOX_COLS]
    w_rwkv = w_in[:, :, FOX_COLS:FOX_COLS + RWKV_COLS]
    w_ret = w_in[:, :, FOX_COLS + RWKV_COLS:]
    w_ff = w_fox[:, :, 3 * FOX_W:]
    pad = jnp.zeros((L, D, LANE - FOX_HEADS), w_in.dtype)
    return jnp.concatenate([w_rwkv, w_fox[:, :, :3 * FOX_W], w_ret, w_ff, pad], axis=-1).astype(BF16)


def _layer(x2d, p, tables, B, S):
    h = _in_proj(x2d, p['w_in'])
    c = _fox_c(h, p['fox_bias'], B, S)
    c3 = c[:, :FOX_HEADS, :].reshape(B * FOX_HEADS, 1, S)
    y_fox = _fox_attention(h, c3, B, S)
    rkv, lw, a, g = _rwkv_pre(h, p['mu'], p['w0'], p['w_up_lora'], p['a0'], p['a_up'], p['g_up'], S)
    y_rwkv = _rwkv_scan(rkv, lw, a, g, p['k_k'], p['k_a'], p['r_k'], p['rgn_w'], p['rgn_b'], B, S)
    y_ret = _retention(h, tables, p['tgn_w'], p['tgn_b'], B, S)
    x2d = _out_proj_ln(x2d, y_fox, y_rwkv, y_ret, p['wo_f'], p['wo_r'], p['wo_t'],
                       p['ln1_g'], p['ln1_b'])
    return _ffn_ln(x2d, p['w_up'], p['w_down'], p['ln2_g'], p['ln2_b'])


def kernel(x, w_in, fox_forget_bias, rwkv_mu, rwkv_w0, rwkv_w_up, rwkv_a0, rwkv_a_up, rwkv_g_up,
           rwkv_k_k, rwkv_k_a, rwkv_r_k, rwkv_gn_w, rwkv_gn_b, ret_gn_w, ret_gn_b, w_out, ln1_g,
           ln1_b, w_up, w_down, ln2_g, ln2_b):
    B, S, D = x.shape
    L = w_in.shape[0]
    w_in_p = _permute_w_in(w_in)
    w_out_b = w_out.astype(BF16)
    w_up_b = w_up.astype(BF16)
    w_down_b = w_down.astype(BF16)
    bias_pad = jnp.pad(fox_forget_bias, ((0, 0), (0, LANE - FOX_HEADS)))
    tables = _ret_tables(S)
    row = lambda t: t.reshape(1, -1)
    x2d = x.reshape(B * S, D)
    for l in range(L):
        p = {
            'w_in': w_in_p[l],
            'fox_bias': row(bias_pad[l]),
            'mu': row(rwkv_mu[l]), 'w0': row(rwkv_w0[l]), 'w_up_lora': rwkv_w_up[l].astype(BF16),
            'a0': row(rwkv_a0[l]), 'a_up': rwkv_a_up[l].astype(BF16),
            'g_up': rwkv_g_up[l].astype(BF16),
            'k_k': row(rwkv_k_k[l]), 'k_a': row(rwkv_k_a[l]), 'r_k': row(rwkv_r_k[l]),
            'rgn_w': row(rwkv_gn_w[l]), 'rgn_b': row(rwkv_gn_b[l]),
            'tgn_w': row(ret_gn_w[l]), 'tgn_b': row(ret_gn_b[l]),
            'wo_f': w_out_b[l, :FOX_W], 'wo_r': w_out_b[l, FOX_W:FOX_W + RWKV_W],
            'wo_t': w_out_b[l, FOX_W + RWKV_W:],
            'ln1_g': row(ln1_g[l]), 'ln1_b': row(ln1_b[l]),
            'w_up': w_up_b[l], 'w_down': w_down_b[l],
            'ln2_g': row(ln2_g[l]), 'ln2_b': row(ln2_b[l]),
        }
        x2d = _layer(x2d, p, tables, B, S)
    return x2d.reshape(B, S, D)
```

```python
import functools

import jax
import jax.numpy as jnp
from jax import lax
from jax.experimental import pallas as pl
from jax.experimental.pallas import tpu as pltpu

F32 = jnp.float32
BF16 = jnp.bfloat16

D_MODEL = 2048
DEPTH = 4
FOX_W = 768
RWKV_W = 640
RET_W = 640
FOX_HEAD_DIM = 128
FOX_HEADS = 6
RWKV_HEAD_DIM = 64
RWKV_HEADS = 10
RET_HEAD_DIM = 128
RET_HEADS = 5
DECAY_LORA = 64
ICLR_LORA = 64
GATE_LORA = 128
D_FF = 4 * D_MODEL
RET_CHUNK = 128
ROPE_BASE = 10000.0
LN_EPS = 1e-5
RWKV_GN_EPS = 64e-5
RET_GN_EPS = 1e-5
ALPHA = (2 * DEPTH) ** 0.25

FOX_COLS = 3 * FOX_W + FOX_HEADS
RWKV_COLS = 3 * RWKV_W + DECAY_LORA + ICLR_LORA + GATE_LORA
RET_COLS = 4 * RET_W

LANE = 128
H_RET0 = RWKV_COLS
H_FF0 = H_RET0 + RET_COLS
H_COLS = H_FF0 + LANE
LOG2E = 1.4426950408889634
FOX_Q_SCALE = FOX_HEAD_DIM ** -0.5 * LOG2E

RWKV_CHUNK = 64
NEG_BIG = -1e30

VMEM_LIMIT = 56 * 1024 * 1024


def _cparams(sem):
    return pltpu.CompilerParams(dimension_semantics=sem, vmem_limit_bytes=VMEM_LIMIT)


def _dot(a, b):
    return jnp.dot(a, b, preferred_element_type=F32)


def _dot_nt(a, b):
    return lax.dot_general(a, b, (((1,), (1,)), ((), ())), preferred_element_type=F32)


def _dot_tn(a, b):
    return lax.dot_general(a, b, (((0,), (0,)), ((), ())), preferred_element_type=F32)


def _split3(x):
    hi = x.astype(BF16)
    r1 = x - hi.astype(F32)
    mid = r1.astype(BF16)
    lo = (r1 - mid.astype(F32)).astype(BF16)
    return hi, mid, lo


def _split2(x):
    hi = x.astype(BF16)
    lo = (x - hi.astype(F32)).astype(BF16)
    return hi, lo


def _mask_dot(mask_bf16, x):
    hi, mid, lo = _split3(x)
    return _dot(mask_bf16, hi) + _dot(mask_bf16, mid) + _dot(mask_bf16, lo)


def _softplus(z):
    return jnp.maximum(z, 0.0) + jnp.log1p(jnp.exp(-jnp.abs(z)))


def _sigmoid(z):
    return 1.0 / (1.0 + jnp.exp(-z))


def _layer_norm_rows(y, g, b):
    mu = jnp.mean(y, axis=-1, keepdims=True)
    yc = y - mu
    var = jnp.mean(yc * yc, axis=-1, keepdims=True)
    return yc * lax.rsqrt(var + LN_EPS) * g + b


def _inproj_kernel(x_ref, w_ref, o_ref):
    o_ref[...] = _dot(x_ref[...], w_ref[...])


def _inproj_scaled_kernel(x_ref, w_ref, s_ref, o_ref):
    o_ref[...] = (_dot(x_ref[...], w_ref[...]) * s_ref[...]).astype(o_ref.dtype)


def _in_proj(xb, w, col_scale=None):
    M, K = xb.shape
    N = w.shape[1]
    tm = min(512, M)
    tn = N if col_scale is not None else N // 2
    in_specs = [pl.BlockSpec((tm, K), lambda j, i: (i, 0)),
                pl.BlockSpec((K, tn), lambda j, i: (0, j))]
    args = [xb, w]
    if col_scale is not None:
        in_specs.append(pl.BlockSpec((1, tn), lambda j, i: (0, j)))
        args.append(col_scale)
    return pl.pallas_call(
        _inproj_kernel if col_scale is None else _inproj_scaled_kernel,
        name="in_proj",
        out_shape=jax.ShapeDtypeStruct((M, N), F32 if col_scale is None else BF16),
        grid=(N // tn, M // tm),
        in_specs=in_specs,
        out_specs=pl.BlockSpec((tm, tn), lambda j, i: (i, j)),
        compiler_params=_cparams(("parallel", "parallel")),
    )(*args)


def _fox_c_kernel(f_ref, bias_ref, c_ref, *, rows):
    S = f_ref.shape[0]
    r = lax.broadcasted_iota(jnp.int32, (rows, rows), 0)
    c = lax.broadcasted_iota(jnp.int32, (rows, rows), 1)
    tri = (c <= r).astype(BF16)

    def body(i, carry):
        start = pl.multiple_of(i * rows, rows)
        z = f_ref[pl.ds(start, rows), :] + bias_ref[...]
        ls = -_softplus(-z)
        cs = _mask_dot(tri, ls) + carry
        c_ref[:, pl.ds(start, rows)] = (cs * LOG2E).T[0:8, :]
        return cs[rows - 1:rows, :]

    lax.fori_loop(0, S // rows, body, jnp.zeros((1, LANE), F32))


def _fox_c(h, bias_pad, B, S):
    rows = min(256, S)
    return pl.pallas_call(
        functools.partial(_fox_c_kernel, rows=rows),
        name="fox_c",
        out_shape=jax.ShapeDtypeStruct((B, 8, S), F32),
        grid=(B,),
        in_specs=[pl.BlockSpec((S, LANE), lambda b: (b, H_FF0 // LANE)),
                  pl.BlockSpec((1, LANE), lambda b: (0, 0))],
        out_specs=pl.BlockSpec((None, 8, S), lambda b: (b, 0, 0)),
        compiler_params=_cparams(("parallel",)),
    )(h, bias_pad)


def _fox_attn_kernel(q_ref, k_ref, v_ref, c_ref, o_ref, *, tq, tk, nsplit):
    i = pl.program_id(2)
    d = q_ref.shape[1]
    hq = tq // nsplit
    q0 = pl.multiple_of(i * tq, tq)
    c0 = c_ref[:, pl.ds(q0, tq)][:, 0:1]
    qs = [q_ref[s * hq:(s + 1) * hq, :] for s in range(nsplit)]

    def step(start, carry, masked):
        kb = k_ref[pl.ds(start, tk), :]
        vb = v_ref[pl.ds(start, tk), :]
        bias = c0 - c_ref[:, pl.ds(start, tk)]
        sc = [_dot_nt(q, kb) + bias for q in qs]
        if masked:
            row = lax.broadcasted_iota(jnp.int32, (hq, tk), 0)
            col = lax.broadcasted_iota(jnp.int32, (hq, tk), 1)
            sc = [jnp.where(col <= row + s * hq, x, NEG_BIG) for s, x in enumerate(sc)]
        m_new = [jnp.maximum(c[0], jnp.max(x, axis=1, keepdims=True)) for c, x in zip(carry, sc)]
        alpha = [jnp.exp2(c[0] - mn) for c, mn in zip(carry, m_new)]
        p = [jnp.exp2(x - mn) for x, mn in zip(sc, m_new)]
        l = [al * c[1] + jnp.sum(x, axis=1, keepdims=True) for al, c, x in zip(alpha, carry, p)]
        acc = [al * c[2] + _dot(x.astype(BF16), vb) for al, c, x in zip(alpha, carry, p)]
        return tuple(zip(m_new, l, acc))

    def body(j, carry):
        return step(pl.multiple_of(j * tk, tk), carry, False)

    init = tuple((jnp.full((hq, 1), NEG_BIG, F32), jnp.zeros((hq, 1), F32),
                  jnp.zeros((hq, d), F32)) for _ in range(nsplit))
    carry = lax.fori_loop(0, i, body, init)
    carry = step(q0, carry, True)
    for s, (m, l, acc) in enumerate(carry):
        o_ref[s * hq:(s + 1) * hq, :] = (acc / l).astype(o_ref.dtype)


def _fox_attention(hq, c3, B, S):
    tq = tk = min(512, S)
    nq = S // tq
    nh = FOX_HEADS
    return pl.pallas_call(
        functools.partial(_fox_attn_kernel, tq=tq, tk=tk, nsplit=2),
        name="fox_attn",
        out_shape=jax.ShapeDtypeStruct((B * S, FOX_W), BF16),
        grid=(B, FOX_HEADS, nq),
        in_specs=[pl.BlockSpec((tq, LANE), lambda b, hd, i: (b * nq + i, hd)),
                  pl.BlockSpec((S, LANE), lambda b, hd, i: (b, nh + hd)),
                  pl.BlockSpec((S, LANE), lambda b, hd, i: (b, 2 * nh + hd)),
                  pl.BlockSpec((None, 1, S), lambda b, hd, i: (b * nh + hd, 0, 0))],
        out_specs=pl.BlockSpec((tq, LANE), lambda b, hd, i: (b * nq + i, hd)),
        compiler_params=_cparams(("parallel", "parallel", "arbitrary")),
    )(hq, hq, hq, c3)


def _rwkv_pre_kernel(h_ref, hprev_ref, mu_ref, w0_ref, wup_ref, a0_ref, aup_ref, gup_ref,
                     rkv_ref, lw_ref, a_ref, g_ref, *, tm, S):
    i = pl.program_id(0)
    h = h_ref[...]
    prev = hprev_ref[7:8, :]
    prev = jnp.where((i * tm) % S == 0, 0.0, prev)
    rows = lax.broadcasted_iota(jnp.int32, h.shape, 0)
    hp = jnp.where(rows == 0, prev, pltpu.roll(h, 1, 0))
    hs = h + (hp - h) * mu_ref[...]
    W3 = 3 * RWKV_W
    rkv_ref[...] = hs[:, :W3]
    wd = hs[:, W3:W3 + DECAY_LORA]
    ad = hs[:, W3 + DECAY_LORA:W3 + DECAY_LORA + ICLR_LORA]
    gd = hs[:, W3 + DECAY_LORA + ICLR_LORA:]
    z = w0_ref[...] + _dot(jnp.tanh(wd).astype(BF16), wup_ref[...])
    w = -_softplus(-z) - 0.5
    lw_ref[...] = -jnp.exp(w)
    a_ref[...] = _sigmoid(a0_ref[...] + _dot(ad.astype(BF16), aup_ref[...]))
    g_ref[...] = _dot(_sigmoid(gd).astype(BF16), gup_ref[...])


def _rwkv_pre(h, mu, w0, wup, a0, aup, gup, S):
    M = h.shape[0]
    tm = min(256, S)
    row = lambda i: (i, 0)
    const = lambda i: (0, 0)
    return pl.pallas_call(
        functools.partial(_rwkv_pre_kernel, tm=tm, S=S),
        name="rwkv_pre",
        out_shape=(jax.ShapeDtypeStruct((M, 3 * RWKV_W), F32),
                   jax.ShapeDtypeStruct((M, RWKV_W), F32),
                   jax.ShapeDtypeStruct((M, RWKV_W), F32),
                   jax.ShapeDtypeStruct((M, RWKV_W), F32)),
        grid=(M // tm,),
        in_specs=[pl.BlockSpec((tm, RWKV_COLS), row),
                  pl.BlockSpec((8, RWKV_COLS), lambda i: (jnp.maximum(i * (tm // 8) - 1, 0), 0)),
                  pl.BlockSpec((1, RWKV_COLS), const),
                  pl.BlockSpec((1, RWKV_W), const),
                  pl.BlockSpec((DECAY_LORA, RWKV_W), const),
                  pl.BlockSpec((1, RWKV_W), const),
                  pl.BlockSpec((ICLR_LORA, RWKV_W), const),
                  pl.BlockSpec((GATE_LORA, RWKV_W), const)],
        out_specs=(pl.BlockSpec((tm, 3 * RWKV_W), row),
                   pl.BlockSpec((tm, RWKV_W), row),
                   pl.BlockSpec((tm, RWKV_W), row),
                   pl.BlockSpec((tm, RWKV_W), row)),
        compiler_params=_cparams(("parallel",)),
    )(h, h, mu, w0, wup, a0, aup, gup)


def _rwkv_scan_kernel(rkv_ref, lw_ref, a_ref, g_ref, kk_ref, ka_ref, rk_ref, gw_ref, gb_ref,
                      o_ref, st_ref):
    C = RWKV_CHUNK
    N = RWKV_HEAD_DIM

    @pl.when(pl.program_id(1) == 0)
    def _():
        st_ref[...] = jnp.zeros(st_ref.shape, F32)

    rkv = rkv_ref[...]
    r = rkv[:, :RWKV_W]
    k = rkv[:, RWKV_W:2 * RWKV_W]
    v = rkv[:, 2 * RWKV_W:]
    lw = lw_ref[...]
    a = a_ref[...]
    g = g_ref[...]

    ri = lax.broadcasted_iota(jnp.int32, (C, C), 0)
    ci = lax.broadcasted_iota(jnp.int32, (C, C), 1)
    incl = ci <= ri
    strict = ci < ri

    cum = _mask_dot(incl.astype(BF16), lw)
    cl = cum[C - 1:C, :]
    e_in = jnp.exp(cum)
    e_ex = jnp.exp(cum - lw)
    e_neg = jnp.exp(-cum)
    e_tail = jnp.exp(cl - cum)
    pc = jnp.exp(cl)

    kk = k * kk_ref[...]
    k2 = k * (1.0 + (a - 1.0) * ka_ref[...])
    rkr = r * k2 * rk_ref[...]
    gw = gw_ref[...]
    gb = gb_ref[...]

    HD = range(RWKV_HEADS)
    sls = [slice(hd * N, (hd + 1) * N) for hd in HD]
    kkh = [kk[:, sl] for sl in sls]
    nrm = [jnp.sqrt(jnp.sum(t * t, axis=1, keepdims=True)) for t in kkh]
    kkh = [t / jnp.maximum(n, 1e-12) for t, n in zip(kkh, nrm)]
    bv = [t * a[:, sl] for t, sl in zip(kkh, sls)]
    vh = [v[:, sl] for sl in sls]
    vb = [t.astype(BF16) for t in vh]
    At = [-t * e_ex[:, sl] for t, sl in zip(kkh, sls)]
    Rt = [r[:, sl] * e_in[:, sl] for sl in sls]
    Bt = [(t * e_neg[:, sl]).astype(BF16) for t, sl in zip(bv, sls)]
    Kt = [(k2[:, sl] * e_neg[:, sl]).astype(BF16) for sl in sls]
    Bp = [(t * e_tail[:, sl]).astype(BF16) for t, sl in zip(bv, sls)]
    Kp = [(k2[:, sl] * e_tail[:, sl]).astype(BF16) for sl in sls]

    AR = [jnp.concatenate([x, y], axis=0).astype(BF16) for x, y in zip(At, Rt)]
    GB = [_dot_nt(x, y) for x, y in zip(AR, Bt)]
    GK = [_dot_nt(x, y) for x, y in zip(AR, Kt)]
    P = [jnp.where(strict, t[:C], 0.0) for t in GB]
    Lrb = [jnp.where(incl, t[C:], 0.0).astype(BF16) for t in GB]
    Lak = [jnp.where(strict, t[:C], 0.0).astype(BF16) for t in GK]
    Lrk = [jnp.where(incl, t[C:], 0.0).astype(BF16) for t in GK]

    X = [jnp.concatenate([x, _dot(y, z)], axis=1) for x, y, z in zip(At, Lak, vb)]
    steps = C.bit_length() - 1
    for it in range(steps):
        Pb = [t.astype(BF16) for t in P]
        X = [x + _dot(p, x.astype(BF16)) for x, p in zip(X, Pb)]
        if it + 1 < steps:
            P = [_dot(p, p) for p in Pb]
    Xb = [x.astype(BF16) for x in X]

    LX = [_dot(x, y) for x, y in zip(Lrb, Xb)]
    Qp = [(x + y[:, :N]).astype(BF16) for x, y in zip(Rt, LX)]
    Y0 = [y[:, N:] + _dot(x, z) for y, x, z in zip(LX, Lrk, vb)]

    S0 = [st_ref[hd] for hd in HD]
    Ssp = [_split2(s) for s in S0]
    ys = [_dot_nt(q, s[0]) + y0 for q, s, y0 in zip(Qp, Ssp, Y0)]

    XB = [_dot_tn(x, b) for x, b in zip(Xb, Bp)]
    Mw = [t[:N].astype(BF16) for t in XB]
    Nn = [t[N:] + _dot_tn(x, y) for t, x, y in zip(XB, vb, Kp)]
    for hd in HD:
        s_hi, s_lo = Ssp[hd]
        st_ref[hd] = S0[hd] * pc[:, sls[hd]] + _dot(s_hi, Mw[hd]) + _dot(s_lo, Mw[hd]) + Nn[hd]

    for hd in HD:
        sl = sls[hd]
        y = ys[hd]
        mu = jnp.mean(y, axis=1, keepdims=True)
        yc = y - mu
        var = jnp.mean(yc * yc, axis=1, keepdims=True)
        yn = yc * lax.rsqrt(var + RWKV_GN_EPS) * gw[:, sl] + gb[:, sl]
        bonus = jnp.sum(rkr[:, sl], axis=1, keepdims=True) * vh[hd]
        o_ref[:, sl] = ((yn + bonus) * g[:, sl]).astype(o_ref.dtype)


def _rwkv_scan(rkv, lw, a, g, k_k, k_a, r_k, gn_w, gn_b, B, S):
    C = RWKV_CHUNK
    nc = S // C
    row = lambda b, c: (b * nc + c, 0)
    const = lambda b, c: (0, 0)
    vec = pl.BlockSpec((1, RWKV_W), const)
    return pl.pallas_call(
        _rwkv_scan_kernel,
        name="rwkv_scan",
        out_shape=jax.ShapeDtypeStruct((B * S, RWKV_W), BF16),
        grid=(B, nc),
        in_specs=[pl.BlockSpec((C, 3 * RWKV_W), row),
                  pl.BlockSpec((C, RWKV_W), row),
                  pl.BlockSpec((C, RWKV_W), row),
                  pl.BlockSpec((C, RWKV_W), row),
                  vec, vec, vec, vec, vec],
        out_specs=pl.BlockSpec((C, RWKV_W), row),
        scratch_shapes=[pltpu.VMEM((RWKV_HEADS, RWKV_HEAD_DIM, RWKV_HEAD_DIM), F32)],
        compiler_params=_cparams(("parallel", "arbitrary")),
    )(rkv, lw, a, g, k_k, k_a, r_k, gn_w, gn_b)


def _ret_kernel(q_ref, k_ref, v_ref, g_ref, cos_ref, sin_ref, dmat_ref, qd_ref, kd_ref, cd_ref,
                gw_ref, gb_ref, o_ref, R_ref, *, nchunk):
    C = RET_CHUNK
    d = RET_HEAD_DIM

    @pl.when(pl.program_id(2) == 0)
    def _():
        R_ref[...] = jnp.zeros(R_ref.shape, F32)

    dmat = dmat_ref[...]
    qd = qd_ref[...]
    kd = kd_ref[...]
    cd = cd_ref[0:1, :]
    gw = gw_ref[...]
    gb = gb_ref[...]
    for c in range(nchunk):
        rs = slice(c * C, (c + 1) * C)
        cos = cos_ref[rs, :]
        sin = sin_ref[rs, :]
        q = q_ref[rs, :]
        k = k_ref[rs, :]
        qr = q * cos + pltpu.roll(q, d // 2, 1) * sin
        kr = (k * cos + pltpu.roll(k, d // 2, 1) * sin) * (d ** -0.5)
        vb = v_ref[rs, :].astype(BF16)
        qb = qr.astype(BF16)
        kb = kr.astype(BF16)
        inner = (_dot_nt(qb, kb) * dmat).astype(BF16)
        R = R_ref[...]
        r_hi, r_lo = _split2(R)
        o = _dot(inner, vb) + (_dot(qb, r_hi) + _dot(qb, r_lo)) * qd
        R_ref[...] = cd * R + _dot_tn((kr * kd).astype(BF16), vb)
        mu = jnp.mean(o, axis=1, keepdims=True)
        oc = o - mu
        var = jnp.mean(oc * oc, axis=1, keepdims=True)
        on = oc * lax.rsqrt(var + RET_GN_EPS) * gw + gb
        gate = g_ref[rs, :]
        o_ref[rs, :] = (gate * _sigmoid(gate) * on).astype(o_ref.dtype)


def _ret_tables(S):
    C, d, H = RET_CHUNK, RET_HEAD_DIM, RET_HEADS
    half = d // 2
    inv = 1.0 / (ROPE_BASE ** (jnp.arange(half, dtype=F32) / half))
    ang = jnp.arange(S, dtype=F32)[:, None] * inv[None, :]
    cos = jnp.cos(ang)
    sin = jnp.sin(ang)
    cos2 = jnp.concatenate([cos, cos], axis=1)
    sin2 = jnp.concatenate([-sin, sin], axis=1)
    log_g = jnp.log(1.0 - 2.0 ** (-5.0 - jnp.arange(H, dtype=F32)))
    pos = jnp.arange(C, dtype=F32)
    rel = pos[:, None] - pos[None, :]
    dmat = jnp.where(rel >= 0, jnp.exp(log_g[:, None, None] * jnp.maximum(rel, 0.0)), 0.0)
    qd = jnp.broadcast_to(jnp.exp(log_g[:, None] * (pos + 1.0))[..., None], (H, C, d))
    kd = jnp.broadcast_to(jnp.exp(log_g[:, None] * (C - 1.0 - pos))[..., None], (H, C, d))
    cd = jnp.broadcast_to(jnp.exp(log_g * C)[:, None, None], (H, 8, d))
    return cos2, sin2, dmat, qd, kd, cd


def _retention(h, tables, gn_w, gn_b, B, S):
    cos2, sin2, dmat, qd, kd, cd = tables
    C = RET_CHUNK
    T = min(512, S)
    nt = S // T
    W = RET_W // LANE
    qb0 = H_RET0 // LANE
    col = lambda off: (lambda b, hd, t: (b * nt + t, qb0 + off * W + hd))
    tab = pl.BlockSpec((T, LANE), lambda b, hd, t: (t, 0))
    per_head = lambda rows: pl.BlockSpec((None, rows, LANE), lambda b, hd, t: (hd, 0, 0))
    vec = pl.BlockSpec((1, LANE), lambda b, hd, t: (0, hd))
    return pl.pallas_call(
        functools.partial(_ret_kernel, nchunk=T // C),
        name="retention",
        out_shape=jax.ShapeDtypeStruct((B * S, RET_W), BF16),
        grid=(B, RET_HEADS, nt),
        in_specs=[pl.BlockSpec((T, LANE), col(0)), pl.BlockSpec((T, LANE), col(1)),
                  pl.BlockSpec((T, LANE), col(2)), pl.BlockSpec((T, LANE), col(3)),
                  tab, tab, per_head(C), per_head(C), per_head(C), per_head(8), vec, vec],
        out_specs=pl.BlockSpec((T, LANE), lambda b, hd, t: (b * nt + t, hd)),
        scratch_shapes=[pltpu.VMEM((RET_HEAD_DIM, RET_HEAD_DIM), F32)],
        compiler_params=_cparams(("parallel", "parallel", "arbitrary")),
    )(h, h, h, h, cos2, sin2, dmat, qd, kd, cd, gn_w, gn_b)


def _outproj_kernel(x_ref, yf_ref, yr_ref, yt_ref, wf_ref, wr_ref, wt_ref, g_ref, b_ref, o_ref):
    y = (_dot(yf_ref[...], wf_ref[...]) + _dot(yr_ref[...], wr_ref[...])
         + _dot(yt_ref[...], wt_ref[...]))
    o_ref[...] = _layer_norm_rows(ALPHA * x_ref[...] + y, g_ref[...], b_ref[...])


def _out_proj_ln(x2d, y_fox, y_rwkv, y_ret, w_f, w_r, w_t, g, b):
    M, D = x2d.shape
    tm = min(256, M)
    row = lambda i: (i, 0)
    const = lambda i: (0, 0)
    return pl.pallas_call(
        _outproj_kernel,
        name="out_proj_ln",
        out_shape=jax.ShapeDtypeStruct((M, D), F32),
        grid=(M // tm,),
        in_specs=[pl.BlockSpec((tm, D), row), pl.BlockSpec((tm, FOX_W), row),
                  pl.BlockSpec((tm, RWKV_W), row), pl.BlockSpec((tm, RET_W), row),
                  pl.BlockSpec((FOX_W, D), const), pl.BlockSpec((RWKV_W, D), const),
                  pl.BlockSpec((RET_W, D), const),
                  pl.BlockSpec((1, D), const), pl.BlockSpec((1, D), const)],
        out_specs=pl.BlockSpec((tm, D), row),
        compiler_params=_cparams(("parallel",)),
    )(x2d, y_fox, y_rwkv, y_ret, w_f, w_r, w_t, g, b)


def _ffn_kernel(x_ref, wu_ref, wd_ref, g_ref, b_ref, o_ref, ob_ref, acc_ref, xb_ref):
    f = pl.program_id(1)

    @pl.when(f == 0)
    def _():
        xb_ref[...] = x_ref[...].astype(BF16)
        acc_ref[...] = jnp.zeros(acc_ref.shape, F32)

    hid = jnp.maximum(_dot(xb_ref[...], wu_ref[...]), 0.0)
    acc_ref[...] += _dot((hid * hid).astype(BF16), wd_ref[...])

    @pl.when(f == pl.num_programs(1) - 1)
    def _():
        y = _layer_norm_rows(ALPHA * x_ref[...] + acc_ref[...], g_ref[...], b_ref[...])
        o_ref[...] = y
        ob_ref[...] = y.astype(BF16)


def _ffn_ln(x2d, w_up, w_down, g, b):
    M, D = x2d.shape
    F = w_up.shape[1]
    tm = min(512, M)
    tf = 512
    return pl.pallas_call(
        _ffn_kernel,
        name="ffn_ln",
        out_shape=(jax.ShapeDtypeStruct((M, D), F32), jax.ShapeDtypeStruct((M, D), BF16)),
        grid=(M // tm, F // tf),
        in_specs=[pl.BlockSpec((tm, D), lambda i, f: (i, 0)),
                  pl.BlockSpec((D, tf), lambda i, f: (0, f)),
                  pl.BlockSpec((tf, D), lambda i, f: (f, 0)),
                  pl.BlockSpec((1, D), lambda i, f: (0, 0)),
                  pl.BlockSpec((1, D), lambda i, f: (0, 0))],
        out_specs=(pl.BlockSpec((tm, D), lambda i, f: (i, 0)),
                   pl.BlockSpec((tm, D), lambda i, f: (i, 0))),
        scratch_shapes=[pltpu.VMEM((tm, D), F32), pltpu.VMEM((tm, D), BF16)],
        compiler_params=_cparams(("parallel", "arbitrary")),
    )(x2d, w_up, w_down, g, b)


def _permute_w_in(w_in):
    L, D, _ = w_in.shape
    w_fox = w_in[:, :, :FOX_COLS]
    w_rwkv = w_in[:, :, FOX_COLS:FOX_COLS + RWKV_COLS]
    w_ret = w_in[:, :, FOX_COLS + RWKV_COLS:]
    w_ff = w_fox[:, :, 3 * FOX_W:]
    pad = jnp.zeros((L, D, LANE - FOX_HEADS), w_in.dtype)
    w_q = w_fox[:, :, :3 * FOX_W].astype(BF16)
    w_main = jnp.concatenate([w_rwkv, w_ret, w_ff, pad], axis=-1).astype(BF16)
    return w_q, w_main


def _layer(x2d, xb, p, tables, B, S):
    hq = _in_proj(xb, p['w_q'], p['q_scale'])
    h = _in_proj(xb, p['w_in'])
    c = _fox_c(h, p['fox_bias'], B, S)
    c3 = c[:, :FOX_HEADS, :].reshape(B * FOX_HEADS, 1, S)
    y_fox = _fox_attention(hq, c3, B, S)
    rkv, lw, a, g = _rwkv_pre(h, p['mu'], p['w0'], p['w_up_lora'], p['a0'], p['a_up'], p['g_up'], S)
    y_rwkv = _rwkv_scan(rkv, lw, a, g, p['k_k'], p['k_a'], p['r_k'], p['rgn_w'], p['rgn_b'], B, S)
    y_ret = _retention(h, tables, p['tgn_w'], p['tgn_b'], B, S)
    x2d = _out_proj_ln(x2d, y_fox, y_rwkv, y_ret, p['wo_f'], p['wo_r'], p['wo_t'],
                       p['ln1_g'], p['ln1_b'])
    return _ffn_ln(x2d, p['w_up'], p['w_down'], p['ln2_g'], p['ln2_b'])


def kernel(x, w_in, fox_forget_bias, rwkv_mu, rwkv_w0, rwkv_w_up, rwkv_a0, rwkv_a_up, rwkv_g_up,
           rwkv_k_k, rwkv_k_a, rwkv_r_k, rwkv_gn_w, rwkv_gn_b, ret_gn_w, ret_gn_b, w_out, ln1_g,
           ln1_b, w_up, w_down, ln2_g, ln2_b):
    B, S, D = x.shape
    L = w_in.shape[0]
    w_q, w_main = _permute_w_in(w_in)
    q_scale = jnp.concatenate([jnp.full((1, FOX_W), FOX_Q_SCALE, F32),
                               jnp.ones((1, 2 * FOX_W), F32)], axis=1)
    w_out_b = w_out.astype(BF16)
    w_up_b = w_up.astype(BF16)
    w_down_b = w_down.astype(BF16)
    bias_pad = jnp.pad(fox_forget_bias, ((0, 0), (0, LANE - FOX_HEADS)))
    tables = _ret_tables(S)
    row = lambda t: t.reshape(1, -1)
    x2d = x.reshape(B * S, D)
    xb = x2d.astype(BF16)
    for l in range(L):
        p = {
            'w_in': w_main[l], 'w_q': w_q[l], 'q_scale': q_scale,
            'fox_bias': row(bias_pad[l]),
            'mu': row(rwkv_mu[l]), 'w0': row(rwkv_w0[l]), 'w_up_lora': rwkv_w_up[l].astype(BF16),
            'a0': row(rwkv_a0[l]), 'a_up': rwkv_a_up[l].astype(BF16),
            'g_up': rwkv_g_up[l].astype(BF16),
            'k_k': row(rwkv_k_k[l]), 'k_a': row(rwkv_k_a[l]), 'r_k': row(rwkv_r_k[l]),
            'rgn_w': row(rwkv_gn_w[l]), 'rgn_b': row(rwkv_gn_b[l]),
            'tgn_w': row(ret_gn_w[l]), 'tgn_b': row(ret_gn_b[l]),
            'wo_f': w_out_b[l, :FOX_W], 'wo_r': w_out_b[l, FOX_W:FOX_W + RWKV_W],
            'wo_t': w_out_b[l, FOX_W + RWKV_W:],
            'ln1_g': row(ln1_g[l]), 'ln1_b': row(ln1_b[l]),
            'w_up': w_up_b[l], 'w_down': w_down_b[l],
            'ln2_g': row(ln2_g[l]), 'ln2_b': row(ln2_b[l]),
        }
        x2d, xb = _layer(x2d, xb, p, tables, B, S)
    return x2d.reshape(B, S, D)
```

```python
import functools

import jax
import jax.numpy as jnp
from jax import lax
from jax.experimental import pallas as pl
from jax.experimental.pallas import tpu as pltpu

F32 = jnp.float32
BF16 = jnp.bfloat16

D_MODEL = 2048
DEPTH = 4
FOX_W = 768
RWKV_W = 640
RET_W = 640
FOX_HEAD_DIM = 128
FOX_HEADS = 6
RWKV_HEAD_DIM = 64
RWKV_HEADS = 10
RET_HEAD_DIM = 128
RET_HEADS = 5
DECAY_LORA = 64
ICLR_LORA = 64
GATE_LORA = 128
D_FF = 4 * D_MODEL
RET_CHUNK = 128
ROPE_BASE = 10000.0
LN_EPS = 1e-5
RWKV_GN_EPS = 64e-5
RET_GN_EPS = 1e-5
ALPHA = (2 * DEPTH) ** 0.25

FOX_COLS = 3 * FOX_W + FOX_HEADS
RWKV_COLS = 3 * RWKV_W + DECAY_LORA + ICLR_LORA + GATE_LORA
RET_COLS = 4 * RET_W

LANE = 128
H_RET0 = RWKV_COLS
H_FF0 = H_RET0 + RET_COLS
H_COLS = H_FF0 + LANE
LOG2E = 1.4426950408889634
FOX_Q_SCALE = FOX_HEAD_DIM ** -0.5 * LOG2E

RWKV_CHUNK = 64
RWKV_CHUNKS_PER_STEP = 4
NEG_BIG = -1e30

VMEM_LIMIT = 56 * 1024 * 1024


def _cparams(sem):
    return pltpu.CompilerParams(dimension_semantics=sem, vmem_limit_bytes=VMEM_LIMIT)


def _dot(a, b):
    return jnp.dot(a, b, preferred_element_type=F32)


def _dot_nt(a, b):
    return lax.dot_general(a, b, (((1,), (1,)), ((), ())), preferred_element_type=F32)


def _dot_tn(a, b):
    return lax.dot_general(a, b, (((0,), (0,)), ((), ())), preferred_element_type=F32)


def _split3(x):
    hi = x.astype(BF16)
    r1 = x - hi.astype(F32)
    mid = r1.astype(BF16)
    lo = (r1 - mid.astype(F32)).astype(BF16)
    return hi, mid, lo


def _split2(x):
    hi = x.astype(BF16)
    lo = (x - hi.astype(F32)).astype(BF16)
    return hi, lo


def _mask_dot(mask_bf16, x):
    hi, mid, lo = _split3(x)
    return _dot(mask_bf16, hi) + _dot(mask_bf16, mid) + _dot(mask_bf16, lo)


def _softplus(z):
    return jnp.maximum(z, 0.0) + jnp.log1p(jnp.exp(-jnp.abs(z)))


def _sigmoid(z):
    return 1.0 / (1.0 + jnp.exp(-z))


def _layer_norm_rows(y, g, b):
    mu = jnp.mean(y, axis=-1, keepdims=True)
    yc = y - mu
    var = jnp.mean(yc * yc, axis=-1, keepdims=True)
    return yc * lax.rsqrt(var + LN_EPS) * g + b


def _inproj_kernel(x_ref, w_ref, o_ref):
    o_ref[...] = _dot(x_ref[...], w_ref[...])


def _inproj_scaled_kernel(x_ref, w_ref, s_ref, o_ref):
    o_ref[...] = (_dot(x_ref[...], w_ref[...]) * s_ref[...]).astype(o_ref.dtype)


def _in_proj(xb, w, col_scale=None):
    M, K = xb.shape
    N = w.shape[1]
    tm = min(512, M)
    tn = N if col_scale is not None else N // 2
    in_specs = [pl.BlockSpec((tm, K), lambda j, i: (i, 0)),
                pl.BlockSpec((K, tn), lambda j, i: (0, j))]
    args = [xb, w]
    if col_scale is not None:
        in_specs.append(pl.BlockSpec((1, tn), lambda j, i: (0, j)))
        args.append(col_scale)
    return pl.pallas_call(
        _inproj_kernel if col_scale is None else _inproj_scaled_kernel,
        name="in_proj",
        out_shape=jax.ShapeDtypeStruct((M, N), F32 if col_scale is None else BF16),
        grid=(N // tn, M // tm),
        in_specs=in_specs,
        out_specs=pl.BlockSpec((tm, tn), lambda j, i: (i, j)),
        compiler_params=_cparams(("parallel", "parallel")),
    )(*args)


def _fox_c_kernel(f_ref, bias_ref, c_ref, *, rows):
    S = f_ref.shape[0]
    r = lax.broadcasted_iota(jnp.int32, (rows, rows), 0)
    c = lax.broadcasted_iota(jnp.int32, (rows, rows), 1)
    tri = (c <= r).astype(BF16)

    def body(i, carry):
        start = pl.multiple_of(i * rows, rows)
        z = f_ref[pl.ds(start, rows), :] + bias_ref[...]
        ls = -_softplus(-z)
        cs = _mask_dot(tri, ls) + carry
        c_ref[:, pl.ds(start, rows)] = (cs * LOG2E).T[0:8, :]
        return cs[rows - 1:rows, :]

    lax.fori_loop(0, S // rows, body, jnp.zeros((1, LANE), F32))


def _fox_c(h, bias_pad, B, S):
    rows = min(256, S)
    return pl.pallas_call(
        functools.partial(_fox_c_kernel, rows=rows),
        name="fox_c",
        out_shape=jax.ShapeDtypeStruct((B, 8, S), F32),
        grid=(B,),
        in_specs=[pl.BlockSpec((S, LANE), lambda b: (b, H_FF0 // LANE)),
                  pl.BlockSpec((1, LANE), lambda b: (0, 0))],
        out_specs=pl.BlockSpec((None, 8, S), lambda b: (b, 0, 0)),
        compiler_params=_cparams(("parallel",)),
    )(h, bias_pad)


def _fox_attn_kernel(q_ref, k_ref, v_ref, c_ref, o_ref, s0_ref, *, tq, tk):
    i = pl.program_id(2)
    d = q_ref.shape[1]
    q0 = pl.multiple_of(i * tq, tq)
    c0 = c_ref[:, pl.ds(q0, tq)][:, 0:1]
    q = q_ref[...]

    def scores(start):
        bias = c0 - c_ref[:, pl.ds(start, tk)]
        return _dot_nt(q, k_ref[pl.ds(start, tk), :]) + bias

    ones = jnp.ones((tk, d), BF16)

    def update(start, sc, carry, masked):
        m, acc = carry
        if masked:
            row = lax.broadcasted_iota(jnp.int32, (tq, tk), 0)
            col = lax.broadcasted_iota(jnp.int32, (tq, tk), 1)
            sc = jnp.where(col <= row, sc, NEG_BIG)
        m_new = jnp.maximum(m, jnp.max(sc, axis=1, keepdims=True))
        alpha = jnp.exp2(m - m_new)
        p = jnp.exp2((sc - m_new).astype(BF16))
        vb = jnp.concatenate([v_ref[pl.ds(start, tk), :], ones], axis=1)
        acc = alpha * acc + _dot(p, vb)
        return m_new, acc

    def pair(t, carry):
        b0 = pl.multiple_of(2 * t * tk, tk)
        b1 = pl.multiple_of(b0 + tk, tk)
        s1 = scores(b1)
        carry = update(b0, s0_ref[...], carry, False)
        s0_ref[...] = scores(pl.multiple_of(b1 + tk, tk))
        return update(b1, s1, carry, False)

    s0_ref[...] = scores(0)
    init = (jnp.full((tq, 1), NEG_BIG, F32), jnp.zeros((tq, 2 * d), F32))
    carry = lax.fori_loop(0, i // 2, pair, init)

    def odd_tail(carry):
        s1 = scores(q0)
        carry = update(pl.multiple_of(q0 - tk, tk), s0_ref[...], carry, False)
        return update(q0, s1, carry, True)

    def even_tail(carry):
        return update(q0, s0_ref[...], carry, True)

    m, acc = lax.cond(i % 2 == 1, odd_tail, even_tail, carry)
    o_ref[...] = (acc[:, :d] / acc[:, d:]).astype(o_ref.dtype)


def _fox_attention(hq, c3, B, S):
    tq = tk = min(512, S)
    nq = S // tq
    nh = FOX_HEADS
    return pl.pallas_call(
        functools.partial(_fox_attn_kernel, tq=tq, tk=tk),
        name="fox_attn",
        scratch_shapes=[pltpu.VMEM((tq, tk), F32)],
        out_shape=jax.ShapeDtypeStruct((B * S, FOX_W), BF16),
        grid=(B, FOX_HEADS, nq),
        in_specs=[pl.BlockSpec((tq, LANE), lambda b, hd, i: (b * nq + i, hd)),
                  pl.BlockSpec((S, LANE), lambda b, hd, i: (b, nh + hd)),
                  pl.BlockSpec((S, LANE), lambda b, hd, i: (b, 2 * nh + hd)),
                  pl.BlockSpec((None, 1, S), lambda b, hd, i: (b * nh + hd, 0, 0))],
        out_specs=pl.BlockSpec((tq, LANE), lambda b, hd, i: (b * nq + i, hd)),
        compiler_params=_cparams(("parallel", "parallel", "arbitrary")),
    )(hq, hq, hq, c3)


def _rwkv_pre_kernel(h_ref, hprev_ref, mu_ref, w0_ref, wup_ref, a0_ref, aup_ref, gup_ref,
                     rkv_ref, lw_ref, a_ref, g_ref, *, tm, S):
    i = pl.program_id(0)
    h = h_ref[...]
    prev = hprev_ref[7:8, :]
    prev = jnp.where((i * tm) % S == 0, 0.0, prev)
    rows = lax.broadcasted_iota(jnp.int32, h.shape, 0)
    hp = jnp.where(rows == 0, prev, pltpu.roll(h, 1, 0))
    hs = h + (hp - h) * mu_ref[...]
    W3 = 3 * RWKV_W
    rkv_ref[...] = hs[:, :W3]
    wd = hs[:, W3:W3 + DECAY_LORA]
    ad = hs[:, W3 + DECAY_LORA:W3 + DECAY_LORA + ICLR_LORA]
    gd = hs[:, W3 + DECAY_LORA + ICLR_LORA:]
    z = w0_ref[...] + _dot(jnp.tanh(wd).astype(BF16), wup_ref[...])
    w = -_softplus(-z) - 0.5
    lw_ref[...] = -jnp.exp(w)
    a_ref[...] = _sigmoid(a0_ref[...] + _dot(ad.astype(BF16), aup_ref[...]))
    g_ref[...] = _dot(_sigmoid(gd).astype(BF16), gup_ref[...])


def _rwkv_pre(h, mu, w0, wup, a0, aup, gup, S):
    M = h.shape[0]
    tm = min(256, S)
    row = lambda i: (i, 0)
    const = lambda i: (0, 0)
    return pl.pallas_call(
        functools.partial(_rwkv_pre_kernel, tm=tm, S=S),
        name="rwkv_pre",
        out_shape=(jax.ShapeDtypeStruct((M, 3 * RWKV_W), F32),
                   jax.ShapeDtypeStruct((M, RWKV_W), F32),
                   jax.ShapeDtypeStruct((M, RWKV_W), F32),
                   jax.ShapeDtypeStruct((M, RWKV_W), F32)),
        grid=(M // tm,),
        in_specs=[pl.BlockSpec((tm, RWKV_COLS), row),
                  pl.BlockSpec((8, RWKV_COLS), lambda i: (jnp.maximum(i * (tm // 8) - 1, 0), 0)),
                  pl.BlockSpec((1, RWKV_COLS), const),
                  pl.BlockSpec((1, RWKV_W), const),
                  pl.BlockSpec((DECAY_LORA, RWKV_W), const),
                  pl.BlockSpec((1, RWKV_W), const),
                  pl.BlockSpec((ICLR_LORA, RWKV_W), const),
                  pl.BlockSpec((GATE_LORA, RWKV_W), const)],
        out_specs=(pl.BlockSpec((tm, 3 * RWKV_W), row),
                   pl.BlockSpec((tm, RWKV_W), row),
                   pl.BlockSpec((tm, RWKV_W), row),
                   pl.BlockSpec((tm, RWKV_W), row)),
        compiler_params=_cparams(("parallel",)),
    )(h, h, mu, w0, wup, a0, aup, gup)


def _head_sums(x, e_ref):
    hi, lo = _split2(x)
    e = e_ref[...]
    e_last = e[:LANE, :LANE]
    outs = []
    for c0 in range(0, RWKV_W, 2 * LANE):
        w = min(2 * LANE, RWKV_W - c0)
        ee = e if w == 2 * LANE else e_last
        outs.append(_dot(hi[:, c0:c0 + w], ee) + _dot(lo[:, c0:c0 + w], ee))
    return jnp.concatenate(outs, axis=1)


def _rwkv_scan_kernel(rkv_ref, lw_ref, a_ref, g_ref, kk_ref, ka_ref, rk_ref, gw_ref, gb_ref,
                      e_ref, o_ref, st_ref, *, nchunk):
    C = RWKV_CHUNK
    N = RWKV_HEAD_DIM
    T = nchunk * C

    @pl.when(pl.program_id(1) == 0)
    def _():
        st_ref[...] = jnp.zeros(st_ref.shape, F32)

    rkv = rkv_ref[...]
    r = rkv[:, :RWKV_W]
    k = rkv[:, RWKV_W:2 * RWKV_W]
    v = rkv[:, 2 * RWKV_W:]
    lw = lw_ref[...]
    a = a_ref[...]

    ri = lax.broadcasted_iota(jnp.int32, (C, C), 0)
    ci = lax.broadcasted_iota(jnp.int32, (C, C), 1)
    incl = ci <= ri
    strict = ci < ri

    rt = lax.broadcasted_iota(jnp.int32, (T, T), 0)
    ct = lax.broadcasted_iota(jnp.int32, (T, T), 1)
    tri = ((ct <= rt) & (ct >= (rt // C) * C)).astype(BF16)
    lw_hi, lw_lo = _split2(lw)
    cum = _dot(tri, lw_hi) + _dot(tri, lw_lo)
    cls = [cum[(c + 1) * C - 1:(c + 1) * C, :] for c in range(nchunk)]
    cl_rows = jnp.concatenate([jnp.broadcast_to(t, (C, RWKV_W)) for t in cls], axis=0)
    e_in = jnp.exp(cum)
    e_ex = jnp.exp(cum - lw)
    e_neg = jnp.exp(-cum)
    e_tail = jnp.exp(cl_rows - cum)
    pcs = [jnp.exp(t) for t in cls]

    kk = k * kk_ref[...]
    kkn = kk / jnp.maximum(jnp.sqrt(_head_sums(kk * kk, e_ref)), 1e-12)
    k2 = k * (1.0 + (a - 1.0) * ka_ref[...])
    bvec = kkn * a
    At_f = -kkn * e_ex
    Rt_f = r * e_in
    AtB = At_f.astype(BF16)
    RtB = Rt_f.astype(BF16)
    BtB = (bvec * e_neg).astype(BF16)
    KtB = (k2 * e_neg).astype(BF16)
    BpB = (bvec * e_tail).astype(BF16)
    KpB = (k2 * e_tail).astype(BF16)
    vB = v.astype(BF16)

    HD = range(RWKV_HEADS)
    hsl = [slice(hd * N, (hd + 1) * N) for hd in HD]
    sls = [(slice(c * C, (c + 1) * C), hs) for c in range(nchunk) for hs in hsl]
    At = [At_f[rs, hs] for rs, hs in sls]
    Rt = [Rt_f[rs, hs] for rs, hs in sls]
    vb = [vB[rs, hs] for rs, hs in sls]
    Bt = [BtB[rs, hs] for rs, hs in sls]
    Kt = [KtB[rs, hs] for rs, hs in sls]
    Bp = [BpB[rs, hs] for rs, hs in sls]
    Kp = [KpB[rs, hs] for rs, hs in sls]

    AR = [jnp.concatenate([AtB[rs, hs], RtB[rs, hs]], axis=0) for rs, hs in sls]
    BK = [jnp.concatenate([x, y], axis=0) for x, y in zip(Bt, Kt)]
    G = [_dot_nt(x, y) for x, y in zip(AR, BK)]
    r2 = lax.broadcasted_iota(jnp.int32, (C, 2 * C), 0)
    c2 = lax.broadcasted_iota(jnp.int32, (C, 2 * C), 1)
    incl2 = jnp.where(c2 >= C, c2 - C, c2) <= r2
    P = [jnp.where(strict, t[:C, :C], 0.0) for t in G]
    Lak = [jnp.where(strict, t[:C, C:], 0.0).astype(BF16) for t in G]
    L2 = [jnp.where(incl2, t[C:], 0.0).astype(BF16) for t in G]

    X = [jnp.concatenate([x, _dot(y, z)], axis=1) for x, y, z in zip(At, Lak, vb)]
    steps = C.bit_length() - 1
    for it in range(steps):
        Pb = [t.astype(BF16) for t in P]
        Xb = [x.astype(BF16) for x in X]
        if it + 1 < steps:
            PX = [_dot(p, jnp.concatenate([x, p], axis=1)) for p, x in zip(Pb, Xb)]
            X = [x + t[:, :2 * N] for x, t in zip(X, PX)]
            P = [t[:, 2 * N:] for t in PX]
        else:
            X = [x + _dot(p, xb) for x, p, xb in zip(X, Pb, Xb)]
    Xb = [x.astype(BF16) for x in X]

    zpad = jnp.zeros((C, N), BF16)
    RHS = [jnp.concatenate([x, jnp.concatenate([zpad, z], axis=1)], axis=0)
           for x, z in zip(Xb, vb)]
    LX = [_dot(x, y) for x, y in zip(L2, RHS)]
    Qp = [(x + y[:, :N]).astype(BF16) for x, y in zip(Rt, LX)]
    Y0 = [y[:, N:] for y in LX]

    XB = [_dot_tn(x, b) for x, b in zip(Xb, Bp)]
    Mw = [t[:N].astype(BF16) for t in XB]
    Nn = [t[N:] + _dot_tn(x, y) for t, x, y in zip(XB, vb, Kp)]

    S = [st_ref[hd] for hd in HD]
    y_rows = []
    for c in range(nchunk):
        ch = [c * RWKV_HEADS + hd for hd in HD]
        Ssp = [_split2(s) for s in S]
        ys = [_dot_nt(Qp[j], sp[0]) + Y0[j] for j, sp in zip(ch, Ssp)]
        y_rows.append(jnp.concatenate(ys, axis=1))
        S = [s * pcs[c][:, hs] + _dot(sp[0], Mw[j]) + _dot(sp[1], Mw[j]) + Nn[j]
             for s, sp, j, hs in zip(S, Ssp, ch, hsl)]
    for hd in HD:
        st_ref[hd] = S[hd]

    y = jnp.concatenate(y_rows, axis=0)
    inv_n = 1.0 / N
    yc = y - _head_sums(y, e_ref) * inv_n
    var = _head_sums(yc * yc, e_ref) * inv_n
    yn = yc * lax.rsqrt(var + RWKV_GN_EPS) * gw_ref[...] + gb_ref[...]
    bonus = _head_sums(r * k2 * rk_ref[...], e_ref) * v
    o_ref[...] = ((yn + bonus) * g_ref[...]).astype(o_ref.dtype)


def _rwkv_scan(rkv, lw, a, g, k_k, k_a, r_k, gn_w, gn_b, B, S):
    nchunk = RWKV_CHUNKS_PER_STEP
    T = nchunk * RWKV_CHUNK
    nc = S // T
    row = lambda b, c: (b * nc + c, 0)
    const = lambda b, c: (0, 0)
    vec = pl.BlockSpec((1, RWKV_W), const)
    hid = jnp.arange(2 * LANE, dtype=jnp.int32) // RWKV_HEAD_DIM
    head_ones = (hid[:, None] == hid[None, :]).astype(BF16)
    return pl.pallas_call(
        functools.partial(_rwkv_scan_kernel, nchunk=nchunk),
        name="rwkv_scan",
        out_shape=jax.ShapeDtypeStruct((B * S, RWKV_W), BF16),
        grid=(B, nc),
        in_specs=[pl.BlockSpec((T, 3 * RWKV_W), row),
                  pl.BlockSpec((T, RWKV_W), row),
                  pl.BlockSpec((T, RWKV_W), row),
                  pl.BlockSpec((T, RWKV_W), row),
                  vec, vec, vec, vec, vec,
                  pl.BlockSpec((2 * LANE, 2 * LANE), const)],
        out_specs=pl.BlockSpec((T, RWKV_W), row),
        scratch_shapes=[pltpu.VMEM((RWKV_HEADS, RWKV_HEAD_DIM, RWKV_HEAD_DIM), F32)],
        compiler_params=_cparams(("parallel", "arbitrary")),
    )(rkv, lw, a, g, k_k, k_a, r_k, gn_w, gn_b, head_ones)


def _ret_kernel(q_ref, k_ref, v_ref, g_ref, cos_ref, sin_ref, dmat_ref, qd_ref, kd_ref, cd_ref,
                gw_ref, gb_ref, o_ref, R_ref, *, nchunk):
    C = RET_CHUNK
    d = RET_HEAD_DIM

    @pl.when(pl.program_id(2) == 0)
    def _():
        R_ref[...] = jnp.zeros(R_ref.shape, F32)

    dmat = dmat_ref[...]
    qd = qd_ref[...]
    kd = kd_ref[...]
    cd = cd_ref[0:1, :]
    gw = gw_ref[...]
    gb = gb_ref[...]
    for c in range(nchunk):
        rs = slice(c * C, (c + 1) * C)
        cos = cos_ref[rs, :]
        sin = sin_ref[rs, :]
        q = q_ref[rs, :]
        k = k_ref[rs, :]
        qr = q * cos + pltpu.roll(q, d // 2, 1) * sin
        kr = (k * cos + pltpu.roll(k, d // 2, 1) * sin) * (d ** -0.5)
        vb = v_ref[rs, :].astype(BF16)
        qb = qr.astype(BF16)
        kb = kr.astype(BF16)
        inner = (_dot_nt(qb, kb) * dmat).astype(BF16)
        R = R_ref[...]
        r_hi, r_lo = _split2(R)
        o = _dot(inner, vb) + (_dot(qb, r_hi) + _dot(qb, r_lo)) * qd
        R_ref[...] = cd * R + _dot_tn((kr * kd).astype(BF16), vb)
        mu = jnp.mean(o, axis=1, keepdims=True)
        oc = o - mu
        var = jnp.mean(oc * oc, axis=1, keepdims=True)
        on = oc * lax.rsqrt(var + RET_GN_EPS) * gw + gb
        gate = g_ref[rs, :]
        o_ref[rs, :] = (gate * _sigmoid(gate) * on).astype(o_ref.dtype)


def _ret_tables(S):
    C, d, H = RET_CHUNK, RET_HEAD_DIM, RET_HEADS
    half = d // 2
    inv = 1.0 / (ROPE_BASE ** (jnp.arange(half, dtype=F32) / half))
    ang = jnp.arange(S, dtype=F32)[:, None] * inv[None, :]
    cos = jnp.cos(ang)
    sin = jnp.sin(ang)
    cos2 = jnp.concatenate([cos, cos], axis=1)
    sin2 = jnp.concatenate([-sin, sin], axis=1)
    log_g = jnp.log(1.0 - 2.0 ** (-5.0 - jnp.arange(H, dtype=F32)))
    pos = jnp.arange(C, dtype=F32)
    rel = pos[:, None] - pos[None, :]
    dmat = jnp.where(rel >= 0, jnp.exp(log_g[:, None, None] * jnp.maximum(rel, 0.0)), 0.0)
    qd = jnp.broadcast_to(jnp.exp(log_g[:, None] * (pos + 1.0))[..., None], (H, C, d))
    kd = jnp.broadcast_to(jnp.exp(log_g[:, None] * (C - 1.0 - pos))[..., None], (H, C, d))
    cd = jnp.broadcast_to(jnp.exp(log_g * C)[:, None, None], (H, 8, d))
    return cos2, sin2, dmat, qd, kd, cd


def _retention(h, tables, gn_w, gn_b, B, S):
    cos2, sin2, dmat, qd, kd, cd = tables
    C = RET_CHUNK
    T = min(512, S)
    nt = S // T
    W = RET_W // LANE
    qb0 = H_RET0 // LANE
    col = lambda off: (lambda b, hd, t: (b * nt + t, qb0 + off * W + hd))
    tab = pl.BlockSpec((T, LANE), lambda b, hd, t: (t, 0))
    per_head = lambda rows: pl.BlockSpec((None, rows, LANE), lambda b, hd, t: (hd, 0, 0))
    vec = pl.BlockSpec((1, LANE), lambda b, hd, t: (0, hd))
    return pl.pallas_call(
        functools.partial(_ret_kernel, nchunk=T // C),
        name="retention",
        out_shape=jax.ShapeDtypeStruct((B * S, RET_W), BF16),
        grid=(B, RET_HEADS, nt),
        in_specs=[pl.BlockSpec((T, LANE), col(0)), pl.BlockSpec((T, LANE), col(1)),
                  pl.BlockSpec((T, LANE), col(2)), pl.BlockSpec((T, LANE), col(3)),
                  tab, tab, per_head(C), per_head(C), per_head(C), per_head(8), vec, vec],
        out_specs=pl.BlockSpec((T, LANE), lambda b, hd, t: (b * nt + t, hd)),
        scratch_shapes=[pltpu.VMEM((RET_HEAD_DIM, RET_HEAD_DIM), F32)],
        compiler_params=_cparams(("parallel", "parallel", "arbitrary")),
    )(h, h, h, h, cos2, sin2, dmat, qd, kd, cd, gn_w, gn_b)


def _outproj_kernel(x_ref, yf_ref, yr_ref, yt_ref, wf_ref, wr_ref, wt_ref, g_ref, b_ref, o_ref):
    y = (_dot(yf_ref[...], wf_ref[...]) + _dot(yr_ref[...], wr_ref[...])
         + _dot(yt_ref[...], wt_ref[...]))
    o_ref[...] = _layer_norm_rows(ALPHA * x_ref[...] + y, g_ref[...], b_ref[...])


def _out_proj_ln(x2d, y_fox, y_rwkv, y_ret, w_f, w_r, w_t, g, b):
    M, D = x2d.shape
    tm = min(256, M)
    row = lambda i: (i, 0)
    const = lambda i: (0, 0)
    return pl.pallas_call(
        _outproj_kernel,
        name="out_proj_ln",
        out_shape=jax.ShapeDtypeStruct((M, D), F32),
        grid=(M // tm,),
        in_specs=[pl.BlockSpec((tm, D), row), pl.BlockSpec((tm, FOX_W), row),
                  pl.BlockSpec((tm, RWKV_W), row), pl.BlockSpec((tm, RET_W), row),
                  pl.BlockSpec((FOX_W, D), const), pl.BlockSpec((RWKV_W, D), const),
                  pl.BlockSpec((RET_W, D), const),
                  pl.BlockSpec((1, D), const), pl.BlockSpec((1, D), const)],
        out_specs=pl.BlockSpec((tm, D), row),
        compiler_params=_cparams(("parallel",)),
    )(x2d, y_fox, y_rwkv, y_ret, w_f, w_r, w_t, g, b)


def _ffn_kernel(x_ref, wu_ref, wd_ref, g_ref, b_ref, o_ref, ob_ref, acc_ref, xb_ref):
    f = pl.program_id(1)

    @pl.when(f == 0)
    def _():
        xb_ref[...] = x_ref[...].astype(BF16)
        acc_ref[...] = jnp.zeros(acc_ref.shape, F32)

    hid = jnp.maximum(_dot(xb_ref[...], wu_ref[...]), 0.0)
    acc_ref[...] += _dot((hid * hid).astype(BF16), wd_ref[...])

    @pl.when(f == pl.num_programs(1) - 1)
    def _():
        y = _layer_norm_rows(ALPHA * x_ref[...] + acc_ref[...], g_ref[...], b_ref[...])
        o_ref[...] = y
        ob_ref[...] = y.astype(BF16)


def _ffn_ln(x2d, w_up, w_down, g, b):
    M, D = x2d.shape
    F = w_up.shape[1]
    tm = min(512, M)
    tf = 1024
    return pl.pallas_call(
        _ffn_kernel,
        name="ffn_ln",
        out_shape=(jax.ShapeDtypeStruct((M, D), F32), jax.ShapeDtypeStruct((M, D), BF16)),
        grid=(M // tm, F // tf),
        in_specs=[pl.BlockSpec((tm, D), lambda i, f: (i, 0)),
                  pl.BlockSpec((D, tf), lambda i, f: (0, f)),
                  pl.BlockSpec((tf, D), lambda i, f: (f, 0)),
                  pl.BlockSpec((1, D), lambda i, f: (0, 0)),
                  pl.BlockSpec((1, D), lambda i, f: (0, 0))],
        out_specs=(pl.BlockSpec((tm, D), lambda i, f: (i, 0)),
                   pl.BlockSpec((tm, D), lambda i, f: (i, 0))),
        scratch_shapes=[pltpu.VMEM((tm, D), F32), pltpu.VMEM((tm, D), BF16)],
        compiler_params=_cparams(("parallel", "arbitrary")),
    )(x2d, w_up, w_down, g, b)


def _permute_w_in(w_in):
    L, D, _ = w_in.shape
    w_fox = w_in[:, :, :FOX_COLS]
    w_rwkv = w_in[:, :, FOX_COLS:FOX_COLS + RWKV_COLS]
    w_ret = w_in[:, :, FOX_COLS + RWKV_COLS:]
    w_ff = w_fox[:, :, 3 * FOX_W:]
    pad = jnp.zeros((L, D, LANE - FOX_HEADS), w_in.dtype)
    w_q = w_fox[:, :, :3 * FOX_W].astype(BF16)
    w_main = jnp.concatenate([w_rwkv, w_ret, w_ff, pad], axis=-1).astype(BF16)
    return w_q, w_main


def _layer(x2d, xb, p, tables, B, S):
    hq = _in_proj(xb, p['w_q'], p['q_scale'])
    h = _in_proj(xb, p['w_in'])
    c = _fox_c(h, p['fox_bias'], B, S)
    c3 = c[:, :FOX_HEADS, :].reshape(B * FOX_HEADS, 1, S)
    y_fox = _fox_attention(hq, c3, B, S)
    rkv, lw, a, g = _rwkv_pre(h, p['mu'], p['w0'], p['w_up_lora'], p['a0'], p['a_up'], p['g_up'], S)
    y_rwkv = _rwkv_scan(rkv, lw, a, g, p['k_k'], p['k_a'], p['r_k'], p['rgn_w'], p['rgn_b'], B, S)
    y_ret = _retention(h, tables, p['tgn_w'], p['tgn_b'], B, S)
    x2d = _out_proj_ln(x2d, y_fox, y_rwkv, y_ret, p['wo_f'], p['wo_r'], p['wo_t'],
                       p['ln1_g'], p['ln1_b'])
    return _ffn_ln(x2d, p['w_up'], p['w_down'], p['ln2_g'], p['ln2_b'])


def kernel(x, w_in, fox_forget_bias, rwkv_mu, rwkv_w0, rwkv_w_up, rwkv_a0, rwkv_a_up, rwkv_g_up,
           rwkv_k_k, rwkv_k_a, rwkv_r_k, rwkv_gn_w, rwkv_gn_b, ret_gn_w, ret_gn_b, w_out, ln1_g,
           ln1_b, w_up, w_down, ln2_g, ln2_b):
    B, S, D = x.shape
    L = w_in.shape[0]
    w_q, w_main = _permute_w_in(w_in)
    q_scale = jnp.concatenate([jnp.full((1, FOX_W), FOX_Q_SCALE, F32),
                               jnp.ones((1, 2 * FOX_W), F32)], axis=1)
    w_out_b = w_out.astype(BF16)
    w_up_b = w_up.astype(BF16)
    w_down_b = w_down.astype(BF16)
    bias_pad = jnp.pad(fox_forget_bias, ((0, 0), (0, LANE - FOX_HEADS)))
    tables = _ret_tables(S)
    row = lambda t: t.reshape(1, -1)
    x2d = x.reshape(B * S, D)
    xb = x2d.astype(BF16)
    for l in range(L):
        p = {
            'w_in': w_main[l], 'w_q': w_q[l], 'q_scale': q_scale,
            'fox_bias': row(bias_pad[l]),
            'mu': row(rwkv_mu[l]), 'w0': row(rwkv_w0[l]), 'w_up_lora': rwkv_w_up[l].astype(BF16),
            'a0': row(rwkv_a0[l]), 'a_up': rwkv_a_up[l].astype(BF16),
            'g_up': rwkv_g_up[l].astype(BF16),
            'k_k': row(rwkv_k_k[l]), 'k_a': row(rwkv_k_a[l]), 'r_k': row(rwkv_r_k[l]),
            'rgn_w': row(rwkv_gn_w[l]), 'rgn_b': row(rwkv_gn_b[l]),
            'tgn_w': row(ret_gn_w[l]), 'tgn_b': row(ret_gn_b[l]),
            'wo_f': w_out_b[l, :FOX_W], 'wo_r': w_out_b[l, FOX_W:FOX_W + RWKV_W],
            'wo_t': w_out_b[l, FOX_W + RWKV_W:],
            'ln1_g': row(ln1_g[l]), 'ln1_b': row(ln1_b[l]),
            'w_up': w_up_b[l], 'w_down': w_down_b[l],
            'ln2_g': row(ln2_g[l]), 'ln2_b': row(ln2_b[l]),
        }
        x2d, xb = _layer(x2d, xb, p, tables, B, S)
    return x2d.reshape(B, S, D)
```

```python
import functools

import jax
import jax.numpy as jnp
from jax import lax
from jax.experimental import pallas as pl
from jax.experimental.pallas import tpu as pltpu

F32 = jnp.float32
BF16 = jnp.bfloat16

D_MODEL = 2048
DEPTH = 4
FOX_W = 768
RWKV_W = 640
RET_W = 640
FOX_HEAD_DIM = 128
FOX_HEADS = 6
RWKV_HEAD_DIM = 64
RWKV_HEADS = 10
RET_HEAD_DIM = 128
RET_HEADS = 5
DECAY_LORA = 64
ICLR_LORA = 64
GATE_LORA = 128
D_FF = 4 * D_MODEL
RET_CHUNK = 128
ROPE_BASE = 10000.0
LN_EPS = 1e-5
RWKV_GN_EPS = 64e-5
RET_GN_EPS = 1e-5
ALPHA = (2 * DEPTH) ** 0.25

FOX_COLS = 3 * FOX_W + FOX_HEADS
RWKV_COLS = 3 * RWKV_W + DECAY_LORA + ICLR_LORA + GATE_LORA
RET_COLS = 4 * RET_W

LANE = 128
H_FF0 = RWKV_COLS
LOG2E = 1.4426950408889634
FOX_Q_SCALE = FOX_HEAD_DIM ** -0.5 * LOG2E

RWKV_CHUNK = 64
RWKV_CHUNKS_PER_STEP = 4
NEG_BIG = -1e30

VMEM_LIMIT = 56 * 1024 * 1024


def _cparams(sem):
    return pltpu.CompilerParams(dimension_semantics=sem, vmem_limit_bytes=VMEM_LIMIT)


def _dot(a, b):
    return jnp.dot(a, b, preferred_element_type=F32)


def _dot_nt(a, b):
    return lax.dot_general(a, b, (((1,), (1,)), ((), ())), preferred_element_type=F32)


def _dot_tn(a, b):
    return lax.dot_general(a, b, (((0,), (0,)), ((), ())), preferred_element_type=F32)


def _split3(x):
    hi = x.astype(BF16)
    r1 = x - hi.astype(F32)
    mid = r1.astype(BF16)
    lo = (r1 - mid.astype(F32)).astype(BF16)
    return hi, mid, lo


def _split2(x):
    hi = x.astype(BF16)
    lo = (x - hi.astype(F32)).astype(BF16)
    return hi, lo


def _mask_dot(mask_bf16, x):
    hi, mid, lo = _split3(x)
    return _dot(mask_bf16, hi) + _dot(mask_bf16, mid) + _dot(mask_bf16, lo)


def _softplus(z):
    return jnp.maximum(z, 0.0) + jnp.log1p(jnp.exp(-jnp.abs(z)))


def _sigmoid(z):
    return 1.0 / (1.0 + jnp.exp(-z))


def _layer_norm_rows(y, g, b):
    mu = jnp.mean(y, axis=-1, keepdims=True)
    yc = y - mu
    var = jnp.mean(yc * yc, axis=-1, keepdims=True)
    return yc * lax.rsqrt(var + LN_EPS) * g + b


def _inproj_kernel(x_ref, w_ref, o_ref):
    o_ref[...] = _dot(x_ref[...], w_ref[...])


def _inproj_scaled_kernel(x_ref, w_ref, s_ref, o_ref):
    o_ref[...] = (_dot(x_ref[...], w_ref[...]) * s_ref[...]).astype(o_ref.dtype)


def _in_proj(xb, w, col_scale=None):
    M, K = xb.shape
    N = w.shape[1]
    tm = min(512, M)
    tn = N
    in_specs = [pl.BlockSpec((tm, K), lambda j, i: (i, 0)),
                pl.BlockSpec((K, tn), lambda j, i: (0, j))]
    args = [xb, w]
    if col_scale is not None:
        in_specs.append(pl.BlockSpec((1, tn), lambda j, i: (0, j)))
        args.append(col_scale)
    return pl.pallas_call(
        _inproj_kernel if col_scale is None else _inproj_scaled_kernel,
        name="in_proj",
        out_shape=jax.ShapeDtypeStruct((M, N), F32 if col_scale is None else BF16),
        grid=(N // tn, M // tm),
        in_specs=in_specs,
        out_specs=pl.BlockSpec((tm, tn), lambda j, i: (i, j)),
        compiler_params=_cparams(("parallel", "parallel")),
    )(*args)


def _fox_c_kernel(f_ref, bias_ref, c_ref, *, rows):
    S = f_ref.shape[0]
    r = lax.broadcasted_iota(jnp.int32, (rows, rows), 0)
    c = lax.broadcasted_iota(jnp.int32, (rows, rows), 1)
    tri = (c <= r).astype(BF16)

    def body(i, carry):
        start = pl.multiple_of(i * rows, rows)
        z = f_ref[pl.ds(start, rows), :] + bias_ref[...]
        ls = -_softplus(-z)
        cs = _mask_dot(tri, ls) + carry
        c_ref[:, pl.ds(start, rows)] = (cs * LOG2E).T[0:8, :]
        return cs[rows - 1:rows, :]

    lax.fori_loop(0, S // rows, body, jnp.zeros((1, LANE), F32))


def _fox_c(h, bias_pad, B, S):
    rows = min(256, S)
    return pl.pallas_call(
        functools.partial(_fox_c_kernel, rows=rows),
        name="fox_c",
        out_shape=jax.ShapeDtypeStruct((B, 8, S), F32),
        grid=(B,),
        in_specs=[pl.BlockSpec((S, LANE), lambda b: (b, H_FF0 // LANE)),
                  pl.BlockSpec((1, LANE), lambda b: (0, 0))],
        out_specs=pl.BlockSpec((None, 8, S), lambda b: (b, 0, 0)),
        compiler_params=_cparams(("parallel",)),
    )(h, bias_pad)


def _fox_attn_kernel(q_ref, k_ref, v_ref, c_ref, o_ref, s0_ref, *, tq, tk):
    i = pl.program_id(2)
    d = q_ref.shape[1]
    q0 = pl.multiple_of(i * tq, tq)
    c0 = c_ref[:, pl.ds(q0, tq)][:, 0:1]
    q = q_ref[...]

    def scores(start):
        bias = c0 - c_ref[:, pl.ds(start, tk)]
        return _dot_nt(q, k_ref[pl.ds(start, tk), :]) + bias

    ones = jnp.ones((tk, d), BF16)

    def update(start, sc, carry, masked):
        m, acc = carry
        if masked:
            row = lax.broadcasted_iota(jnp.int32, (tq, tk), 0)
            col = lax.broadcasted_iota(jnp.int32, (tq, tk), 1)
            sc = jnp.where(col <= row, sc, NEG_BIG)
        m_new = jnp.maximum(m, jnp.max(sc, axis=1, keepdims=True))
        alpha = jnp.exp2(m - m_new)
        p = jnp.exp2((sc - m_new).astype(BF16))
        vb = jnp.concatenate([v_ref[pl.ds(start, tk), :], ones], axis=1)
        acc = alpha * acc + _dot(p, vb)
        return m_new, acc

    def pair(t, carry):
        b0 = pl.multiple_of(2 * t * tk, tk)
        b1 = pl.multiple_of(b0 + tk, tk)
        s1 = scores(b1)
        carry = update(b0, s0_ref[...], carry, False)
        s0_ref[...] = scores(pl.multiple_of(b1 + tk, tk))
        return update(b1, s1, carry, False)

    s0_ref[...] = scores(0)
    init = (jnp.full((tq, 1), NEG_BIG, F32), jnp.zeros((tq, 2 * d), F32))
    carry = lax.fori_loop(0, i // 2, pair, init)

    def odd_tail(carry):
        s1 = scores(q0)
        carry = update(pl.multiple_of(q0 - tk, tk), s0_ref[...], carry, False)
        return update(q0, s1, carry, True)

    def even_tail(carry):
        return update(q0, s0_ref[...], carry, True)

    m, acc = lax.cond(i % 2 == 1, odd_tail, even_tail, carry)
    o_ref[...] = (acc[:, :d] / acc[:, d:]).astype(o_ref.dtype)


def _fox_attention(hq, c3, B, S):
    tq = tk = min(512, S)
    nq = S // tq
    nh = FOX_HEADS
    return pl.pallas_call(
        functools.partial(_fox_attn_kernel, tq=tq, tk=tk),
        name="fox_attn",
        scratch_shapes=[pltpu.VMEM((tq, tk), F32)],
        out_shape=jax.ShapeDtypeStruct((B * S, FOX_W), BF16),
        grid=(B, FOX_HEADS, nq),
        in_specs=[pl.BlockSpec((tq, LANE), lambda b, hd, i: (b * nq + i, hd)),
                  pl.BlockSpec((S, LANE), lambda b, hd, i: (b, nh + hd)),
                  pl.BlockSpec((S, LANE), lambda b, hd, i: (b, 2 * nh + hd)),
                  pl.BlockSpec((None, 1, S), lambda b, hd, i: (b * nh + hd, 0, 0))],
        out_specs=pl.BlockSpec((tq, LANE), lambda b, hd, i: (b * nq + i, hd)),
        compiler_params=_cparams(("parallel", "parallel", "arbitrary")),
    )(hq, hq, hq, c3)


def _rwkv_pre_kernel(h_ref, hprev_ref, mu_ref, w0_ref, wup_ref, a0_ref, aup_ref, gup_ref,
                     rkv_ref, lw_ref, a_ref, g_ref, *, tm, S):
    i = pl.program_id(0)
    h = h_ref[...]
    prev = hprev_ref[7:8, :]
    prev = jnp.where((i * tm) % S == 0, 0.0, prev)
    rows = lax.broadcasted_iota(jnp.int32, h.shape, 0)
    hp = jnp.where(rows == 0, prev, pltpu.roll(h, 1, 0))
    hs = h + (hp - h) * mu_ref[...]
    W3 = 3 * RWKV_W
    rkv_ref[...] = hs[:, :W3]
    wd = hs[:, W3:W3 + DECAY_LORA]
    ad = hs[:, W3 + DECAY_LORA:W3 + DECAY_LORA + ICLR_LORA]
    gd = hs[:, W3 + DECAY_LORA + ICLR_LORA:]
    z = w0_ref[...] + _dot(jnp.tanh(wd).astype(BF16), wup_ref[...])
    w = -_softplus(-z) - 0.5
    lw_ref[...] = -jnp.exp(w)
    a_ref[...] = _sigmoid(a0_ref[...] + _dot(ad.astype(BF16), aup_ref[...]))
    g_ref[...] = _dot(_sigmoid(gd).astype(BF16), gup_ref[...])


def _rwkv_pre(h, mu, w0, wup, a0, aup, gup, S):
    M = h.shape[0]
    tm = min(256, S)
    row = lambda i: (i, 0)
    const = lambda i: (0, 0)
    return pl.pallas_call(
        functools.partial(_rwkv_pre_kernel, tm=tm, S=S),
        name="rwkv_pre",
        out_shape=(jax.ShapeDtypeStruct((M, 3 * RWKV_W), F32),
                   jax.ShapeDtypeStruct((M, RWKV_W), F32),
                   jax.ShapeDtypeStruct((M, RWKV_W), F32),
                   jax.ShapeDtypeStruct((M, RWKV_W), F32)),
        grid=(M // tm,),
        in_specs=[pl.BlockSpec((tm, RWKV_COLS), row),
                  pl.BlockSpec((8, RWKV_COLS), lambda i: (jnp.maximum(i * (tm // 8) - 1, 0), 0)),
                  pl.BlockSpec((1, RWKV_COLS), const),
                  pl.BlockSpec((1, RWKV_W), const),
                  pl.BlockSpec((DECAY_LORA, RWKV_W), const),
                  pl.BlockSpec((1, RWKV_W), const),
                  pl.BlockSpec((ICLR_LORA, RWKV_W), const),
                  pl.BlockSpec((GATE_LORA, RWKV_W), const)],
        out_specs=(pl.BlockSpec((tm, 3 * RWKV_W), row),
                   pl.BlockSpec((tm, RWKV_W), row),
                   pl.BlockSpec((tm, RWKV_W), row),
                   pl.BlockSpec((tm, RWKV_W), row)),
        compiler_params=_cparams(("parallel",)),
    )(h, h, mu, w0, wup, a0, aup, gup)


def _head_sums(x, e_ref):
    hi, lo = _split2(x)
    e = e_ref[...]
    e_last = e[:LANE, :LANE]
    outs = []
    for c0 in range(0, RWKV_W, 2 * LANE):
        w = min(2 * LANE, RWKV_W - c0)
        ee = e if w == 2 * LANE else e_last
        outs.append(_dot(hi[:, c0:c0 + w], ee) + _dot(lo[:, c0:c0 + w], ee))
    return jnp.concatenate(outs, axis=1)


def _rwkv_scan_kernel(rkv_ref, lw_ref, a_ref, g_ref, kk_ref, ka_ref, rk_ref, gw_ref, gb_ref,
                      e_ref, o_ref, st_ref, *, nchunk):
    C = RWKV_CHUNK
    N = RWKV_HEAD_DIM
    T = nchunk * C

    @pl.when(pl.program_id(1) == 0)
    def _():
        st_ref[...] = jnp.zeros(st_ref.shape, F32)

    rkv = rkv_ref[...]
    r = rkv[:, :RWKV_W]
    k = rkv[:, RWKV_W:2 * RWKV_W]
    v = rkv[:, 2 * RWKV_W:]
    lw = lw_ref[...]
    a = a_ref[...]

    ri = lax.broadcasted_iota(jnp.int32, (C, C), 0)
    ci = lax.broadcasted_iota(jnp.int32, (C, C), 1)
    incl = ci <= ri
    strict = ci < ri

    rt = lax.broadcasted_iota(jnp.int32, (T, T), 0)
    ct = lax.broadcasted_iota(jnp.int32, (T, T), 1)
    tri = ((ct <= rt) & (ct >= (rt // C) * C)).astype(BF16)
    lw_hi, lw_lo = _split2(lw)
    cum = _dot(tri, lw_hi) + _dot(tri, lw_lo)
    cls = [cum[(c + 1) * C - 1:(c + 1) * C, :] for c in range(nchunk)]
    cl_rows = jnp.concatenate([jnp.broadcast_to(t, (C, RWKV_W)) for t in cls], axis=0)
    e_in = jnp.exp(cum)
    e_ex = jnp.exp(cum - lw)
    e_neg = jnp.exp(-cum)
    e_tail = jnp.exp(cl_rows - cum)
    pcs = [jnp.exp(t) for t in cls]

    kk = k * kk_ref[...]
    kkn = kk / jnp.maximum(jnp.sqrt(_head_sums(kk * kk, e_ref)), 1e-12)
    k2 = k * (1.0 + (a - 1.0) * ka_ref[...])
    bvec = kkn * a
    At_f = -kkn * e_ex
    Rt_f = r * e_in
    AtB = At_f.astype(BF16)
    RtB = Rt_f.astype(BF16)
    BtB = (bvec * e_neg).astype(BF16)
    KtB = (k2 * e_neg).astype(BF16)
    BpB = (bvec * e_tail).astype(BF16)
    KpB = (k2 * e_tail).astype(BF16)
    vB = v.astype(BF16)

    HD = range(RWKV_HEADS)
    hsl = [slice(hd * N, (hd + 1) * N) for hd in HD]
    sls = [(slice(c * C, (c + 1) * C), hs) for c in range(nchunk) for hs in hsl]
    At = [At_f[rs, hs] for rs, hs in sls]
    Rt = [Rt_f[rs, hs] for rs, hs in sls]
    vb = [vB[rs, hs] for rs, hs in sls]
    Bt = [BtB[rs, hs] for rs, hs in sls]
    Kt = [KtB[rs, hs] for rs, hs in sls]
    Bp = [BpB[rs, hs] for rs, hs in sls]
    Kp = [KpB[rs, hs] for rs, hs in sls]

    AR = [jnp.concatenate([AtB[rs, hs], RtB[rs, hs]], axis=0) for rs, hs in sls]
    BK = [jnp.concatenate([x, y], axis=0) for x, y in zip(Bt, Kt)]
    G = [_dot_nt(x, y) for x, y in zip(AR, BK)]
    r2 = lax.broadcasted_iota(jnp.int32, (C, 2 * C), 0)
    c2 = lax.broadcasted_iota(jnp.int32, (C, 2 * C), 1)
    incl2 = jnp.where(c2 >= C, c2 - C, c2) <= r2
    P = [jnp.where(strict, t[:C, :C], 0.0) for t in G]
    Lak = [jnp.where(strict, t[:C, C:], 0.0).astype(BF16) for t in G]
    L2 = [jnp.where(incl2, t[C:], 0.0).astype(BF16) for t in G]

    X = [jnp.concatenate([x, _dot(y, z)], axis=1) for x, y, z in zip(At, Lak, vb)]
    steps = C.bit_length() - 1
    for it in range(steps):
        Pb = [t.astype(BF16) for t in P]
        Xb = [x.astype(BF16) for x in X]
        if it + 1 < steps:
            PX = [_dot(p, jnp.concatenate([x, p], axis=1)) for p, x in zip(Pb, Xb)]
            X = [x + t[:, :2 * N] for x, t in zip(X, PX)]
            P = [t[:, 2 * N:] for t in PX]
        else:
            X = [x + _dot(p, xb) for x, p, xb in zip(X, Pb, Xb)]
    Xb = [x.astype(BF16) for x in X]

    zpad = jnp.zeros((C, N), BF16)
    RHS = [jnp.concatenate([x, jnp.concatenate([zpad, z], axis=1)], axis=0)
           for x, z in zip(Xb, vb)]
    LX = [_dot(x, y) for x, y in zip(L2, RHS)]
    Qp = [(x + y[:, :N]).astype(BF16) for x, y in zip(Rt, LX)]
    Y0 = [y[:, N:] for y in LX]

    XB = [_dot_tn(x, b) for x, b in zip(Xb, Bp)]
    Mw = [t[:N].astype(BF16) for t in XB]
    Nn = [t[N:] + _dot_tn(x, y) for t, x, y in zip(XB, vb, Kp)]

    S = [st_ref[hd] for hd in HD]
    y_rows = []
    for c in range(nchunk):
        ch = [c * RWKV_HEADS + hd for hd in HD]
        Ssp = [_split2(s) for s in S]
        ys = [_dot_nt(Qp[j], sp[0]) + Y0[j] for j, sp in zip(ch, Ssp)]
        y_rows.append(jnp.concatenate(ys, axis=1))
        S = [s * pcs[c][:, hs] + _dot(sp[0], Mw[j]) + _dot(sp[1], Mw[j]) + Nn[j]
             for s, sp, j, hs in zip(S, Ssp, ch, hsl)]
    for hd in HD:
        st_ref[hd] = S[hd]

    y = jnp.concatenate(y_rows, axis=0)
    inv_n = 1.0 / N
    yc = y - _head_sums(y, e_ref) * inv_n
    var = _head_sums(yc * yc, e_ref) * inv_n
    yn = yc * lax.rsqrt(var + RWKV_GN_EPS) * gw_ref[...] + gb_ref[...]
    bonus = _head_sums(r * k2 * rk_ref[...], e_ref) * v
    o_ref[...] = ((yn + bonus) * g_ref[...]).astype(o_ref.dtype)


def _rwkv_scan(rkv, lw, a, g, k_k, k_a, r_k, gn_w, gn_b, B, S):
    nchunk = RWKV_CHUNKS_PER_STEP
    T = nchunk * RWKV_CHUNK
    nc = S // T
    row = lambda b, c: (b * nc + c, 0)
    const = lambda b, c: (0, 0)
    vec = pl.BlockSpec((1, RWKV_W), const)
    hid = jnp.arange(2 * LANE, dtype=jnp.int32) // RWKV_HEAD_DIM
    head_ones = (hid[:, None] == hid[None, :]).astype(BF16)
    return pl.pallas_call(
        functools.partial(_rwkv_scan_kernel, nchunk=nchunk),
        name="rwkv_scan",
        out_shape=jax.ShapeDtypeStruct((B * S, RWKV_W), BF16),
        grid=(B, nc),
        in_specs=[pl.BlockSpec((T, 3 * RWKV_W), row),
                  pl.BlockSpec((T, RWKV_W), row),
                  pl.BlockSpec((T, RWKV_W), row),
                  pl.BlockSpec((T, RWKV_W), row),
                  vec, vec, vec, vec, vec,
                  pl.BlockSpec((2 * LANE, 2 * LANE), const)],
        out_specs=pl.BlockSpec((T, RWKV_W), row),
        scratch_shapes=[pltpu.VMEM((RWKV_HEADS, RWKV_HEAD_DIM, RWKV_HEAD_DIM), F32)],
        compiler_params=_cparams(("parallel", "arbitrary")),
    )(rkv, lw, a, g, k_k, k_a, r_k, gn_w, gn_b, head_ones)


def _ret_kernel(h_ref, cos_ref, sin_ref, dmat_ref, qd_ref, kd_ref, cd_ref, gw_ref, gb_ref,
                o_ref, R_ref, *, nchunk):
    C = RET_CHUNK
    d = RET_HEAD_DIM
    H = RET_HEADS
    W = RET_W

    @pl.when(pl.program_id(1) == 0)
    def _():
        R_ref[...] = jnp.zeros(R_ref.shape, F32)

    pairs = [(c, hd) for c in range(nchunk) for hd in range(H)]
    rows = lambda c: slice(c * C, (c + 1) * C)
    cols = lambda part, hd: slice(part * W + hd * d, part * W + (hd + 1) * d)
    cos = [cos_ref[rows(c), :] for c in range(nchunk)]
    sin = [sin_ref[rows(c), :] for c in range(nchunk)]
    q = [h_ref[rows(c), cols(0, hd)] for c, hd in pairs]
    k = [h_ref[rows(c), cols(1, hd)] for c, hd in pairs]
    qr = [x * cos[c] + pltpu.roll(x, d // 2, 1) * sin[c] for x, (c, hd) in zip(q, pairs)]
    kr = [(x * cos[c] + pltpu.roll(x, d // 2, 1) * sin[c]) * (d ** -0.5)
          for x, (c, hd) in zip(k, pairs)]
    vb = [h_ref[rows(c), cols(2, hd)].astype(BF16) for c, hd in pairs]
    qb = [x.astype(BF16) for x in qr]
    kb = [x.astype(BF16) for x in kr]
    kdb = [(x * kd_ref[hd]).astype(BF16) for x, (c, hd) in zip(kr, pairs)]
    inner = [(_dot_nt(x, y) * dmat_ref[hd]).astype(BF16) for x, y, (c, hd) in zip(qb, kb, pairs)]
    o_in = [_dot(x, y) for x, y in zip(inner, vb)]
    kv = [_dot_tn(x, y) for x, y in zip(kdb, vb)]

    R = [R_ref[hd] for hd in range(H)]
    outs = []
    for c in range(nchunk):
        ch = [c * H + hd for hd in range(H)]
        Rsp = [_split2(x) for x in R]
        outs += [o_in[j] + (_dot(qb[j], sp[0]) + _dot(qb[j], sp[1])) * qd_ref[hd]
                 for hd, (j, sp) in enumerate(zip(ch, Rsp))]
        R = [cd_ref[hd][0:1, :] * x + kv[j] for hd, (x, j) in enumerate(zip(R, ch))]
    for hd in range(H):
        R_ref[hd] = R[hd]

    for o, (c, hd) in zip(outs, pairs):
        mu = jnp.mean(o, axis=1, keepdims=True)
        oc = o - mu
        var = jnp.mean(oc * oc, axis=1, keepdims=True)
        gsl = slice(hd * d, (hd + 1) * d)
        on = oc * lax.rsqrt(var + RET_GN_EPS) * gw_ref[:, gsl] + gb_ref[:, gsl]
        gate = h_ref[rows(c), cols(3, hd)]
        o_ref[rows(c), gsl] = (gate * _sigmoid(gate) * on).astype(o_ref.dtype)


def _ret_tables(S):
    C, d, H = RET_CHUNK, RET_HEAD_DIM, RET_HEADS
    half = d // 2
    inv = 1.0 / (ROPE_BASE ** (jnp.arange(half, dtype=F32) / half))
    ang = jnp.arange(S, dtype=F32)[:, None] * inv[None, :]
    cos = jnp.cos(ang)
    sin = jnp.sin(ang)
    cos2 = jnp.concatenate([cos, cos], axis=1)
    sin2 = jnp.concatenate([-sin, sin], axis=1)
    log_g = jnp.log(1.0 - 2.0 ** (-5.0 - jnp.arange(H, dtype=F32)))
    pos = jnp.arange(C, dtype=F32)
    rel = pos[:, None] - pos[None, :]
    dmat = jnp.where(rel >= 0, jnp.exp(log_g[:, None, None] * jnp.maximum(rel, 0.0)), 0.0)
    qd = jnp.broadcast_to(jnp.exp(log_g[:, None] * (pos + 1.0))[..., None], (H, C, d))
    kd = jnp.broadcast_to(jnp.exp(log_g[:, None] * (C - 1.0 - pos))[..., None], (H, C, d))
    cd = jnp.broadcast_to(jnp.exp(log_g * C)[:, None, None], (H, 8, d))
    return cos2, sin2, dmat, qd, kd, cd


def _retention(h, tables, gn_w, gn_b, B, S):
    cos2, sin2, dmat, qd, kd, cd = tables
    C = RET_CHUNK
    T = min(512, S)
    nt = S // T
    H = RET_HEADS
    tab = pl.BlockSpec((T, LANE), lambda b, t: (t, 0))
    per_head = lambda rows: pl.BlockSpec((H, rows, LANE), lambda b, t: (0, 0, 0))
    vec = pl.BlockSpec((1, RET_W), lambda b, t: (0, 0))
    return pl.pallas_call(
        functools.partial(_ret_kernel, nchunk=T // C),
        name="retention",
        out_shape=jax.ShapeDtypeStruct((B * S, RET_W), BF16),
        grid=(B, nt),
        in_specs=[pl.BlockSpec((T, RET_COLS), lambda b, t: (b * nt + t, 0)),
                  tab, tab, per_head(C), per_head(C), per_head(C), per_head(8), vec, vec],
        out_specs=pl.BlockSpec((T, RET_W), lambda b, t: (b * nt + t, 0)),
        scratch_shapes=[pltpu.VMEM((H, RET_HEAD_DIM, RET_HEAD_DIM), F32)],
        compiler_params=_cparams(("parallel", "arbitrary")),
    )(h, cos2, sin2, dmat, qd, kd, cd, gn_w, gn_b)


def _outproj_kernel(x_ref, yf_ref, yr_ref, yt_ref, wf_ref, wr_ref, wt_ref, g_ref, b_ref, o_ref):
    y = (_dot(yf_ref[...], wf_ref[...]) + _dot(yr_ref[...], wr_ref[...])
         + _dot(yt_ref[...], wt_ref[...]))
    o_ref[...] = _layer_norm_rows(ALPHA * x_ref[...] + y, g_ref[...], b_ref[...])


def _out_proj_ln(x2d, y_fox, y_rwkv, y_ret, w_f, w_r, w_t, g, b):
    M, D = x2d.shape
    tm = min(512, M)
    row = lambda i: (i, 0)
    const = lambda i: (0, 0)
    return pl.pallas_call(
        _outproj_kernel,
        name="out_proj_ln",
        out_shape=jax.ShapeDtypeStruct((M, D), F32),
        grid=(M // tm,),
        in_specs=[pl.BlockSpec((tm, D), row), pl.BlockSpec((tm, FOX_W), row),
                  pl.BlockSpec((tm, RWKV_W), row), pl.BlockSpec((tm, RET_W), row),
                  pl.BlockSpec((FOX_W, D), const), pl.BlockSpec((RWKV_W, D), const),
                  pl.BlockSpec((RET_W, D), const),
                  pl.BlockSpec((1, D), const), pl.BlockSpec((1, D), const)],
        out_specs=pl.BlockSpec((tm, D), row),
        compiler_params=_cparams(("parallel",)),
    )(x2d, y_fox, y_rwkv, y_ret, w_f, w_r, w_t, g, b)


def _ffn_kernel(x_ref, wu_ref, wd_ref, g_ref, b_ref, o_ref, ob_ref, acc_ref, xb_ref):
    f = pl.program_id(1)

    @pl.when(f == 0)
    def _():
        xb_ref[...] = x_ref[...].astype(BF16)
        acc_ref[...] = jnp.zeros(acc_ref.shape, F32)

    hid = jnp.maximum(_dot(xb_ref[...], wu_ref[...]), 0.0)
    acc_ref[...] += _dot((hid * hid).astype(BF16), wd_ref[...])

    @pl.when(f == pl.num_programs(1) - 1)
    def _():
        y = _layer_norm_rows(ALPHA * x_ref[...] + acc_ref[...], g_ref[...], b_ref[...])
        o_ref[...] = y
        ob_ref[...] = y.astype(BF16)


def _ffn_ln(x2d, w_up, w_down, g, b):
    M, D = x2d.shape
    F = w_up.shape[1]
    tm = min(512, M)
    tf = 1024
    return pl.pallas_call(
        _ffn_kernel,
        name="ffn_ln",
        out_shape=(jax.ShapeDtypeStruct((M, D), F32), jax.ShapeDtypeStruct((M, D), BF16)),
        grid=(M // tm, F // tf),
        in_specs=[pl.BlockSpec((tm, D), lambda i, f: (i, 0)),
                  pl.BlockSpec((D, tf), lambda i, f: (0, f)),
                  pl.BlockSpec((tf, D), lambda i, f: (f, 0)),
                  pl.BlockSpec((1, D), lambda i, f: (0, 0)),
                  pl.BlockSpec((1, D), lambda i, f: (0, 0))],
        out_specs=(pl.BlockSpec((tm, D), lambda i, f: (i, 0)),
                   pl.BlockSpec((tm, D), lambda i, f: (i, 0))),
        scratch_shapes=[pltpu.VMEM((tm, D), F32), pltpu.VMEM((tm, D), BF16)],
        compiler_params=_cparams(("parallel", "arbitrary")),
    )(x2d, w_up, w_down, g, b)


def _permute_w_in(w_in):
    L, D, _ = w_in.shape
    w_fox = w_in[:, :, :FOX_COLS]
    w_rwkv = w_in[:, :, FOX_COLS:FOX_COLS + RWKV_COLS]
    w_ret = w_in[:, :, FOX_COLS + RWKV_COLS:]
    w_ff = w_fox[:, :, 3 * FOX_W:]
    pad = jnp.zeros((L, D, LANE - FOX_HEADS), w_in.dtype)
    w_q = w_fox[:, :, :3 * FOX_W].astype(BF16)
    w_rw = jnp.concatenate([w_rwkv, w_ff, pad], axis=-1).astype(BF16)
    return w_q, w_rw, w_ret.astype(BF16)


def _layer(x2d, xb, p, tables, B, S):
    hq = _in_proj(xb, p['w_q'], p['q_scale'])
    h = _in_proj(xb, p['w_rw'])
    ht = _in_proj(xb, p['w_ret'])
    c = _fox_c(h, p['fox_bias'], B, S)
    c3 = c[:, :FOX_HEADS, :].reshape(B * FOX_HEADS, 1, S)
    y_fox = _fox_attention(hq, c3, B, S)
    rkv, lw, a, g = _rwkv_pre(h, p['mu'], p['w0'], p['w_up_lora'], p['a0'], p['a_up'], p['g_up'], S)
    y_rwkv = _rwkv_scan(rkv, lw, a, g, p['k_k'], p['k_a'], p['r_k'], p['rgn_w'], p['rgn_b'], B, S)
    y_ret = _retention(ht, tables, p['tgn_w'], p['tgn_b'], B, S)
    x2d = _out_proj_ln(x2d, y_fox, y_rwkv, y_ret, p['wo_f'], p['wo_r'], p['wo_t'],
                       p['ln1_g'], p['ln1_b'])
    return _ffn_ln(x2d, p['w_up'], p['w_down'], p['ln2_g'], p['ln2_b'])


def kernel(x, w_in, fox_forget_bias, rwkv_mu, rwkv_w0, rwkv_w_up, rwkv_a0, rwkv_a_up, rwkv_g_up,
           rwkv_k_k, rwkv_k_a, rwkv_r_k, rwkv_gn_w, rwkv_gn_b, ret_gn_w, ret_gn_b, w_out, ln1_g,
           ln1_b, w_up, w_down, ln2_g, ln2_b):
    B, S, D = x.shape
    L = w_in.shape[0]
    w_q, w_rw, w_ret = _permute_w_in(w_in)
    q_scale = jnp.concatenate([jnp.full((1, FOX_W), FOX_Q_SCALE, F32),
                               jnp.ones((1, 2 * FOX_W), F32)], axis=1)
    w_out_b = w_out.astype(BF16)
    w_up_b = w_up.astype(BF16)
    w_down_b = w_down.astype(BF16)
    bias_pad = jnp.pad(fox_forget_bias, ((0, 0), (0, LANE - FOX_HEADS)))
    tables = _ret_tables(S)
    row = lambda t: t.reshape(1, -1)
    x2d = x.reshape(B * S, D)
    xb = x2d.astype(BF16)
    for l in range(L):
        p = {
            'w_rw': w_rw[l], 'w_ret': w_ret[l], 'w_q': w_q[l], 'q_scale': q_scale,
            'fox_bias': row(bias_pad[l]),
            'mu': row(rwkv_mu[l]), 'w0': row(rwkv_w0[l]), 'w_up_lora': rwkv_w_up[l].astype(BF16),
            'a0': row(rwkv_a0[l]), 'a_up': rwkv_a_up[l].astype(BF16),
            'g_up': rwkv_g_up[l].astype(BF16),
            'k_k': row(rwkv_k_k[l]), 'k_a': row(rwkv_k_a[l]), 'r_k': row(rwkv_r_k[l]),
            'rgn_w': row(rwkv_gn_w[l]), 'rgn_b': row(rwkv_gn_b[l]),
            'tgn_w': row(ret_gn_w[l]), 'tgn_b': row(ret_gn_b[l]),
            'wo_f': w_out_b[l, :FOX_W], 'wo_r': w_out_b[l, FOX_W:FOX_W + RWKV_W],
            'wo_t': w_out_b[l, FOX_W + RWKV_W:],
            'ln1_g': row(ln1_g[l]), 'ln1_b': row(ln1_b[l]),
            'w_up': w_up_b[l], 'w_down': w_down_b[l],
            'ln2_g': row(ln2_g[l]), 'ln2_b': row(ln2_b[l]),
        }
        x2d, xb = _layer(x2d, xb, p, tables, B, S)
    return x2d.reshape(B, S, D)
```

```python
import functools

import jax
import jax.numpy as jnp
from jax import lax
from jax.experimental import pallas as pl
from jax.experimental.pallas import tpu as pltpu

F32 = jnp.float32
BF16 = jnp.bfloat16

D_MODEL = 2048
DEPTH = 4
FOX_W = 768
RWKV_W = 640
RET_W = 640
FOX_HEAD_DIM = 128
FOX_HEADS = 6
RWKV_HEAD_DIM = 64
RWKV_HEADS = 10
RET_HEAD_DIM = 128
RET_HEADS = 5
DECAY_LORA = 64
ICLR_LORA = 64
GATE_LORA = 128
D_FF = 4 * D_MODEL
RET_CHUNK = 128
ROPE_BASE = 10000.0
LN_EPS = 1e-5
RWKV_GN_EPS = 64e-5
RET_GN_EPS = 1e-5
ALPHA = (2 * DEPTH) ** 0.25

FOX_COLS = 3 * FOX_W + FOX_HEADS
RWKV_COLS = 3 * RWKV_W + DECAY_LORA + ICLR_LORA + GATE_LORA
RET_COLS = 4 * RET_W

LANE = 128
H_FF0 = RWKV_COLS
LOG2E = 1.4426950408889634
FOX_Q_SCALE = FOX_HEAD_DIM ** -0.5 * LOG2E

FOX_BLOCKS_PER_TRIP = 4
RWKV_CHUNK = 64
RWKV_CHUNKS_PER_STEP = 4
NEG_BIG = -1e30

VMEM_LIMIT = 56 * 1024 * 1024


def _cparams(sem):
    return pltpu.CompilerParams(dimension_semantics=sem, vmem_limit_bytes=VMEM_LIMIT)


def _dot(a, b):
    return jnp.dot(a, b, preferred_element_type=F32)


def _dot_nt(a, b):
    return lax.dot_general(a, b, (((1,), (1,)), ((), ())), preferred_element_type=F32)


def _dot_tn(a, b):
    return lax.dot_general(a, b, (((0,), (0,)), ((), ())), preferred_element_type=F32)


def _split3(x):
    hi = x.astype(BF16)
    r1 = x - hi.astype(F32)
    mid = r1.astype(BF16)
    lo = (r1 - mid.astype(F32)).astype(BF16)
    return hi, mid, lo


def _split2(x):
    hi = x.astype(BF16)
    lo = (x - hi.astype(F32)).astype(BF16)
    return hi, lo


def _mask_dot(mask_bf16, x):
    hi, mid, lo = _split3(x)
    return _dot(mask_bf16, hi) + _dot(mask_bf16, mid) + _dot(mask_bf16, lo)


def _softplus(z):
    return jnp.maximum(z, 0.0) + jnp.log1p(jnp.exp(-jnp.abs(z)))


def _sigmoid(z):
    return 1.0 / (1.0 + jnp.exp(-z))


def _layer_norm_rows(y, g, b):
    mu = jnp.mean(y, axis=-1, keepdims=True)
    yc = y - mu
    var = jnp.mean(yc * yc, axis=-1, keepdims=True)
    return yc * lax.rsqrt(var + LN_EPS) * g + b


def _inproj_kernel(x_ref, w_ref, o_ref):
    o_ref[...] = _dot(x_ref[...], w_ref[...])


def _inproj_scaled_kernel(x_ref, w_ref, s_ref, o_ref):
    o_ref[...] = (_dot(x_ref[...], w_ref[...]) * s_ref[...]).astype(o_ref.dtype)


def _in_proj(xb, w, l, col_scale=None):
    M, K = xb.shape
    N = w.shape[2]
    tm = min(512, M)
    tn = N
    in_specs = [pl.BlockSpec((tm, K), lambda j, i: (i, 0)),
                pl.BlockSpec((None, K, tn), lambda j, i: (l, 0, j))]
    args = [xb, w]
    if col_scale is not None:
        in_specs.append(pl.BlockSpec((1, tn), lambda j, i: (0, j)))
        args.append(col_scale)
    return pl.pallas_call(
        _inproj_kernel if col_scale is None else _inproj_scaled_kernel,
        name="in_proj",
        out_shape=jax.ShapeDtypeStruct((M, N), F32 if col_scale is None else BF16),
        grid=(N // tn, M // tm),
        in_specs=in_specs,
        out_specs=pl.BlockSpec((tm, tn), lambda j, i: (i, j)),
        compiler_params=_cparams(("parallel", "parallel")),
    )(*args)


def _fox_c_kernel(f_ref, bias_ref, c_ref, *, rows):
    S = f_ref.shape[0]
    r = lax.broadcasted_iota(jnp.int32, (rows, rows), 0)
    c = lax.broadcasted_iota(jnp.int32, (rows, rows), 1)
    tri = (c <= r).astype(BF16)

    def body(i, carry):
        start = pl.multiple_of(i * rows, rows)
        z = f_ref[pl.ds(start, rows), :] + bias_ref[...]
        ls = -_softplus(-z)
        cs = _mask_dot(tri, ls) + carry
        c_ref[:, pl.ds(start, rows)] = (cs * LOG2E).T[0:8, :]
        return cs[rows - 1:rows, :]

    lax.fori_loop(0, S // rows, body, jnp.zeros((1, LANE), F32))


def _fox_c(h, bias_pad, B, S):
    rows = min(256, S)
    return pl.pallas_call(
        functools.partial(_fox_c_kernel, rows=rows),
        name="fox_c",
        out_shape=jax.ShapeDtypeStruct((B, 8, S), F32),
        grid=(B,),
        in_specs=[pl.BlockSpec((S, LANE), lambda b: (b, H_FF0 // LANE)),
                  pl.BlockSpec((1, LANE), lambda b: (0, 0))],
        out_specs=pl.BlockSpec((None, 8, S), lambda b: (b, 0, 0)),
        compiler_params=_cparams(("parallel",)),
    )(h, bias_pad)


def _fox_attn_kernel(q_ref, k_ref, v_ref, c_ref, o_ref, s0_ref, *, tq, tk):
    i = pl.program_id(2)
    d = q_ref.shape[1]
    q0 = pl.multiple_of(i * tq, tq)
    c0 = c_ref[:, pl.ds(q0, tq)][:, 0:1]
    q = q_ref[...]

    def scores(start):
        bias = c0 - c_ref[:, pl.ds(start, tk)]
        return _dot_nt(q, k_ref[pl.ds(start, tk), :]) + bias

    ones = jnp.ones((tk, d), BF16)

    def update(start, sc, carry, masked):
        m, acc = carry
        if masked:
            row = lax.broadcasted_iota(jnp.int32, (tq, tk), 0)
            col = lax.broadcasted_iota(jnp.int32, (tq, tk), 1)
            sc = jnp.where(col <= row, sc, NEG_BIG)
        m_new = jnp.maximum(m, jnp.max(sc, axis=1, keepdims=True))
        alpha = jnp.exp2(m - m_new)
        p = jnp.exp2((sc - m_new).astype(BF16))
        vb = jnp.concatenate([v_ref[pl.ds(start, tk), :], ones], axis=1)
        acc = alpha * acc + _dot(p, vb)
        return m_new, acc

    U = FOX_BLOCKS_PER_TRIP

    def walk(base, n, carry, last_masked):
        s_cur = s0_ref[...]
        for u in range(n):
            start = pl.multiple_of(base + u * tk, tk)
            final = last_masked and u == n - 1
            s_next = None if final else scores(pl.multiple_of(start + tk, tk))
            carry = update(start, s_cur, carry, final)
            s_cur = s_next
        if not last_masked:
            s0_ref[...] = s_cur
        return carry

    s0_ref[...] = scores(0)
    init = (jnp.full((tq, 1), NEG_BIG, F32), jnp.zeros((tq, 2 * d), F32))
    trips = i // U
    carry = lax.fori_loop(
        0, trips, lambda t, c: walk(pl.multiple_of(t * (U * tk), tk), U, c, False), init)

    base = pl.multiple_of(trips * (U * tk), tk)
    tails = [functools.partial(walk, base, r + 1, last_masked=True) for r in range(U)]
    m, acc = lax.switch(i - trips * U, tails, carry)
    o_ref[...] = (acc[:, :d] / acc[:, d:]).astype(o_ref.dtype)


def _fox_attention(hq, c3, B, S):
    tq = tk = min(512, S)
    nq = S // tq
    nh = FOX_HEADS
    return pl.pallas_call(
        functools.partial(_fox_attn_kernel, tq=tq, tk=tk),
        name="fox_attn",
        scratch_shapes=[pltpu.VMEM((tq, tk), F32)],
        out_shape=jax.ShapeDtypeStruct((B * S, FOX_W), BF16),
        grid=(B, FOX_HEADS, nq),
        in_specs=[pl.BlockSpec((tq, LANE), lambda b, hd, i: (b * nq + i, hd)),
                  pl.BlockSpec((S, LANE), lambda b, hd, i: (b, nh + hd)),
                  pl.BlockSpec((S, LANE), lambda b, hd, i: (b, 2 * nh + hd)),
                  pl.BlockSpec((None, 1, S), lambda b, hd, i: (b * nh + hd, 0, 0))],
        out_specs=pl.BlockSpec((tq, LANE), lambda b, hd, i: (b * nq + i, hd)),
        compiler_params=_cparams(("parallel", "parallel", "arbitrary")),
    )(hq, hq, hq, c3)


def _rwkv_pre_kernel(h_ref, hprev_ref, mu_ref, w0_ref, wup_ref, a0_ref, aup_ref, gup_ref,
                     rkv_ref, lw_ref, a_ref, g_ref, *, tm, S):
    i = pl.program_id(0)
    h = h_ref[...]
    prev = hprev_ref[7:8, :]
    prev = jnp.where((i * tm) % S == 0, 0.0, prev)
    rows = lax.broadcasted_iota(jnp.int32, h.shape, 0)
    hp = jnp.where(rows == 0, prev, pltpu.roll(h, 1, 0))
    hs = h + (hp - h) * mu_ref[...]
    W3 = 3 * RWKV_W
    rkv_ref[...] = hs[:, :W3]
    wd = hs[:, W3:W3 + DECAY_LORA]
    ad = hs[:, W3 + DECAY_LORA:W3 + DECAY_LORA + ICLR_LORA]
    gd = hs[:, W3 + DECAY_LORA + ICLR_LORA:]
    z = w0_ref[...] + _dot(jnp.tanh(wd).astype(BF16), wup_ref[...])
    w = -_softplus(-z) - 0.5
    lw_ref[...] = -jnp.exp(w)
    a_ref[...] = _sigmoid(a0_ref[...] + _dot(ad.astype(BF16), aup_ref[...]))
    g_ref[...] = _dot(_sigmoid(gd).astype(BF16), gup_ref[...])


def _rwkv_pre(h, mu, w0, wup, a0, aup, gup, S):
    M = h.shape[0]
    tm = min(256, S)
    row = lambda i: (i, 0)
    const = lambda i: (0, 0)
    return pl.pallas_call(
        functools.partial(_rwkv_pre_kernel, tm=tm, S=S),
        name="rwkv_pre",
        out_shape=(jax.ShapeDtypeStruct((M, 3 * RWKV_W), F32),
                   jax.ShapeDtypeStruct((M, RWKV_W), F32),
                   jax.ShapeDtypeStruct((M, RWKV_W), F32),
                   jax.ShapeDtypeStruct((M, RWKV_W), F32)),
        grid=(M // tm,),
        in_specs=[pl.BlockSpec((tm, RWKV_COLS), row),
                  pl.BlockSpec((8, RWKV_COLS), lambda i: (jnp.maximum(i * (tm // 8) - 1, 0), 0)),
                  pl.BlockSpec((1, RWKV_COLS), const),
                  pl.BlockSpec((1, RWKV_W), const),
                  pl.BlockSpec((DECAY_LORA, RWKV_W), const),
                  pl.BlockSpec((1, RWKV_W), const),
                  pl.BlockSpec((ICLR_LORA, RWKV_W), const),
                  pl.BlockSpec((GATE_LORA, RWKV_W), const)],
        out_specs=(pl.BlockSpec((tm, 3 * RWKV_W), row),
                   pl.BlockSpec((tm, RWKV_W), row),
                   pl.BlockSpec((tm, RWKV_W), row),
                   pl.BlockSpec((tm, RWKV_W), row)),
        compiler_params=_cparams(("parallel",)),
    )(h, h, mu, w0, wup, a0, aup, gup)


def _head_sums(x, e_ref):
    hi, lo = _split2(x)
    e = e_ref[...]
    e_last = e[:LANE, :LANE]
    outs = []
    for c0 in range(0, RWKV_W, 2 * LANE):
        w = min(2 * LANE, RWKV_W - c0)
        ee = e if w == 2 * LANE else e_last
        outs.append(_dot(hi[:, c0:c0 + w], ee) + _dot(lo[:, c0:c0 + w], ee))
    return jnp.concatenate(outs, axis=1)


def _rwkv_scan_kernel(rkv_ref, lw_ref, a_ref, g_ref, kk_ref, ka_ref, rk_ref, gw_ref, gb_ref,
                      e_ref, o_ref, st_ref, *, nchunk):
    C = RWKV_CHUNK
    N = RWKV_HEAD_DIM
    T = nchunk * C

    @pl.when(pl.program_id(1) == 0)
    def _():
        st_ref[...] = jnp.zeros(st_ref.shape, F32)

    rkv = rkv_ref[...]
    r = rkv[:, :RWKV_W]
    k = rkv[:, RWKV_W:2 * RWKV_W]
    v = rkv[:, 2 * RWKV_W:]
    lw = lw_ref[...]
    a = a_ref[...]

    ri = lax.broadcasted_iota(jnp.int32, (C, C), 0)
    ci = lax.broadcasted_iota(jnp.int32, (C, C), 1)
    incl = ci <= ri
    strict = ci < ri

    rt = lax.broadcasted_iota(jnp.int32, (T, T), 0)
    ct = lax.broadcasted_iota(jnp.int32, (T, T), 1)
    tri = ((ct <= rt) & (ct >= (rt // C) * C)).astype(BF16)
    lw_hi, lw_lo = _split2(lw)
    cum = _dot(tri, lw_hi) + _dot(tri, lw_lo)
    cls = [cum[(c + 1) * C - 1:(c + 1) * C, :] for c in range(nchunk)]
    cl_rows = jnp.concatenate([jnp.broadcast_to(t, (C, RWKV_W)) for t in cls], axis=0)
    e_in = jnp.exp(cum)
    e_ex = jnp.exp(cum - lw)
    e_neg = jnp.exp(-cum)
    e_tail = jnp.exp(cl_rows - cum)
    pcs = [jnp.exp(t) for t in cls]

    kk = k * kk_ref[...]
    kkn = kk / jnp.maximum(jnp.sqrt(_head_sums(kk * kk, e_ref)), 1e-12)
    k2 = k * (1.0 + (a - 1.0) * ka_ref[...])
    bvec = kkn * a
    At_f = -kkn * e_ex
    Rt_f = r * e_in
    AtB = At_f.astype(BF16)
    RtB = Rt_f.astype(BF16)
    BtB = (bvec * e_neg).astype(BF16)
    KtB = (k2 * e_neg).astype(BF16)
    BpB = (bvec * e_tail).astype(BF16)
    KpB = (k2 * e_tail).astype(BF16)
    vB = v.astype(BF16)

    HD = range(RWKV_HEADS)
    hsl = [slice(hd * N, (hd + 1) * N) for hd in HD]
    sls = [(slice(c * C, (c + 1) * C), hs) for c in range(nchunk) for hs in hsl]
    At = [At_f[rs, hs] for rs, hs in sls]
    Rt = [Rt_f[rs, hs] for rs, hs in sls]
    vb = [vB[rs, hs] for rs, hs in sls]
    Bt = [BtB[rs, hs] for rs, hs in sls]
    Kt = [KtB[rs, hs] for rs, hs in sls]
    Bp = [BpB[rs, hs] for rs, hs in sls]
    Kp = [KpB[rs, hs] for rs, hs in sls]

    AR = [jnp.concatenate([AtB[rs, hs], RtB[rs, hs]], axis=0) for rs, hs in sls]
    BK = [jnp.concatenate([x, y], axis=0) for x, y in zip(Bt, Kt)]
    G = [_dot_nt(x, y) for x, y in zip(AR, BK)]
    r2 = lax.broadcasted_iota(jnp.int32, (C, 2 * C), 0)
    c2 = lax.broadcasted_iota(jnp.int32, (C, 2 * C), 1)
    incl2 = jnp.where(c2 >= C, c2 - C, c2) <= r2
    P = [jnp.where(strict, t[:C, :C], 0.0) for t in G]
    Lak = [jnp.where(strict, t[:C, C:], 0.0).astype(BF16) for t in G]
    L2 = [jnp.where(incl2, t[C:], 0.0).astype(BF16) for t in G]

    X = [jnp.concatenate([x, _dot(y, z)], axis=1) for x, y, z in zip(At, Lak, vb)]
    steps = C.bit_length() - 1
    for it in range(steps):
        Pb = [t.astype(BF16) for t in P]
        Xb = [x.astype(BF16) for x in X]
        if it + 1 < steps:
            PX = [_dot(p, jnp.concatenate([x, p], axis=1)) for p, x in zip(Pb, Xb)]
            X = [x + t[:, :2 * N] for x, t in zip(X, PX)]
            P = [t[:, 2 * N:] for t in PX]
        else:
            X = [x + _dot(p, xb) for x, p, xb in zip(X, Pb, Xb)]
    Xb = [x.astype(BF16) for x in X]

    zpad = jnp.zeros((C, N), BF16)
    RHS = [jnp.concatenate([x, jnp.concatenate([zpad, z], axis=1)], axis=0)
           for x, z in zip(Xb, vb)]
    LX = [_dot(x, y) for x, y in zip(L2, RHS)]
    Qp = [(x + y[:, :N]).astype(BF16) for x, y in zip(Rt, LX)]
    Y0 = [y[:, N:] for y in LX]

    XB = [_dot_tn(x, b) for x, b in zip(Xb, Bp)]
    Mw = [t[:N].astype(BF16) for t in XB]
    Nn = [t[N:] + _dot_tn(x, y) for t, x, y in zip(XB, vb, Kp)]

    S = [st_ref[hd] for hd in HD]
    y_rows = []
    for c in range(nchunk):
        ch = [c * RWKV_HEADS + hd for hd in HD]
        Ssp = [_split2(s) for s in S]
        ys = [_dot_nt(Qp[j], sp[0]) + Y0[j] for j, sp in zip(ch, Ssp)]
        y_rows.append(jnp.concatenate(ys, axis=1))
        S = [s * pcs[c][:, hs] + _dot(sp[0], Mw[j]) + _dot(sp[1], Mw[j]) + Nn[j]
             for s, sp, j, hs in zip(S, Ssp, ch, hsl)]
    for hd in HD:
        st_ref[hd] = S[hd]

    y = jnp.concatenate(y_rows, axis=0)
    inv_n = 1.0 / N
    yc = y - _head_sums(y, e_ref) * inv_n
    var = _head_sums(yc * yc, e_ref) * inv_n
    yn = yc * lax.rsqrt(var + RWKV_GN_EPS) * gw_ref[...] + gb_ref[...]
    bonus = _head_sums(r * k2 * rk_ref[...], e_ref) * v
    o_ref[...] = ((yn + bonus) * g_ref[...]).astype(o_ref.dtype)


def _rwkv_scan(rkv, lw, a, g, k_k, k_a, r_k, gn_w, gn_b, B, S):
    nchunk = RWKV_CHUNKS_PER_STEP
    T = nchunk * RWKV_CHUNK
    nc = S // T
    row = lambda b, c: (b * nc + c, 0)
    const = lambda b, c: (0, 0)
    vec = pl.BlockSpec((1, RWKV_W), const)
    hid = jnp.arange(2 * LANE, dtype=jnp.int32) // RWKV_HEAD_DIM
    head_ones = (hid[:, None] == hid[None, :]).astype(BF16)
    return pl.pallas_call(
        functools.partial(_rwkv_scan_kernel, nchunk=nchunk),
        name="rwkv_scan",
        out_shape=jax.ShapeDtypeStruct((B * S, RWKV_W), BF16),
        grid=(B, nc),
        in_specs=[pl.BlockSpec((T, 3 * RWKV_W), row),
                  pl.BlockSpec((T, RWKV_W), row),
                  pl.BlockSpec((T, RWKV_W), row),
                  pl.BlockSpec((T, RWKV_W), row),
                  vec, vec, vec, vec, vec,
                  pl.BlockSpec((2 * LANE, 2 * LANE), const)],
        out_specs=pl.BlockSpec((T, RWKV_W), row),
        scratch_shapes=[pltpu.VMEM((RWKV_HEADS, RWKV_HEAD_DIM, RWKV_HEAD_DIM), F32)],
        compiler_params=_cparams(("parallel", "arbitrary")),
    )(rkv, lw, a, g, k_k, k_a, r_k, gn_w, gn_b, head_ones)


def _ret_kernel(h_ref, cos_ref, sin_ref, dmat_ref, qd_ref, kd_ref, cd_ref, gw_ref, gb_ref,
                o_ref, R_ref, *, nchunk):
    C = RET_CHUNK
    d = RET_HEAD_DIM
    H = RET_HEADS
    W = RET_W

    @pl.when(pl.program_id(1) == 0)
    def _():
        R_ref[...] = jnp.zeros(R_ref.shape, F32)

    pairs = [(c, hd) for c in range(nchunk) for hd in range(H)]
    rows = lambda c: slice(c * C, (c + 1) * C)
    cols = lambda part, hd: slice(part * W + hd * d, part * W + (hd + 1) * d)
    cos = [cos_ref[rows(c), :] for c in range(nchunk)]
    sin = [sin_ref[rows(c), :] for c in range(nchunk)]
    q = [h_ref[rows(c), cols(0, hd)] for c, hd in pairs]
    k = [h_ref[rows(c), cols(1, hd)] for c, hd in pairs]
    qr = [x * cos[c] + pltpu.roll(x, d // 2, 1) * sin[c] for x, (c, hd) in zip(q, pairs)]
    kr = [(x * cos[c] + pltpu.roll(x, d // 2, 1) * sin[c]) * (d ** -0.5)
          for x, (c, hd) in zip(k, pairs)]
    vb = [h_ref[rows(c), cols(2, hd)].astype(BF16) for c, hd in pairs]
    qb = [x.astype(BF16) for x in qr]
    kb = [x.astype(BF16) for x in kr]
    kdb = [(x * kd_ref[hd]).astype(BF16) for x, (c, hd) in zip(kr, pairs)]
    inner = [(_dot_nt(x, y) * dmat_ref[hd]).astype(BF16) for x, y, (c, hd) in zip(qb, kb, pairs)]
    o_in = [_dot(x, y) for x, y in zip(inner, vb)]
    kv = [_dot_tn(x, y) for x, y in zip(kdb, vb)]

    R = [R_ref[hd] for hd in range(H)]
    outs = []
    for c in range(nchunk):
        ch = [c * H + hd for hd in range(H)]
        Rsp = [_split2(x) for x in R]
        outs += [o_in[j] + (_dot(qb[j], sp[0]) + _dot(qb[j], sp[1])) * qd_ref[hd]
                 for hd, (j, sp) in enumerate(zip(ch, Rsp))]
        R = [cd_ref[hd][0:1, :] * x + kv[j] for hd, (x, j) in enumerate(zip(R, ch))]
    for hd in range(H):
        R_ref[hd] = R[hd]

    for o, (c, hd) in zip(outs, pairs):
        mu = jnp.mean(o, axis=1, keepdims=True)
        oc = o - mu
        var = jnp.mean(oc * oc, axis=1, keepdims=True)
        gsl = slice(hd * d, (hd + 1) * d)
        on = oc * lax.rsqrt(var + RET_GN_EPS) * gw_ref[:, gsl] + gb_ref[:, gsl]
        gate = h_ref[rows(c), cols(3, hd)]
        o_ref[rows(c), gsl] = (gate * _sigmoid(gate) * on).astype(o_ref.dtype)


def _ret_tables(S):
    C, d, H = RET_CHUNK, RET_HEAD_DIM, RET_HEADS
    half = d // 2
    inv = 1.0 / (ROPE_BASE ** (jnp.arange(half, dtype=F32) / half))
    ang = jnp.arange(S, dtype=F32)[:, None] * inv[None, :]
    cos = jnp.cos(ang)
    sin = jnp.sin(ang)
    cos2 = jnp.concatenate([cos, cos], axis=1)
    sin2 = jnp.concatenate([-sin, sin], axis=1)
    log_g = jnp.log(1.0 - 2.0 ** (-5.0 - jnp.arange(H, dtype=F32)))
    pos = jnp.arange(C, dtype=F32)
    rel = pos[:, None] - pos[None, :]
    dmat = jnp.where(rel >= 0, jnp.exp(log_g[:, None, None] * jnp.maximum(rel, 0.0)), 0.0)
    qd = jnp.broadcast_to(jnp.exp(log_g[:, None] * (pos + 1.0))[..., None], (H, C, d))
    kd = jnp.broadcast_to(jnp.exp(log_g[:, None] * (C - 1.0 - pos))[..., None], (H, C, d))
    cd = jnp.broadcast_to(jnp.exp(log_g * C)[:, None, None], (H, 8, d))
    return cos2, sin2, dmat, qd, kd, cd


def _retention(h, tables, gn_w, gn_b, B, S):
    cos2, sin2, dmat, qd, kd, cd = tables
    C = RET_CHUNK
    T = min(512, S)
    nt = S // T
    H = RET_HEADS
    tab = pl.BlockSpec((T, LANE), lambda b, t: (t, 0))
    per_head = lambda rows: pl.BlockSpec((H, rows, LANE), lambda b, t: (0, 0, 0))
    vec = pl.BlockSpec((1, RET_W), lambda b, t: (0, 0))
    return pl.pallas_call(
        functools.partial(_ret_kernel, nchunk=T // C),
        name="retention",
        out_shape=jax.ShapeDtypeStruct((B * S, RET_W), BF16),
        grid=(B, nt),
        in_specs=[pl.BlockSpec((T, RET_COLS), lambda b, t: (b * nt + t, 0)),
                  tab, tab, per_head(C), per_head(C), per_head(C), per_head(8), vec, vec],
        out_specs=pl.BlockSpec((T, RET_W), lambda b, t: (b * nt + t, 0)),
        scratch_shapes=[pltpu.VMEM((H, RET_HEAD_DIM, RET_HEAD_DIM), F32)],
        compiler_params=_cparams(("parallel", "arbitrary")),
    )(h, cos2, sin2, dmat, qd, kd, cd, gn_w, gn_b)


def _outproj_kernel(x_ref, yf_ref, yr_ref, yt_ref, w_ref, g_ref, b_ref, o_ref):
    r0, r1 = FOX_W, FOX_W + RWKV_W
    y = (_dot(yf_ref[...], w_ref[:r0, :]) + _dot(yr_ref[...], w_ref[r0:r1, :])
         + _dot(yt_ref[...], w_ref[r1:, :]))
    o_ref[...] = _layer_norm_rows(ALPHA * x_ref[...] + y, g_ref[...], b_ref[...])


def _out_proj_ln(x2d, y_fox, y_rwkv, y_ret, w_out, l, g, b):
    M, D = x2d.shape
    tm = min(512, M)
    row = lambda i: (i, 0)
    const = lambda i: (0, 0)
    return pl.pallas_call(
        _outproj_kernel,
        name="out_proj_ln",
        out_shape=jax.ShapeDtypeStruct((M, D), F32),
        grid=(M // tm,),
        in_specs=[pl.BlockSpec((tm, D), row), pl.BlockSpec((tm, FOX_W), row),
                  pl.BlockSpec((tm, RWKV_W), row), pl.BlockSpec((tm, RET_W), row),
                  pl.BlockSpec((None, D, D), lambda i: (l, 0, 0)),
                  pl.BlockSpec((1, D), const), pl.BlockSpec((1, D), const)],
        out_specs=pl.BlockSpec((tm, D), row),
        compiler_params=_cparams(("parallel",)),
    )(x2d, y_fox, y_rwkv, y_ret, w_out, g, b)


def _ffn_kernel(x_ref, wu_ref, wd_ref, g_ref, b_ref, o_ref, ob_ref, acc_ref, xb_ref):
    f = pl.program_id(1)

    @pl.when(f == 0)
    def _():
        xb_ref[...] = x_ref[...].astype(BF16)
        acc_ref[...] = jnp.zeros(acc_ref.shape, F32)

    hid = jnp.maximum(_dot(xb_ref[...], wu_ref[...]), 0.0)
    acc_ref[...] += _dot((hid * hid).astype(BF16), wd_ref[...])

    @pl.when(f == pl.num_programs(1) - 1)
    def _():
        y = _layer_norm_rows(ALPHA * x_ref[...] + acc_ref[...], g_ref[...], b_ref[...])
        o_ref[...] = y
        ob_ref[...] = y.astype(BF16)


def _ffn_ln(x2d, w_up, w_down, l, g, b):
    M, D = x2d.shape
    F = w_up.shape[2]
    tm = min(512, M)
    tf = 1024
    return pl.pallas_call(
        _ffn_kernel,
        name="ffn_ln",
        out_shape=(jax.ShapeDtypeStruct((M, D), F32), jax.ShapeDtypeStruct((M, D), BF16)),
        grid=(M // tm, F // tf),
        in_specs=[pl.BlockSpec((tm, D), lambda i, f: (i, 0)),
                  pl.BlockSpec((None, D, tf), lambda i, f: (l, 0, f)),
                  pl.BlockSpec((None, tf, D), lambda i, f: (l, f, 0)),
                  pl.BlockSpec((1, D), lambda i, f: (0, 0)),
                  pl.BlockSpec((1, D), lambda i, f: (0, 0))],
        out_specs=(pl.BlockSpec((tm, D), lambda i, f: (i, 0)),
                   pl.BlockSpec((tm, D), lambda i, f: (i, 0))),
        scratch_shapes=[pltpu.VMEM((tm, D), F32), pltpu.VMEM((tm, D), BF16)],
        compiler_params=_cparams(("parallel", "arbitrary")),
    )(x2d, w_up, w_down, g, b)


def _permute_w_in(w_in):
    L, D, _ = w_in.shape
    w_fox = w_in[:, :, :FOX_COLS]
    w_rwkv = w_in[:, :, FOX_COLS:FOX_COLS + RWKV_COLS]
    w_ret = w_in[:, :, FOX_COLS + RWKV_COLS:]
    w_ff = w_fox[:, :, 3 * FOX_W:]
    pad = jnp.zeros((L, D, LANE - FOX_HEADS), w_in.dtype)
    w_q = w_fox[:, :, :3 * FOX_W].astype(BF16)
    w_rw = jnp.concatenate([w_rwkv, w_ff, pad], axis=-1).astype(BF16)
    return w_q, w_rw, w_ret.astype(BF16)


def _layer(x2d, xb, l, p, tables, B, S):
    hq = _in_proj(xb, p['w_q'], l, p['q_scale'])
    h = _in_proj(xb, p['w_rw'], l)
    ht = _in_proj(xb, p['w_ret'], l)
    c = _fox_c(h, p['fox_bias'], B, S)
    c3 = c[:, :FOX_HEADS, :].reshape(B * FOX_HEADS, 1, S)
    y_fox = _fox_attention(hq, c3, B, S)
    rkv, lw, a, g = _rwkv_pre(h, p['mu'], p['w0'], p['w_up_lora'], p['a0'], p['a_up'], p['g_up'], S)
    y_rwkv = _rwkv_scan(rkv, lw, a, g, p['k_k'], p['k_a'], p['r_k'], p['rgn_w'], p['rgn_b'], B, S)
    y_ret = _retention(ht, tables, p['tgn_w'], p['tgn_b'], B, S)
    x2d = _out_proj_ln(x2d, y_fox, y_rwkv, y_ret, p['w_out'], l, p['ln1_g'], p['ln1_b'])
    return _ffn_ln(x2d, p['w_up'], p['w_down'], l, p['ln2_g'], p['ln2_b'])


def kernel(x, w_in, fox_forget_bias, rwkv_mu, rwkv_w0, rwkv_w_up, rwkv_a0, rwkv_a_up, rwkv_g_up,
           rwkv_k_k, rwkv_k_a, rwkv_r_k, rwkv_gn_w, rwkv_gn_b, ret_gn_w, ret_gn_b, w_out, ln1_g,
           ln1_b, w_up, w_down, ln2_g, ln2_b):
    B, S, D = x.shape
    L = w_in.shape[0]
    w_q, w_rw, w_ret = _permute_w_in(w_in)
    q_scale = jnp.concatenate([jnp.full((1, FOX_W), FOX_Q_SCALE, F32),
                               jnp.ones((1, 2 * FOX_W), F32)], axis=1)
    w_out_b = w_out.astype(BF16)
    w_up_b = w_up.astype(BF16)
    w_down_b = w_down.astype(BF16)
    bias_pad = jnp.pad(fox_forget_bias, ((0, 0), (0, LANE - FOX_HEADS)))
    tables = _ret_tables(S)
    row = lambda t: t.reshape(1, -1)
    x2d = x.reshape(B * S, D)
    xb = x2d.astype(BF16)
    for l in range(L):
        p = {
            'w_rw': w_rw, 'w_ret': w_ret, 'w_q': w_q, 'q_scale': q_scale,
            'fox_bias': row(bias_pad[l]),
            'mu': row(rwkv_mu[l]), 'w0': row(rwkv_w0[l]), 'w_up_lora': rwkv_w_up[l].astype(BF16),
            'a0': row(rwkv_a0[l]), 'a_up': rwkv_a_up[l].astype(BF16),
            'g_up': rwkv_g_up[l].astype(BF16),
            'k_k': row(rwkv_k_k[l]), 'k_a': row(rwkv_k_a[l]), 'r_k': row(rwkv_r_k[l]),
            'rgn_w': row(rwkv_gn_w[l]), 'rgn_b': row(rwkv_gn_b[l]),
            'tgn_w': row(ret_gn_w[l]), 'tgn_b': row(ret_gn_b[l]),
            'w_out': w_out_b,
            'ln1_g': row(ln1_g[l]), 'ln1_b': row(ln1_b[l]),
            'w_up': w_up_b, 'w_down': w_down_b,
            'ln2_g': row(ln2_g[l]), 'ln2_b': row(ln2_b[l]),
        }
        x2d, xb = _layer(x2d, xb, l, p, tables, B, S)
    return x2d.reshape(B, S, D)
```

```python
import functools

import jax
import jax.numpy as jnp
from jax import lax
from jax.experimental import pallas as pl
from jax.experimental.pallas import tpu as pltpu

F32 = jnp.float32
BF16 = jnp.bfloat16

D_MODEL = 2048
DEPTH = 4
FOX_W = 768
RWKV_W = 640
RET_W = 640
FOX_HEAD_DIM = 128
FOX_HEADS = 6
RWKV_HEAD_DIM = 64
RWKV_HEADS = 10
RET_HEAD_DIM = 128
RET_HEADS = 5
DECAY_LORA = 64
ICLR_LORA = 64
GATE_LORA = 128
D_FF = 4 * D_MODEL
RET_CHUNK = 128
ROPE_BASE = 10000.0
LN_EPS = 1e-5
RWKV_GN_EPS = 64e-5
RET_GN_EPS = 1e-5
ALPHA = (2 * DEPTH) ** 0.25

FOX_COLS = 3 * FOX_W + FOX_HEADS
RWKV_COLS = 3 * RWKV_W + DECAY_LORA + ICLR_LORA + GATE_LORA
RET_COLS = 4 * RET_W

LANE = 128
H_FF0 = RWKV_COLS
LOG2E = 1.4426950408889634
FOX_Q_SCALE = FOX_HEAD_DIM ** -0.5 * LOG2E

FOX_BLOCKS_PER_TRIP = 4
RWKV_CHUNK = 64
RWKV_CHUNKS_PER_STEP = 4
NEG_BIG = -1e30

VMEM_LIMIT = 56 * 1024 * 1024


def _cparams(sem):
    return pltpu.CompilerParams(dimension_semantics=sem, vmem_limit_bytes=VMEM_LIMIT)


def _dot(a, b):
    return jnp.dot(a, b, preferred_element_type=F32)


def _dot_nt(a, b):
    return lax.dot_general(a, b, (((1,), (1,)), ((), ())), preferred_element_type=F32)


def _dot_tn(a, b):
    return lax.dot_general(a, b, (((0,), (0,)), ((), ())), preferred_element_type=F32)


def _split3(x):
    hi = x.astype(BF16)
    r1 = x - hi.astype(F32)
    mid = r1.astype(BF16)
    lo = (r1 - mid.astype(F32)).astype(BF16)
    return hi, mid, lo


def _split2(x):
    hi = x.astype(BF16)
    lo = (x - hi.astype(F32)).astype(BF16)
    return hi, lo


def _mask_dot(mask_bf16, x):
    hi, mid, lo = _split3(x)
    return _dot(mask_bf16, hi) + _dot(mask_bf16, mid) + _dot(mask_bf16, lo)


def _softplus(z):
    return jnp.maximum(z, 0.0) + jnp.log1p(jnp.exp(-jnp.abs(z)))


def _sigmoid(z):
    return 1.0 / (1.0 + jnp.exp(-z))


def _layer_norm_rows(y, g, b):
    mu = jnp.mean(y, axis=-1, keepdims=True)
    yc = y - mu
    var = jnp.mean(yc * yc, axis=-1, keepdims=True)
    return yc * lax.rsqrt(var + LN_EPS) * g + b


def _inproj_kernel(x_ref, w_ref, o_ref):
    o_ref[...] = _dot(x_ref[...], w_ref[...])


def _inproj_scaled_kernel(x_ref, w_ref, s_ref, o_ref):
    o_ref[...] = (_dot(x_ref[...], w_ref[...]) * s_ref[...]).astype(o_ref.dtype)


def _in_proj(xb, w, l, col_scale=None):
    M, K = xb.shape
    N = w.shape[2]
    tm = min(512, M)
    tn = N
    in_specs = [pl.BlockSpec((tm, K), lambda j, i: (i, 0)),
                pl.BlockSpec((None, K, tn), lambda j, i: (l, 0, j))]
    args = [xb, w]
    if col_scale is not None:
        in_specs.append(pl.BlockSpec((1, tn), lambda j, i: (0, j)))
        args.append(col_scale)
    return pl.pallas_call(
        _inproj_kernel if col_scale is None else _inproj_scaled_kernel,
        name="in_proj",
        out_shape=jax.ShapeDtypeStruct((M, N), F32 if col_scale is None else BF16),
        grid=(N // tn, M // tm),
        in_specs=in_specs,
        out_specs=pl.BlockSpec((tm, tn), lambda j, i: (i, j)),
        compiler_params=_cparams(("parallel", "parallel")),
    )(*args)


def _fox_c_kernel(f_ref, bias_ref, c_ref, *, rows):
    S = f_ref.shape[0]
    r = lax.broadcasted_iota(jnp.int32, (rows, rows), 0)
    c = lax.broadcasted_iota(jnp.int32, (rows, rows), 1)
    tri = (c <= r).astype(BF16)

    def body(i, carry):
        start = pl.multiple_of(i * rows, rows)
        z = f_ref[pl.ds(start, rows), :] + bias_ref[...]
        ls = -_softplus(-z)
        cs = _mask_dot(tri, ls) + carry
        c_ref[:, pl.ds(start, rows)] = (cs * LOG2E).T[0:8, :]
        return cs[rows - 1:rows, :]

    lax.fori_loop(0, S // rows, body, jnp.zeros((1, LANE), F32))


def _fox_c(h, bias_pad, B, S):
    rows = min(256, S)
    return pl.pallas_call(
        functools.partial(_fox_c_kernel, rows=rows),
        name="fox_c",
        out_shape=jax.ShapeDtypeStruct((B, 8, S), F32),
        grid=(B,),
        in_specs=[pl.BlockSpec((S, LANE), lambda b: (b, H_FF0 // LANE)),
                  pl.BlockSpec((1, LANE), lambda b: (0, 0))],
        out_specs=pl.BlockSpec((None, 8, S), lambda b: (b, 0, 0)),
        compiler_params=_cparams(("parallel",)),
    )(h, bias_pad)


def _fox_attn_kernel(q_ref, k_ref, v_ref, c_ref, o_ref, s0_ref, *, tq, tk):
    i = pl.program_id(2)
    d = q_ref.shape[1]
    q0 = pl.multiple_of(i * tq, tq)
    c0 = c_ref[:, pl.ds(q0, tq)][:, 0:1]
    q = q_ref[...]

    def scores(start):
        bias = c0 - c_ref[:, pl.ds(start, tk)]
        return _dot_nt(q, k_ref[pl.ds(start, tk), :]) + bias

    ones = jnp.ones((tk, d), BF16)

    def update(start, sc, carry, masked):
        m, acc = carry
        if masked:
            row = lax.broadcasted_iota(jnp.int32, (tq, tk), 0)
            col = lax.broadcasted_iota(jnp.int32, (tq, tk), 1)
            sc = jnp.where(col <= row, sc, NEG_BIG)
        m_new = jnp.maximum(m, jnp.max(sc, axis=1, keepdims=True))
        alpha = jnp.exp2(m - m_new)
        p = jnp.exp2((sc - m_new).astype(BF16))
        vb = jnp.concatenate([v_ref[pl.ds(start, tk), :], ones], axis=1)
        acc = alpha * acc + _dot(p, vb)
        return m_new, acc

    U = FOX_BLOCKS_PER_TRIP

    def walk(base, n, carry, last_masked):
        s_cur = s0_ref[...]
        for u in range(n):
            start = pl.multiple_of(base + u * tk, tk)
            final = last_masked and u == n - 1
            s_next = None if final else scores(pl.multiple_of(start + tk, tk))
            carry = update(start, s_cur, carry, final)
            s_cur = s_next
        if not last_masked:
            s0_ref[...] = s_cur
        return carry

    s0_ref[...] = scores(0)
    init = (jnp.full((tq, 1), NEG_BIG, F32), jnp.zeros((tq, 2 * d), F32))
    trips = i // U
    carry = lax.fori_loop(
        0, trips, lambda t, c: walk(pl.multiple_of(t * (U * tk), tk), U, c, False), init)

    base = pl.multiple_of(trips * (U * tk), tk)
    tails = [functools.partial(walk, base, r + 1, last_masked=True) for r in range(U)]
    m, acc = lax.switch(i - trips * U, tails, carry)
    o_ref[...] = (acc[:, :d] / acc[:, d:]).astype(o_ref.dtype)


def _fox_attention(hq, c3, B, S):
    tq = tk = min(512, S)
    nq = S // tq
    nh = FOX_HEADS
    return pl.pallas_call(
        functools.partial(_fox_attn_kernel, tq=tq, tk=tk),
        name="fox_attn",
        scratch_shapes=[pltpu.VMEM((tq, tk), F32)],
        out_shape=jax.ShapeDtypeStruct((B * S, FOX_W), BF16),
        grid=(B, FOX_HEADS, nq),
        in_specs=[pl.BlockSpec((tq, LANE), lambda b, hd, i: (b * nq + i, hd)),
                  pl.BlockSpec((S, LANE), lambda b, hd, i: (b, nh + hd)),
                  pl.BlockSpec((S, LANE), lambda b, hd, i: (b, 2 * nh + hd)),
                  pl.BlockSpec((None, 1, S), lambda b, hd, i: (b * nh + hd, 0, 0))],
        out_specs=pl.BlockSpec((tq, LANE), lambda b, hd, i: (b * nq + i, hd)),
        compiler_params=_cparams(("parallel", "parallel", "arbitrary")),
    )(hq, hq, hq, c3)


def _rwkv_premix(h, prev, mu, w0, wup, a0, aup, gup):
    rows = lax.broadcasted_iota(jnp.int32, h.shape, 0)
    hp = jnp.where(rows == 0, prev, pltpu.roll(h, 1, 0))
    hs = h + (hp - h) * mu
    W3 = 3 * RWKV_W
    r = hs[:, :RWKV_W]
    k = hs[:, RWKV_W:2 * RWKV_W]
    v = hs[:, 2 * RWKV_W:W3]
    wd = hs[:, W3:W3 + DECAY_LORA]
    ad = hs[:, W3 + DECAY_LORA:W3 + DECAY_LORA + ICLR_LORA]
    gd = hs[:, W3 + DECAY_LORA + ICLR_LORA:]
    z = w0 + _dot(jnp.tanh(wd).astype(BF16), wup)
    w = -_softplus(-z) - 0.5
    lw = -jnp.exp(w)
    a = _sigmoid(a0 + _dot(ad.astype(BF16), aup))
    g = _dot(_sigmoid(gd).astype(BF16), gup)
    return r, k, v, lw, a, g


def _head_sums(x, e_ref):
    hi, lo = _split2(x)
    e = e_ref[...]
    e_last = e[:LANE, :LANE]
    outs = []
    for c0 in range(0, RWKV_W, 2 * LANE):
        w = min(2 * LANE, RWKV_W - c0)
        ee = e if w == 2 * LANE else e_last
        outs.append(_dot(hi[:, c0:c0 + w], ee) + _dot(lo[:, c0:c0 + w], ee))
    return jnp.concatenate(outs, axis=1)


def _rwkv_scan_kernel(h_ref, hprev_ref, mu_ref, w0_ref, wup_ref, a0_ref, aup_ref, gup_ref,
                      kk_ref, ka_ref, rk_ref, gw_ref, gb_ref, e_ref, o_ref, st_ref, *, nchunk):
    C = RWKV_CHUNK
    N = RWKV_HEAD_DIM
    T = nchunk * C
    first = pl.program_id(1) == 0

    @pl.when(first)
    def _():
        st_ref[...] = jnp.zeros(st_ref.shape, F32)

    prev = jnp.where(first, 0.0, hprev_ref[7:8, :])
    r, k, v, lw, a, g = _rwkv_premix(h_ref[...], prev, mu_ref[...], w0_ref[...], wup_ref[...],
                                     a0_ref[...], aup_ref[...], gup_ref[...])

    ri = lax.broadcasted_iota(jnp.int32, (C, C), 0)
    ci = lax.broadcasted_iota(jnp.int32, (C, C), 1)
    incl = ci <= ri
    strict = ci < ri

    rt = lax.broadcasted_iota(jnp.int32, (T, T), 0)
    ct = lax.broadcasted_iota(jnp.int32, (T, T), 1)
    tri = ((ct <= rt) & (ct >= (rt // C) * C)).astype(BF16)
    lw_hi, lw_lo = _split2(lw)
    cum = _dot(tri, lw_hi) + _dot(tri, lw_lo)
    cls = [cum[(c + 1) * C - 1:(c + 1) * C, :] for c in range(nchunk)]
    cl_rows = jnp.concatenate([jnp.broadcast_to(t, (C, RWKV_W)) for t in cls], axis=0)
    e_in = jnp.exp(cum)
    e_ex = jnp.exp(cum - lw)
    e_neg = jnp.exp(-cum)
    e_tail = jnp.exp(cl_rows - cum)
    pcs = [jnp.exp(t) for t in cls]

    kk = k * kk_ref[...]
    kkn = kk / jnp.maximum(jnp.sqrt(_head_sums(kk * kk, e_ref)), 1e-12)
    k2 = k * (1.0 + (a - 1.0) * ka_ref[...])
    bvec = kkn * a
    At_f = -kkn * e_ex
    Rt_f = r * e_in
    AtB = At_f.astype(BF16)
    RtB = Rt_f.astype(BF16)
    BtB = (bvec * e_neg).astype(BF16)
    KtB = (k2 * e_neg).astype(BF16)
    BpB = (bvec * e_tail).astype(BF16)
    KpB = (k2 * e_tail).astype(BF16)
    vB = v.astype(BF16)

    HD = range(RWKV_HEADS)
    hsl = [slice(hd * N, (hd + 1) * N) for hd in HD]
    sls = [(slice(c * C, (c + 1) * C), hs) for c in range(nchunk) for hs in hsl]
    At = [At_f[rs, hs] for rs, hs in sls]
    Rt = [Rt_f[rs, hs] for rs, hs in sls]
    vb = [vB[rs, hs] for rs, hs in sls]
    Bt = [BtB[rs, hs] for rs, hs in sls]
    Kt = [KtB[rs, hs] for rs, hs in sls]
    Bp = [BpB[rs, hs] for rs, hs in sls]
    Kp = [KpB[rs, hs] for rs, hs in sls]

    AR = [jnp.concatenate([AtB[rs, hs], RtB[rs, hs]], axis=0) for rs, hs in sls]
    BK = [jnp.concatenate([x, y], axis=0) for x, y in zip(Bt, Kt)]
    G = [_dot_nt(x, y) for x, y in zip(AR, BK)]
    r2 = lax.broadcasted_iota(jnp.int32, (C, 2 * C), 0)
    c2 = lax.broadcasted_iota(jnp.int32, (C, 2 * C), 1)
    incl2 = jnp.where(c2 >= C, c2 - C, c2) <= r2
    P = [jnp.where(strict, t[:C, :C], 0.0) for t in G]
    Lak = [jnp.where(strict, t[:C, C:], 0.0).astype(BF16) for t in G]
    L2 = [jnp.where(incl2, t[C:], 0.0).astype(BF16) for t in G]

    X = [jnp.concatenate([x, _dot(y, z)], axis=1) for x, y, z in zip(At, Lak, vb)]
    steps = C.bit_length() - 1
    for it in range(steps):
        Pb = [t.astype(BF16) for t in P]
        Xb = [x.astype(BF16) for x in X]
        if it + 1 < steps:
            PX = [_dot(p, jnp.concatenate([x, p], axis=1)) for p, x in zip(Pb, Xb)]
            X = [x + t[:, :2 * N] for x, t in zip(X, PX)]
            P = [t[:, 2 * N:] for t in PX]
        else:
            X = [x + _dot(p, xb) for x, p, xb in zip(X, Pb, Xb)]
    Xb = [x.astype(BF16) for x in X]

    zpad = jnp.zeros((C, N), BF16)
    RHS = [jnp.concatenate([x, jnp.concatenate([zpad, z], axis=1)], axis=0)
           for x, z in zip(Xb, vb)]
    LX = [_dot(x, y) for x, y in zip(L2, RHS)]
    Qp = [(x + y[:, :N]).astype(BF16) for x, y in zip(Rt, LX)]
    Y0 = [y[:, N:] for y in LX]

    XB = [_dot_tn(x, b) for x, b in zip(Xb, Bp)]
    Mw = [t[:N].astype(BF16) for t in XB]
    Nn = [t[N:] + _dot_tn(x, y) for t, x, y in zip(XB, vb, Kp)]

    S = [st_ref[hd] for hd in HD]
    y_rows = []
    for c in range(nchunk):
        ch = [c * RWKV_HEADS + hd for hd in HD]
        Ssp = [_split2(s) for s in S]
        ys = [_dot_nt(Qp[j], sp[0]) + Y0[j] for j, sp in zip(ch, Ssp)]
        y_rows.append(jnp.concatenate(ys, axis=1))
        SM = [_dot(jnp.concatenate(sp, axis=0), Mw[j]) for sp, j in zip(Ssp, ch)]
        S = [s * pcs[c][:, hs] + sm[:N] + sm[N:] + Nn[j]
             for s, sm, j, hs in zip(S, SM, ch, hsl)]
    for hd in HD:
        st_ref[hd] = S[hd]

    y = jnp.concatenate(y_rows, axis=0)
    inv_n = 1.0 / N
    yc = y - _head_sums(y, e_ref) * inv_n
    var = _head_sums(yc * yc, e_ref) * inv_n
    yn = yc * lax.rsqrt(var + RWKV_GN_EPS) * gw_ref[...] + gb_ref[...]
    bonus = _head_sums(r * k2 * rk_ref[...], e_ref) * v
    o_ref[...] = ((yn + bonus) * g).astype(o_ref.dtype)


def _rwkv_mix(h, mu, w0, wup, a0, aup, gup, k_k, k_a, r_k, gn_w, gn_b, B, S):
    nchunk = RWKV_CHUNKS_PER_STEP
    T = nchunk * RWKV_CHUNK
    nc = S // T
    row = lambda b, c: (b * nc + c, 0)
    prev_row = lambda b, c: (jnp.maximum((b * nc + c) * (T // 8) - 1, 0), 0)
    const = lambda b, c: (0, 0)
    vec = pl.BlockSpec((1, RWKV_W), const)
    hid = jnp.arange(2 * LANE, dtype=jnp.int32) // RWKV_HEAD_DIM
    head_ones = (hid[:, None] == hid[None, :]).astype(BF16)
    return pl.pallas_call(
        functools.partial(_rwkv_scan_kernel, nchunk=nchunk),
        name="rwkv_scan",
        out_shape=jax.ShapeDtypeStruct((B * S, RWKV_W), BF16),
        grid=(B, nc),
        in_specs=[pl.BlockSpec((T, RWKV_COLS), row),
                  pl.BlockSpec((8, RWKV_COLS), prev_row),
                  pl.BlockSpec((1, RWKV_COLS), const),
                  vec,
                  pl.BlockSpec((DECAY_LORA, RWKV_W), const),
                  vec,
                  pl.BlockSpec((ICLR_LORA, RWKV_W), const),
                  pl.BlockSpec((GATE_LORA, RWKV_W), const),
                  vec, vec, vec, vec, vec,
                  pl.BlockSpec((2 * LANE, 2 * LANE), const)],
        out_specs=pl.BlockSpec((T, RWKV_W), row),
        scratch_shapes=[pltpu.VMEM((RWKV_HEADS, RWKV_HEAD_DIM, RWKV_HEAD_DIM), F32)],
        compiler_params=_cparams(("parallel", "arbitrary")),
    )(h, h, mu, w0, wup, a0, aup, gup, k_k, k_a, r_k, gn_w, gn_b, head_ones)


def _ret_kernel(h_ref, cos_ref, sin_ref, dmat_ref, qd_ref, kd_ref, cd_ref, gw_ref, gb_ref,
                o_ref, R_ref, *, nchunk):
    C = RET_CHUNK
    d = RET_HEAD_DIM
    H = RET_HEADS
    W = RET_W

    @pl.when(pl.program_id(1) == 0)
    def _():
        R_ref[...] = jnp.zeros(R_ref.shape, F32)

    pairs = [(c, hd) for c in range(nchunk) for hd in range(H)]
    rows = lambda c: slice(c * C, (c + 1) * C)
    cols = lambda part, hd: slice(part * W + hd * d, part * W + (hd + 1) * d)
    cos = [cos_ref[rows(c), :] for c in range(nchunk)]
    sin = [sin_ref[rows(c), :] for c in range(nchunk)]
    q = [h_ref[rows(c), cols(0, hd)] for c, hd in pairs]
    k = [h_ref[rows(c), cols(1, hd)] for c, hd in pairs]
    qr = [x * cos[c] + pltpu.roll(x, d // 2, 1) * sin[c] for x, (c, hd) in zip(q, pairs)]
    kr = [(x * cos[c] + pltpu.roll(x, d // 2, 1) * sin[c]) * (d ** -0.5)
          for x, (c, hd) in zip(k, pairs)]
    vb = [h_ref[rows(c), cols(2, hd)].astype(BF16) for c, hd in pairs]
    qb = [x.astype(BF16) for x in qr]
    kb = [x.astype(BF16) for x in kr]
    kdb = [(x * kd_ref[hd]).astype(BF16) for x, (c, hd) in zip(kr, pairs)]
    inner = [(_dot_nt(x, y) * dmat_ref[hd]).astype(BF16) for x, y, (c, hd) in zip(qb, kb, pairs)]
    o_in = [_dot(x, y) for x, y in zip(inner, vb)]
    kv = [_dot_tn(x, y) for x, y in zip(kdb, vb)]

    R = [R_ref[hd] for hd in range(H)]
    outs = []
    for c in range(nchunk):
        ch = [c * H + hd for hd in range(H)]
        Rsp = [_split2(x) for x in R]
        outs += [o_in[j] + (_dot(qb[j], sp[0]) + _dot(qb[j], sp[1])) * qd_ref[hd]
                 for hd, (j, sp) in enumerate(zip(ch, Rsp))]
        R = [cd_ref[hd][0:1, :] * x + kv[j] for hd, (x, j) in enumerate(zip(R, ch))]
    for hd in range(H):
        R_ref[hd] = R[hd]

    for o, (c, hd) in zip(outs, pairs):
        mu = jnp.mean(o, axis=1, keepdims=True)
        oc = o - mu
        var = jnp.mean(oc * oc, axis=1, keepdims=True)
        gsl = slice(hd * d, (hd + 1) * d)
        on = oc * lax.rsqrt(var + RET_GN_EPS) * gw_ref[:, gsl] + gb_ref[:, gsl]
        gate = h_ref[rows(c), cols(3, hd)]
        o_ref[rows(c), gsl] = (gate * _sigmoid(gate) * on).astype(o_ref.dtype)


def _ret_tables(S):
    C, d, H = RET_CHUNK, RET_HEAD_DIM, RET_HEADS
    half = d // 2
    inv = 1.0 / (ROPE_BASE ** (jnp.arange(half, dtype=F32) / half))
    ang = jnp.arange(S, dtype=F32)[:, None] * inv[None, :]
    cos = jnp.cos(ang)
    sin = jnp.sin(ang)
    cos2 = jnp.concatenate([cos, cos], axis=1)
    sin2 = jnp.concatenate([-sin, sin], axis=1)
    log_g = jnp.log(1.0 - 2.0 ** (-5.0 - jnp.arange(H, dtype=F32)))
    pos = jnp.arange(C, dtype=F32)
    rel = pos[:, None] - pos[None, :]
    dmat = jnp.where(rel >= 0, jnp.exp(log_g[:, None, None] * jnp.maximum(rel, 0.0)), 0.0)
    qd = jnp.broadcast_to(jnp.exp(log_g[:, None] * (pos + 1.0))[..., None], (H, C, d))
    kd = jnp.broadcast_to(jnp.exp(log_g[:, None] * (C - 1.0 - pos))[..., None], (H, C, d))
    cd = jnp.broadcast_to(jnp.exp(log_g * C)[:, None, None], (H, 8, d))
    return cos2, sin2, dmat, qd, kd, cd


def _retention(h, tables, gn_w, gn_b, B, S):
    cos2, sin2, dmat, qd, kd, cd = tables
    C = RET_CHUNK
    T = min(512, S)
    nt = S // T
    H = RET_HEADS
    tab = pl.BlockSpec((T, LANE), lambda b, t: (t, 0))
    per_head = lambda rows: pl.BlockSpec((H, rows, LANE), lambda b, t: (0, 0, 0))
    vec = pl.BlockSpec((1, RET_W), lambda b, t: (0, 0))
    return pl.pallas_call(
        functools.partial(_ret_kernel, nchunk=T // C),
        name="retention",
        out_shape=jax.ShapeDtypeStruct((B * S, RET_W), BF16),
        grid=(B, nt),
        in_specs=[pl.BlockSpec((T, RET_COLS), lambda b, t: (b * nt + t, 0)),
                  tab, tab, per_head(C), per_head(C), per_head(C), per_head(8), vec, vec],
        out_specs=pl.BlockSpec((T, RET_W), lambda b, t: (b * nt + t, 0)),
        scratch_shapes=[pltpu.VMEM((H, RET_HEAD_DIM, RET_HEAD_DIM), F32)],
        compiler_params=_cparams(("parallel", "arbitrary")),
    )(h, cos2, sin2, dmat, qd, kd, cd, gn_w, gn_b)


def _outproj_kernel(x_ref, yf_ref, yr_ref, yt_ref, w_ref, g_ref, b_ref, o_ref):
    r0, r1 = FOX_W, FOX_W + RWKV_W
    y = (_dot(yf_ref[...], w_ref[:r0, :]) + _dot(yr_ref[...], w_ref[r0:r1, :])
         + _dot(yt_ref[...], w_ref[r1:, :]))
    o_ref[...] = _layer_norm_rows(ALPHA * x_ref[...] + y, g_ref[...], b_ref[...])


def _out_proj_ln(x2d, y_fox, y_rwkv, y_ret, w_out, l, g, b):
    M, D = x2d.shape
    tm = min(512, M)
    row = lambda i: (i, 0)
    const = lambda i: (0, 0)
    return pl.pallas_call(
        _outproj_kernel,
        name="out_proj_ln",
        out_shape=jax.ShapeDtypeStruct((M, D), F32),
        grid=(M // tm,),
        in_specs=[pl.BlockSpec((tm, D), row), pl.BlockSpec((tm, FOX_W), row),
                  pl.BlockSpec((tm, RWKV_W), row), pl.BlockSpec((tm, RET_W), row),
                  pl.BlockSpec((None, D, D), lambda i: (l, 0, 0)),
                  pl.BlockSpec((1, D), const), pl.BlockSpec((1, D), const)],
        out_specs=pl.BlockSpec((tm, D), row),
        compiler_params=_cparams(("parallel",)),
    )(x2d, y_fox, y_rwkv, y_ret, w_out, g, b)


def _ffn_kernel(x_ref, wu_ref, wd_ref, g_ref, b_ref, o_ref, ob_ref, acc_ref, xb_ref):
    f = pl.program_id(1)

    @pl.when(f == 0)
    def _():
        xb_ref[...] = x_ref[...].astype(BF16)
        acc_ref[...] = jnp.zeros(acc_ref.shape, F32)

    hid = jnp.maximum(_dot(xb_ref[...], wu_ref[...]), 0.0)
    acc_ref[...] += _dot((hid * hid).astype(BF16), wd_ref[...])

    @pl.when(f == pl.num_programs(1) - 1)
    def _():
        y = _layer_norm_rows(ALPHA * x_ref[...] + acc_ref[...], g_ref[...], b_ref[...])
        o_ref[...] = y
        ob_ref[...] = y.astype(BF16)


def _ffn_ln(x2d, w_up, w_down, l, g, b):
    M, D = x2d.shape
    F = w_up.shape[2]
    tm = min(512, M)
    tf = 1024
    return pl.pallas_call(
        _ffn_kernel,
        name="ffn_ln",
        out_shape=(jax.ShapeDtypeStruct((M, D), F32), jax.ShapeDtypeStruct((M, D), BF16)),
        grid=(M // tm, F // tf),
        in_specs=[pl.BlockSpec((tm, D), lambda i, f: (i, 0)),
                  pl.BlockSpec((None, D, tf), lambda i, f: (l, 0, f)),
                  pl.BlockSpec((None, tf, D), lambda i, f: (l, f, 0)),
                  pl.BlockSpec((1, D), lambda i, f: (0, 0)),
                  pl.BlockSpec((1, D), lambda i, f: (0, 0))],
        out_specs=(pl.BlockSpec((tm, D), lambda i, f: (i, 0)),
                   pl.BlockSpec((tm, D), lambda i, f: (i, 0))),
        scratch_shapes=[pltpu.VMEM((tm, D), F32), pltpu.VMEM((tm, D), BF16)],
        compiler_params=_cparams(("parallel", "arbitrary")),
    )(x2d, w_up, w_down, g, b)


def _permute_w_in(w_in):
    L, D, _ = w_in.shape
    w_fox = w_in[:, :, :FOX_COLS]
    w_rwkv = w_in[:, :, FOX_COLS:FOX_COLS + RWKV_COLS]
    w_ret = w_in[:, :, FOX_COLS + RWKV_COLS:]
    w_ff = w_fox[:, :, 3 * FOX_W:]
    pad = jnp.zeros((L, D, LANE - FOX_HEADS), w_in.dtype)
    w_q = w_fox[:, :, :3 * FOX_W].astype(BF16)
    w_rw = jnp.concatenate([w_rwkv, w_ff, pad], axis=-1).astype(BF16)
    return w_q, w_rw, w_ret.astype(BF16)


def _layer(x2d, xb, l, p, tables, B, S):
    hq = _in_proj(xb, p['w_q'], l, p['q_scale'])
    h = _in_proj(xb, p['w_rw'], l)
    ht = _in_proj(xb, p['w_ret'], l)
    c = _fox_c(h, p['fox_bias'], B, S)
    c3 = c[:, :FOX_HEADS, :].reshape(B * FOX_HEADS, 1, S)
    y_fox = _fox_attention(hq, c3, B, S)
    y_rwkv = _rwkv_mix(h, p['mu'], p['w0'], p['w_up_lora'], p['a0'], p['a_up'], p['g_up'],
                       p['k_k'], p['k_a'], p['r_k'], p['rgn_w'], p['rgn_b'], B, S)
    y_ret = _retention(ht, tables, p['tgn_w'], p['tgn_b'], B, S)
    x2d = _out_proj_ln(x2d, y_fox, y_rwkv, y_ret, p['w_out'], l, p['ln1_g'], p['ln1_b'])
    return _ffn_ln(x2d, p['w_up'], p['w_down'], l, p['ln2_g'], p['ln2_b'])


def kernel(x, w_in, fox_forget_bias, rwkv_mu, rwkv_w0, rwkv_w_up, rwkv_a0, rwkv_a_up, rwkv_g_up,
           rwkv_k_k, rwkv_k_a, rwkv_r_k, rwkv_gn_w, rwkv_gn_b, ret_gn_w, ret_gn_b, w_out, ln1_g,
           ln1_b, w_up, w_down, ln2_g, ln2_b):
    B, S, D = x.shape
    L = w_in.shape[0]
    w_q, w_rw, w_ret = _permute_w_in(w_in)
    q_scale = jnp.concatenate([jnp.full((1, FOX_W), FOX_Q_SCALE, F32),
                               jnp.ones((1, 2 * FOX_W), F32)], axis=1)
    w_out_b = w_out.astype(BF16)
    w_up_b = w_up.astype(BF16)
    w_down_b = w_down.astype(BF16)
    bias_pad = jnp.pad(fox_forget_bias, ((0, 0), (0, LANE - FOX_HEADS)))
    tables = _ret_tables(S)
    row = lambda t: t.reshape(1, -1)
    x2d = x.reshape(B * S, D)
    xb = x2d.astype(BF16)
    for l in range(L):
        p = {
            'w_rw': w_rw, 'w_ret': w_ret, 'w_q': w_q, 'q_scale': q_scale,
            'fox_bias': row(bias_pad[l]),
            'mu': row(rwkv_mu[l]), 'w0': row(rwkv_w0[l]), 'w_up_lora': rwkv_w_up[l].astype(BF16),
            'a0': row(rwkv_a0[l]), 'a_up': rwkv_a_up[l].astype(BF16),
            'g_up': rwkv_g_up[l].astype(BF16),
            'k_k': row(rwkv_k_k[l]), 'k_a': row(rwkv_k_a[l]), 'r_k': row(rwkv_r_k[l]),
            'rgn_w': row(rwkv_gn_w[l]), 'rgn_b': row(rwkv_gn_b[l]),
            'tgn_w': row(ret_gn_w[l]), 'tgn_b': row(ret_gn_b[l]),
            'w_out': w_out_b,
            'ln1_g': row(ln1_g[l]), 'ln1_b': row(ln1_b[l]),
            'w_up': w_up_b, 'w_down': w_down_b,
            'ln2_g': row(ln2_g[l]), 'ln2_b': row(ln2_b[l]),
        }
        x2d, xb = _layer(x2d, xb, l, p, tables, B, S)
    return x2d.reshape(B, S, D)
```

```python
import functools

import jax
import jax.numpy as jnp
from jax import lax
from jax.experimental import pallas as pl
from jax.experimental.pallas import tpu as pltpu

F32 = jnp.float32
BF16 = jnp.bfloat16

D_MODEL = 2048
DEPTH = 4
FOX_W = 768
RWKV_W = 640
RET_W = 640
FOX_HEAD_DIM = 128
FOX_HEADS = 6
RWKV_HEAD_DIM = 64
RWKV_HEADS = 10
RET_HEAD_DIM = 128
RET_HEADS = 5
DECAY_LORA = 64
ICLR_LORA = 64
GATE_LORA = 128
D_FF = 4 * D_MODEL
RET_CHUNK = 128
ROPE_BASE = 10000.0
LN_EPS = 1e-5
RWKV_GN_EPS = 64e-5
RET_GN_EPS = 1e-5
ALPHA = (2 * DEPTH) ** 0.25

FOX_COLS = 3 * FOX_W + FOX_HEADS
RWKV_COLS = 3 * RWKV_W + DECAY_LORA + ICLR_LORA + GATE_LORA
RET_COLS = 4 * RET_W

LANE = 128
H_FF0 = RWKV_COLS
LOG2E = 1.4426950408889634
FOX_Q_SCALE = FOX_HEAD_DIM ** -0.5 * LOG2E

FOX_BLOCKS_PER_TRIP = 4
FOX_Q_ROWS = 1024
RWKV_CHUNK = 64
RWKV_CHUNKS_PER_STEP = 4
NEG_BIG = -1e30

VMEM_LIMIT = 56 * 1024 * 1024


def _cparams(sem):
    return pltpu.CompilerParams(dimension_semantics=sem, vmem_limit_bytes=VMEM_LIMIT)


def _dot(a, b):
    return jnp.dot(a, b, preferred_element_type=F32)


def _dot_nt(a, b):
    return lax.dot_general(a, b, (((1,), (1,)), ((), ())), preferred_element_type=F32)


def _dot_tn(a, b):
    return lax.dot_general(a, b, (((0,), (0,)), ((), ())), preferred_element_type=F32)


def _split3(x):
    hi = x.astype(BF16)
    r1 = x - hi.astype(F32)
    mid = r1.astype(BF16)
    lo = (r1 - mid.astype(F32)).astype(BF16)
    return hi, mid, lo


def _split2(x):
    hi = x.astype(BF16)
    lo = (x - hi.astype(F32)).astype(BF16)
    return hi, lo


def _mask_dot(mask_bf16, x):
    hi, mid, lo = _split3(x)
    return _dot(mask_bf16, hi) + _dot(mask_bf16, mid) + _dot(mask_bf16, lo)


def _softplus(z):
    return jnp.maximum(z, 0.0) + jnp.log1p(jnp.exp(-jnp.abs(z)))


def _sigmoid(z):
    return 1.0 / (1.0 + jnp.exp(-z))


def _layer_norm_rows(y, g, b):
    mu = jnp.mean(y, axis=-1, keepdims=True)
    yc = y - mu
    var = jnp.mean(yc * yc, axis=-1, keepdims=True)
    return yc * lax.rsqrt(var + LN_EPS) * g + b


def _inproj_kernel(x_ref, w_ref, o_ref):
    o_ref[...] = _dot(x_ref[...], w_ref[...])


def _inproj_scaled_kernel(x_ref, w_ref, s_ref, o_ref):
    o_ref[...] = (_dot(x_ref[...], w_ref[...]) * s_ref[...]).astype(o_ref.dtype)


def _in_proj(xb, w, l, col_scale=None):
    M, K = xb.shape
    N = w.shape[2]
    tm = min(512, M)
    tn = N
    in_specs = [pl.BlockSpec((tm, K), lambda j, i: (i, 0)),
                pl.BlockSpec((None, K, tn), lambda j, i: (l, 0, j))]
    args = [xb, w]
    if col_scale is not None:
        in_specs.append(pl.BlockSpec((1, tn), lambda j, i: (0, j)))
        args.append(col_scale)
    return pl.pallas_call(
        _inproj_kernel if col_scale is None else _inproj_scaled_kernel,
        name="in_proj",
        out_shape=jax.ShapeDtypeStruct((M, N), F32 if col_scale is None else BF16),
        grid=(N // tn, M // tm),
        in_specs=in_specs,
        out_specs=pl.BlockSpec((tm, tn), lambda j, i: (i, j)),
        compiler_params=_cparams(("parallel", "parallel")),
    )(*args)


def _fox_c_kernel(f_ref, bias_ref, c_ref, *, rows):
    S = f_ref.shape[0]
    r = lax.broadcasted_iota(jnp.int32, (rows, rows), 0)
    c = lax.broadcasted_iota(jnp.int32, (rows, rows), 1)
    tri = (c <= r).astype(BF16)

    def body(i, carry):
        start = pl.multiple_of(i * rows, rows)
        z = f_ref[pl.ds(start, rows), :] + bias_ref[...]
        ls = -_softplus(-z)
        cs = _mask_dot(tri, ls) + carry
        c_ref[:, pl.ds(start, rows)] = (cs * LOG2E).T[0:8, :]
        return cs[rows - 1:rows, :]

    lax.fori_loop(0, S // rows, body, jnp.zeros((1, LANE), F32))


def _fox_c(h, bias_pad, B, S):
    rows = min(256, S)
    return pl.pallas_call(
        functools.partial(_fox_c_kernel, rows=rows),
        name="fox_c",
        out_shape=jax.ShapeDtypeStruct((B, 8, S), F32),
        grid=(B,),
        in_specs=[pl.BlockSpec((S, LANE), lambda b: (b, H_FF0 // LANE)),
                  pl.BlockSpec((1, LANE), lambda b: (0, 0))],
        out_specs=pl.BlockSpec((None, 8, S), lambda b: (b, 0, 0)),
        compiler_params=_cparams(("parallel",)),
    )(h, bias_pad)


def _fox_attn_kernel(q_ref, k_ref, v_ref, c_ref, o_ref, s0_ref, *, tq, tk):
    i = pl.program_id(2)
    d = q_ref.shape[1]
    q0 = pl.multiple_of(i * tq, tq)
    c0 = c_ref[:, pl.ds(q0, tq)][:, 0:1]
    q = q_ref[...]

    def scores(start):
        bias = c0 - c_ref[:, pl.ds(start, tk)]
        return _dot_nt(q, k_ref[pl.ds(start, tk), :]) + bias

    ones = jnp.ones((tk, d), BF16)

    def update(start, sc, carry, masked):
        m, acc = carry
        if masked:
            row = lax.broadcasted_iota(jnp.int32, (tq, tk), 0) + q0
            col = lax.broadcasted_iota(jnp.int32, (tq, tk), 1) + start
            sc = jnp.where(col <= row, sc, NEG_BIG)
        m_new = jnp.maximum(m, jnp.max(sc, axis=1, keepdims=True))
        alpha = jnp.exp2(m - m_new)
        p = jnp.exp2((sc - m_new).astype(BF16))
        vb = jnp.concatenate([v_ref[pl.ds(start, tk), :], ones], axis=1)
        acc = alpha * acc + _dot(p, vb)
        return m_new, acc

    U = FOX_BLOCKS_PER_TRIP

    D = tq // tk

    def walk(base, n, carry, last):
        s_cur = s0_ref[...]
        for u in range(n):
            start = pl.multiple_of(base + u * tk, tk)
            final = last and u == n - 1
            s_next = None if final else scores(pl.multiple_of(start + tk, tk))
            carry = update(start, s_cur, carry, last and u >= n - D)
            s_cur = s_next
        if not last:
            s0_ref[...] = s_cur
        return carry

    s0_ref[...] = scores(0)
    init = (jnp.full((tq, 1), NEG_BIG, F32), jnp.zeros((tq, 2 * d), F32))
    n_full = i * D
    trips = n_full // U
    carry = lax.fori_loop(
        0, trips, lambda t, c: walk(pl.multiple_of(t * (U * tk), tk), U, c, False), init)

    base = pl.multiple_of(trips * (U * tk), tk)
    rems = list(range(0, U, D))
    tails = [functools.partial(walk, base, r + D, last=True) for r in rems]
    m, acc = lax.switch((n_full - trips * U) // D, tails, carry)
    o_ref[...] = (acc[:, :d] / acc[:, d:]).astype(o_ref.dtype)


def _fox_attention(hq, c3, B, S):
    tk = min(512, S)
    tq = min(FOX_Q_ROWS, S)
    nq = S // tq
    nh = FOX_HEADS
    return pl.pallas_call(
        functools.partial(_fox_attn_kernel, tq=tq, tk=tk),
        name="fox_attn",
        scratch_shapes=[pltpu.VMEM((tq, tk), F32)],
        out_shape=jax.ShapeDtypeStruct((B * S, FOX_W), BF16),
        grid=(B, FOX_HEADS, nq),
        in_specs=[pl.BlockSpec((tq, LANE), lambda b, hd, i: (b * nq + i, hd)),
                  pl.BlockSpec((S, LANE), lambda b, hd, i: (b, nh + hd)),
                  pl.BlockSpec((S, LANE), lambda b, hd, i: (b, 2 * nh + hd)),
                  pl.BlockSpec((None, 1, S), lambda b, hd, i: (b * nh + hd, 0, 0))],
        out_specs=pl.BlockSpec((tq, LANE), lambda b, hd, i: (b * nq + i, hd)),
        compiler_params=_cparams(("parallel", "parallel", "arbitrary")),
    )(hq, hq, hq, c3)


def _rwkv_premix(h, prev, mu, w0, wup, a0, aup, gup):
    rows = lax.broadcasted_iota(jnp.int32, h.shape, 0)
    hp = jnp.where(rows == 0, prev, pltpu.roll(h, 1, 0))
    hs = h + (hp - h) * mu
    W3 = 3 * RWKV_W
    r = hs[:, :RWKV_W]
    k = hs[:, RWKV_W:2 * RWKV_W]
    v = hs[:, 2 * RWKV_W:W3]
    wd = hs[:, W3:W3 + DECAY_LORA]
    ad = hs[:, W3 + DECAY_LORA:W3 + DECAY_LORA + ICLR_LORA]
    gd = hs[:, W3 + DECAY_LORA + ICLR_LORA:]
    z = w0 + _dot(jnp.tanh(wd).astype(BF16), wup)
    w = -_softplus(-z) - 0.5
    lw = -jnp.exp(w)
    a = _sigmoid(a0 + _dot(ad.astype(BF16), aup))
    g = _dot(_sigmoid(gd).astype(BF16), gup)
    return r, k, v, lw, a, g


def _head_sums(x, e_ref):
    hi, lo = _split2(x)
    e = e_ref[...]
    e_last = e[:LANE, :LANE]
    outs = []
    for c0 in range(0, RWKV_W, 2 * LANE):
        w = min(2 * LANE, RWKV_W - c0)
        ee = e if w == 2 * LANE else e_last
        outs.append(_dot(hi[:, c0:c0 + w], ee) + _dot(lo[:, c0:c0 + w], ee))
    return jnp.concatenate(outs, axis=1)


def _rwkv_scan_kernel(h_ref, hprev_ref, mu_ref, w0_ref, wup_ref, a0_ref, aup_ref, gup_ref,
                      kk_ref, ka_ref, rk_ref, gw_ref, gb_ref, e_ref, o_ref, st_ref, *, nchunk):
    C = RWKV_CHUNK
    N = RWKV_HEAD_DIM
    T = nchunk * C
    first = pl.program_id(1) == 0

    @pl.when(first)
    def _():
        st_ref[...] = jnp.zeros(st_ref.shape, F32)

    prev = jnp.where(first, 0.0, hprev_ref[7:8, :])
    r, k, v, lw, a, g = _rwkv_premix(h_ref[...], prev, mu_ref[...], w0_ref[...], wup_ref[...],
                                     a0_ref[...], aup_ref[...], gup_ref[...])

    ri = lax.broadcasted_iota(jnp.int32, (C, C), 0)
    ci = lax.broadcasted_iota(jnp.int32, (C, C), 1)
    incl = ci <= ri
    strict = ci < ri

    rt = lax.broadcasted_iota(jnp.int32, (T, T), 0)
    ct = lax.broadcasted_iota(jnp.int32, (T, T), 1)
    tri = ((ct <= rt) & (ct >= (rt // C) * C)).astype(BF16)
    lw_hi, lw_lo = _split2(lw)
    cum = _dot(tri, lw_hi) + _dot(tri, lw_lo)
    cls = [cum[(c + 1) * C - 1:(c + 1) * C, :] for c in range(nchunk)]
    cl_rows = jnp.concatenate([jnp.broadcast_to(t, (C, RWKV_W)) for t in cls], axis=0)
    e_in = jnp.exp(cum)
    e_ex = jnp.exp(cum - lw)
    e_neg = jnp.exp(-cum)
    e_tail = jnp.exp(cl_rows - cum)
    pcs = [jnp.exp(t) for t in cls]

    kk = k * kk_ref[...]
    kkn = kk / jnp.maximum(jnp.sqrt(_head_sums(kk * kk, e_ref)), 1e-12)
    k2 = k * (1.0 + (a - 1.0) * ka_ref[...])
    bvec = kkn * a
    At_f = -kkn * e_ex
    Rt_f = r * e_in
    AtB = At_f.astype(BF16)
    RtB = Rt_f.astype(BF16)
    BtB = (bvec * e_neg).astype(BF16)
    KtB = (k2 * e_neg).astype(BF16)
    BpB = (bvec * e_tail).astype(BF16)
    KpB = (k2 * e_tail).astype(BF16)
    vB = v.astype(BF16)

    HD = range(RWKV_HEADS)
    hsl = [slice(hd * N, (hd + 1) * N) for hd in HD]
    sls = [(slice(c * C, (c + 1) * C), hs) for c in range(nchunk) for hs in hsl]
    At = [At_f[rs, hs] for rs, hs in sls]
    Rt = [Rt_f[rs, hs] for rs, hs in sls]
    vb = [vB[rs, hs] for rs, hs in sls]
    Bt = [BtB[rs, hs] for rs, hs in sls]
    Kt = [KtB[rs, hs] for rs, hs in sls]
    Bp = [BpB[rs, hs] for rs, hs in sls]
    Kp = [KpB[rs, hs] for rs, hs in sls]

    AR = [jnp.concatenate([AtB[rs, hs], RtB[rs, hs]], axis=0) for rs, hs in sls]
    BK = [jnp.concatenate([x, y], axis=0) for x, y in zip(Bt, Kt)]
    G = [_dot_nt(x, y) for x, y in zip(AR, BK)]
    r2 = lax.broadcasted_iota(jnp.int32, (C, 2 * C), 0)
    c2 = lax.broadcasted_iota(jnp.int32, (C, 2 * C), 1)
    incl2 = jnp.where(c2 >= C, c2 - C, c2) <= r2
    P = [jnp.where(strict, t[:C, :C], 0.0) for t in G]
    Lak = [jnp.where(strict, t[:C, C:], 0.0).astype(BF16) for t in G]
    L2 = [jnp.where(incl2, t[C:], 0.0).astype(BF16) for t in G]

    X = [jnp.concatenate([x, _dot(y, z)], axis=1) for x, y, z in zip(At, Lak, vb)]
    steps = C.bit_length() - 1
    for it in range(steps):
        Pb = [t.astype(BF16) for t in P]
        Xb = [x.astype(BF16) for x in X]
        if it + 1 < steps:
            PX = [_dot(p, jnp.concatenate([x, p], axis=1)) for p, x in zip(Pb, Xb)]
            X = [x + t[:, :2 * N] for x, t in zip(X, PX)]
            P = [t[:, 2 * N:] for t in PX]
        else:
            X = [x + _dot(p, xb) for x, p, xb in zip(X, Pb, Xb)]
    Xb = [x.astype(BF16) for x in X]

    zpad = jnp.zeros((C, N), BF16)
    RHS = [jnp.concatenate([x, jnp.concatenate([zpad, z], axis=1)], axis=0)
           for x, z in zip(Xb, vb)]
    LX = [_dot(x, y) for x, y in zip(L2, RHS)]
    Qp = [(x + y[:, :N]).astype(BF16) for x, y in zip(Rt, LX)]
    Y0 = [y[:, N:] for y in LX]

    XB = [_dot_tn(x, b) for x, b in zip(Xb, Bp)]
    Mw = [t[:N].astype(BF16) for t in XB]
    Nn = [t[N:] + _dot_tn(x, y) for t, x, y in zip(XB, vb, Kp)]

    S = [st_ref[hd] for hd in HD]
    y_rows = []
    for c in range(nchunk):
        ch = [c * RWKV_HEADS + hd for hd in HD]
        Ssp = [_split2(s) for s in S]
        ys = [_dot_nt(Qp[j], sp[0]) + Y0[j] for j, sp in zip(ch, Ssp)]
        y_rows.append(jnp.concatenate(ys, axis=1))
        SM = [_dot(jnp.concatenate(sp, axis=0), Mw[j]) for sp, j in zip(Ssp, ch)]
        S = [s * pcs[c][:, hs] + sm[:N] + sm[N:] + Nn[j]
             for s, sm, j, hs in zip(S, SM, ch, hsl)]
    for hd in HD:
        st_ref[hd] = S[hd]

    y = jnp.concatenate(y_rows, axis=0)
    inv_n = 1.0 / N
    yc = y - _head_sums(y, e_ref) * inv_n
    var = _head_sums(yc * yc, e_ref) * inv_n
    yn = yc * lax.rsqrt(var + RWKV_GN_EPS) * gw_ref[...] + gb_ref[...]
    bonus = _head_sums(r * k2 * rk_ref[...], e_ref) * v
    o_ref[...] = ((yn + bonus) * g).astype(o_ref.dtype)


def _rwkv_mix(h, mu, w0, wup, a0, aup, gup, k_k, k_a, r_k, gn_w, gn_b, B, S):
    nchunk = RWKV_CHUNKS_PER_STEP
    T = nchunk * RWKV_CHUNK
    nc = S // T
    row = lambda b, c: (b * nc + c, 0)
    prev_row = lambda b, c: (jnp.maximum((b * nc + c) * (T // 8) - 1, 0), 0)
    const = lambda b, c: (0, 0)
    vec = pl.BlockSpec((1, RWKV_W), const)
    hid = jnp.arange(2 * LANE, dtype=jnp.int32) // RWKV_HEAD_DIM
    head_ones = (hid[:, None] == hid[None, :]).astype(BF16)
    return pl.pallas_call(
        functools.partial(_rwkv_scan_kernel, nchunk=nchunk),
        name="rwkv_scan",
        out_shape=jax.ShapeDtypeStruct((B * S, RWKV_W), BF16),
        grid=(B, nc),
        in_specs=[pl.BlockSpec((T, RWKV_COLS), row),
                  pl.BlockSpec((8, RWKV_COLS), prev_row),
                  pl.BlockSpec((1, RWKV_COLS), const),
                  vec,
                  pl.BlockSpec((DECAY_LORA, RWKV_W), const),
                  vec,
                  pl.BlockSpec((ICLR_LORA, RWKV_W), const),
                  pl.BlockSpec((GATE_LORA, RWKV_W), const),
                  vec, vec, vec, vec, vec,
                  pl.BlockSpec((2 * LANE, 2 * LANE), const)],
        out_specs=pl.BlockSpec((T, RWKV_W), row),
        scratch_shapes=[pltpu.VMEM((RWKV_HEADS, RWKV_HEAD_DIM, RWKV_HEAD_DIM), F32)],
        compiler_params=_cparams(("parallel", "arbitrary")),
    )(h, h, mu, w0, wup, a0, aup, gup, k_k, k_a, r_k, gn_w, gn_b, head_ones)


def _ret_kernel(h_ref, cos_ref, sin_ref, dmat_ref, qd_ref, kd_ref, cd_ref, gw_ref, gb_ref,
                o_ref, R_ref, *, nchunk):
    C = RET_CHUNK
    d = RET_HEAD_DIM
    H = RET_HEADS
    W = RET_W

    @pl.when(pl.program_id(1) == 0)
    def _():
        R_ref[...] = jnp.zeros(R_ref.shape, F32)

    pairs = [(c, hd) for c in range(nchunk) for hd in range(H)]
    rows = lambda c: slice(c * C, (c + 1) * C)
    cols = lambda part, hd: slice(part * W + hd * d, part * W + (hd + 1) * d)
    cos = [cos_ref[rows(c), :] for c in range(nchunk)]
    sin = [sin_ref[rows(c), :] for c in range(nchunk)]
    q = [h_ref[rows(c), cols(0, hd)] for c, hd in pairs]
    k = [h_ref[rows(c), cols(1, hd)] for c, hd in pairs]
    qr = [x * cos[c] + pltpu.roll(x, d // 2, 1) * sin[c] for x, (c, hd) in zip(q, pairs)]
    kr = [(x * cos[c] + pltpu.roll(x, d // 2, 1) * sin[c]) * (d ** -0.5)
          for x, (c, hd) in zip(k, pairs)]
    vb = [h_ref[rows(c), cols(2, hd)].astype(BF16) for c, hd in pairs]
    qb = [x.astype(BF16) for x in qr]
    kb = [x.astype(BF16) for x in kr]
    kdb = [(x * kd_ref[hd]).astype(BF16) for x, (c, hd) in zip(kr, pairs)]
    inner = [(_dot_nt(x, y) * dmat_ref[hd]).astype(BF16) for x, y, (c, hd) in zip(qb, kb, pairs)]
    o_in = [_dot(x, y) for x, y in zip(inner, vb)]
    kv = [_dot_tn(x, y) for x, y in zip(kdb, vb)]

    R = [R_ref[hd] for hd in range(H)]
    outs = []
    for c in range(nchunk):
        ch = [c * H + hd for hd in range(H)]
        Rsp = [_split2(x) for x in R]
        outs += [o_in[j] + (_dot(qb[j], sp[0]) + _dot(qb[j], sp[1])) * qd_ref[hd]
                 for hd, (j, sp) in enumerate(zip(ch, Rsp))]
        R = [cd_ref[hd][0:1, :] * x + kv[j] for hd, (x, j) in enumerate(zip(R, ch))]
    for hd in range(H):
        R_ref[hd] = R[hd]

    for o, (c, hd) in zip(outs, pairs):
        mu = jnp.mean(o, axis=1, keepdims=True)
        oc = o - mu
        var = jnp.mean(oc * oc, axis=1, keepdims=True)
        gsl = slice(hd * d, (hd + 1) * d)
        on = oc * lax.rsqrt(var + RET_GN_EPS) * gw_ref[:, gsl] + gb_ref[:, gsl]
        gate = h_ref[rows(c), cols(3, hd)]
        o_ref[rows(c), gsl] = (gate * _sigmoid(gate) * on).astype(o_ref.dtype)


def _ret_tables(S):
    C, d, H = RET_CHUNK, RET_HEAD_DIM, RET_HEADS
    half = d // 2
    inv = 1.0 / (ROPE_BASE ** (jnp.arange(half, dtype=F32) / half))
    ang = jnp.arange(S, dtype=F32)[:, None] * inv[None, :]
    cos = jnp.cos(ang)
    sin = jnp.sin(ang)
    cos2 = jnp.concatenate([cos, cos], axis=1)
    sin2 = jnp.concatenate([-sin, sin], axis=1)
    log_g = jnp.log(1.0 - 2.0 ** (-5.0 - jnp.arange(H, dtype=F32)))
    pos = jnp.arange(C, dtype=F32)
    rel = pos[:, None] - pos[None, :]
    dmat = jnp.where(rel >= 0, jnp.exp(log_g[:, None, None] * jnp.maximum(rel, 0.0)), 0.0)
    qd = jnp.broadcast_to(jnp.exp(log_g[:, None] * (pos + 1.0))[..., None], (H, C, d))
    kd = jnp.broadcast_to(jnp.exp(log_g[:, None] * (C - 1.0 - pos))[..., None], (H, C, d))
    cd = jnp.broadcast_to(jnp.exp(log_g * C)[:, None, None], (H, 8, d))
    return cos2, sin2, dmat, qd, kd, cd


def _retention(h, tables, gn_w, gn_b, B, S):
    cos2, sin2, dmat, qd, kd, cd = tables
    C = RET_CHUNK
    T = min(512, S)
    nt = S // T
    H = RET_HEADS
    tab = pl.BlockSpec((T, LANE), lambda b, t: (t, 0))
    per_head = lambda rows: pl.BlockSpec((H, rows, LANE), lambda b, t: (0, 0, 0))
    vec = pl.BlockSpec((1, RET_W), lambda b, t: (0, 0))
    return pl.pallas_call(
        functools.partial(_ret_kernel, nchunk=T // C),
        name="retention",
        out_shape=jax.ShapeDtypeStruct((B * S, RET_W), BF16),
        grid=(B, nt),
        in_specs=[pl.BlockSpec((T, RET_COLS), lambda b, t: (b * nt + t, 0)),
                  tab, tab, per_head(C), per_head(C), per_head(C), per_head(8), vec, vec],
        out_specs=pl.BlockSpec((T, RET_W), lambda b, t: (b * nt + t, 0)),
        scratch_shapes=[pltpu.VMEM((H, RET_HEAD_DIM, RET_HEAD_DIM), F32)],
        compiler_params=_cparams(("parallel", "arbitrary")),
    )(h, cos2, sin2, dmat, qd, kd, cd, gn_w, gn_b)


def _outproj_kernel(x_ref, yf_ref, yr_ref, yt_ref, w_ref, g_ref, b_ref, o_ref):
    r0, r1 = FOX_W, FOX_W + RWKV_W
    y = (_dot(yf_ref[...], w_ref[:r0, :]) + _dot(yr_ref[...], w_ref[r0:r1, :])
         + _dot(yt_ref[...], w_ref[r1:, :]))
    o_ref[...] = _layer_norm_rows(ALPHA * x_ref[...] + y, g_ref[...], b_ref[...])


def _out_proj_ln(x2d, y_fox, y_rwkv, y_ret, w_out, l, g, b):
    M, D = x2d.shape
    tm = min(512, M)
    row = lambda i: (i, 0)
    const = lambda i: (0, 0)
    return pl.pallas_call(
        _outproj_kernel,
        name="out_proj_ln",
        out_shape=jax.ShapeDtypeStruct((M, D), F32),
        grid=(M // tm,),
        in_specs=[pl.BlockSpec((tm, D), row), pl.BlockSpec((tm, FOX_W), row),
                  pl.BlockSpec((tm, RWKV_W), row), pl.BlockSpec((tm, RET_W), row),
                  pl.BlockSpec((None, D, D), lambda i: (l, 0, 0)),
                  pl.BlockSpec((1, D), const), pl.BlockSpec((1, D), const)],
        out_specs=pl.BlockSpec((tm, D), row),
        compiler_params=_cparams(("parallel",)),
    )(x2d, y_fox, y_rwkv, y_ret, w_out, g, b)


def _ffn_kernel(x_ref, wu_ref, wd_ref, g_ref, b_ref, o_ref, ob_ref, acc_ref, xb_ref):
    f = pl.program_id(1)

    @pl.when(f == 0)
    def _():
        xb_ref[...] = x_ref[...].astype(BF16)
        acc_ref[...] = jnp.zeros(acc_ref.shape, F32)

    hid = jnp.maximum(_dot(xb_ref[...], wu_ref[...]), 0.0)
    acc_ref[...] += _dot((hid * hid).astype(BF16), wd_ref[...])

    @pl.when(f == pl.num_programs(1) - 1)
    def _():
        y = _layer_norm_rows(ALPHA * x_ref[...] + acc_ref[...], g_ref[...], b_ref[...])
        o_ref[...] = y
        ob_ref[...] = y.astype(BF16)


def _ffn_ln(x2d, w_up, w_down, l, g, b):
    M, D = x2d.shape
    F = w_up.shape[2]
    tm = min(512, M)
    tf = 1024
    return pl.pallas_call(
        _ffn_kernel,
        name="ffn_ln",
        out_shape=(jax.ShapeDtypeStruct((M, D), F32), jax.ShapeDtypeStruct((M, D), BF16)),
        grid=(M // tm, F // tf),
        in_specs=[pl.BlockSpec((tm, D), lambda i, f: (i, 0)),
                  pl.BlockSpec((None, D, tf), lambda i, f: (l, 0, f)),
                  pl.BlockSpec((None, tf, D), lambda i, f: (l, f, 0)),
                  pl.BlockSpec((1, D), lambda i, f: (0, 0)),
                  pl.BlockSpec((1, D), lambda i, f: (0, 0))],
        out_specs=(pl.BlockSpec((tm, D), lambda i, f: (i, 0)),
                   pl.BlockSpec((tm, D), lambda i, f: (i, 0))),
        scratch_shapes=[pltpu.VMEM((tm, D), F32), pltpu.VMEM((tm, D), BF16)],
        compiler_params=_cparams(("parallel", "arbitrary")),
    )(x2d, w_up, w_down, g, b)


def _permute_w_in(w_in):
    L, D, _ = w_in.shape
    w_in = w_in.astype(BF16)
    w_fox = w_in[:, :, :FOX_COLS]
    w_rwkv = w_in[:, :, FOX_COLS:FOX_COLS + RWKV_COLS]
    w_ret = w_in[:, :, FOX_COLS + RWKV_COLS:]
    w_ff = w_fox[:, :, 3 * FOX_W:]
    pad = jnp.zeros((L, D, LANE - FOX_HEADS), BF16)
    w_q = w_fox[:, :, :3 * FOX_W]
    w_rw = jnp.concatenate([w_rwkv, w_ff, pad], axis=-1)
    return w_q, w_rw, w_ret


def _layer(x2d, xb, l, p, tables, B, S):
    hq = _in_proj(xb, p['w_q'], l, p['q_scale'])
    h = _in_proj(xb, p['w_rw'], l)
    ht = _in_proj(xb, p['w_ret'], l)
    c = _fox_c(h, p['fox_bias'], B, S)
    c3 = c[:, :FOX_HEADS, :].reshape(B * FOX_HEADS, 1, S)
    y_fox = _fox_attention(hq, c3, B, S)
    y_rwkv = _rwkv_mix(h, p['mu'], p['w0'], p['w_up_lora'], p['a0'], p['a_up'], p['g_up'],
                       p['k_k'], p['k_a'], p['r_k'], p['rgn_w'], p['rgn_b'], B, S)
    y_ret = _retention(ht, tables, p['tgn_w'], p['tgn_b'], B, S)
    x2d = _out_proj_ln(x2d, y_fox, y_rwkv, y_ret, p['w_out'], l, p['ln1_g'], p['ln1_b'])
    return _ffn_ln(x2d, p['w_up'], p['w_down'], l, p['ln2_g'], p['ln2_b'])


def kernel(x, w_in, fox_forget_bias, rwkv_mu, rwkv_w0, rwkv_w_up, rwkv_a0, rwkv_a_up, rwkv_g_up,
           rwkv_k_k, rwkv_k_a, rwkv_r_k, rwkv_gn_w, rwkv_gn_b, ret_gn_w, ret_gn_b, w_out, ln1_g,
           ln1_b, w_up, w_down, ln2_g, ln2_b):
    B, S, D = x.shape
    L = w_in.shape[0]
    w_q, w_rw, w_ret = _permute_w_in(w_in)
    q_scale = jnp.concatenate([jnp.full((1, FOX_W), FOX_Q_SCALE, F32),
                               jnp.ones((1, 2 * FOX_W), F32)], axis=1)
    w_out_b = w_out.astype(BF16)
    w_up_b = w_up.astype(BF16)
    w_down_b = w_down.astype(BF16)
    bias_pad = jnp.pad(fox_forget_bias, ((0, 0), (0, LANE - FOX_HEADS)))
    tables = _ret_tables(S)
    row = lambda t: t.reshape(1, -1)
    x2d = x.reshape(B * S, D)
    xb = x2d.astype(BF16)
    for l in range(L):
        p = {
            'w_rw': w_rw, 'w_ret': w_ret, 'w_q': w_q, 'q_scale': q_scale,
            'fox_bias': row(bias_pad[l]),
            'mu': row(rwkv_mu[l]), 'w0': row(rwkv_w0[l]), 'w_up_lora': rwkv_w_up[l].astype(BF16),
            'a0': row(rwkv_a0[l]), 'a_up': rwkv_a_up[l].astype(BF16),
            'g_up': rwkv_g_up[l].astype(BF16),
            'k_k': row(rwkv_k_k[l]), 'k_a': row(rwkv_k_a[l]), 'r_k': row(rwkv_r_k[l]),
            'rgn_w': row(rwkv_gn_w[l]), 'rgn_b': row(rwkv_gn_b[l]),
            'tgn_w': row(ret_gn_w[l]), 'tgn_b': row(ret_gn_b[l]),
            'w_out': w_out_b,
            'ln1_g': row(ln1_g[l]), 'ln1_b': row(ln1_b[l]),
            'w_up': w_up_b, 'w_down': w_down_b,
            'ln2_g': row(ln2_g[l]), 'ln2_b': row(ln2_b[l]),
        }
        x2d, xb = _layer(x2d, xb, l, p, tables, B, S)
    return x2d.reshape(B, S, D)
```

```python
import functools

import jax
import jax.numpy as jnp
from jax import lax
from jax.experimental import pallas as pl
from jax.experimental.pallas import tpu as pltpu

F32 = jnp.float32
BF16 = jnp.bfloat16

D_MODEL = 2048
DEPTH = 4
FOX_W = 768
RWKV_W = 640
RET_W = 640
FOX_HEAD_DIM = 128
FOX_HEADS = 6
RWKV_HEAD_DIM = 64
RWKV_HEADS = 10
RET_HEAD_DIM = 128
RET_HEADS = 5
DECAY_LORA = 64
ICLR_LORA = 64
GATE_LORA = 128
D_FF = 4 * D_MODEL
RET_CHUNK = 128
ROPE_BASE = 10000.0
LN_EPS = 1e-5
RWKV_GN_EPS = 64e-5
RET_GN_EPS = 1e-5
ALPHA = (2 * DEPTH) ** 0.25

FOX_COLS = 3 * FOX_W + FOX_HEADS
RWKV_COLS = 3 * RWKV_W + DECAY_LORA + ICLR_LORA + GATE_LORA
RET_COLS = 4 * RET_W

LANE = 128
H_FF0 = RWKV_COLS
LOG2E = 1.4426950408889634
FOX_Q_SCALE = FOX_HEAD_DIM ** -0.5 * LOG2E

FOX_BLOCKS_PER_TRIP = 4
FOX_Q_ROWS = 1024
RWKV_CHUNK = 64
RWKV_CHUNKS_PER_STEP = 4
NEG_BIG = -1e30

VMEM_LIMIT = 56 * 1024 * 1024


def _cparams(sem):
    return pltpu.CompilerParams(dimension_semantics=sem, vmem_limit_bytes=VMEM_LIMIT)


def _dot(a, b):
    return jnp.dot(a, b, preferred_element_type=F32)


def _dot_nt(a, b):
    return lax.dot_general(a, b, (((1,), (1,)), ((), ())), preferred_element_type=F32)


def _dot_tn(a, b):
    return lax.dot_general(a, b, (((0,), (0,)), ((), ())), preferred_element_type=F32)


def _split3(x):
    hi = x.astype(BF16)
    r1 = x - hi.astype(F32)
    mid = r1.astype(BF16)
    lo = (r1 - mid.astype(F32)).astype(BF16)
    return hi, mid, lo


def _split2(x):
    hi = x.astype(BF16)
    lo = (x - hi.astype(F32)).astype(BF16)
    return hi, lo


def _mask_dot(mask_bf16, x):
    hi, mid, lo = _split3(x)
    return _dot(mask_bf16, hi) + _dot(mask_bf16, mid) + _dot(mask_bf16, lo)


def _softplus(z):
    return jnp.maximum(z, 0.0) + jnp.log1p(jnp.exp(-jnp.abs(z)))


def _sigmoid(z):
    return 1.0 / (1.0 + jnp.exp(-z))


def _layer_norm_rows(y, g, b):
    mu = jnp.mean(y, axis=-1, keepdims=True)
    yc = y - mu
    var = jnp.mean(yc * yc, axis=-1, keepdims=True)
    return yc * lax.rsqrt(var + LN_EPS) * g + b


def _inproj_kernel(x_ref, w_ref, o_ref):
    o_ref[...] = _dot(x_ref[...].astype(BF16), w_ref[...])


def _inproj_scaled_kernel(x_ref, w_ref, s_ref, o_ref):
    o_ref[...] = (_dot(x_ref[...].astype(BF16), w_ref[...]) * s_ref[...]).astype(o_ref.dtype)


def _in_proj(xb, w, l, col_scale=None):
    M, K = xb.shape
    N = w.shape[2]
    tm = min(512, M)
    tn = N
    in_specs = [pl.BlockSpec((tm, K), lambda j, i: (i, 0)),
                pl.BlockSpec((None, K, tn), lambda j, i: (l, 0, j))]
    args = [xb, w]
    if col_scale is not None:
        in_specs.append(pl.BlockSpec((1, tn), lambda j, i: (0, j)))
        args.append(col_scale)
    return pl.pallas_call(
        _inproj_kernel if col_scale is None else _inproj_scaled_kernel,
        name="in_proj",
        out_shape=jax.ShapeDtypeStruct((M, N), F32 if col_scale is None else BF16),
        grid=(N // tn, M // tm),
        in_specs=in_specs,
        out_specs=pl.BlockSpec((tm, tn), lambda j, i: (i, j)),
        compiler_params=_cparams(("parallel", "parallel")),
    )(*args)


def _fox_c_kernel(f_ref, bias_ref, c_ref, *, rows):
    S = f_ref.shape[0]
    r = lax.broadcasted_iota(jnp.int32, (rows, rows), 0)
    c = lax.broadcasted_iota(jnp.int32, (rows, rows), 1)
    tri = (c <= r).astype(BF16)

    def body(i, carry):
        start = pl.multiple_of(i * rows, rows)
        z = f_ref[pl.ds(start, rows), :] + bias_ref[...]
        ls = -_softplus(-z)
        cs = _mask_dot(tri, ls) + carry
        c_ref[:, pl.ds(start, rows)] = (cs * LOG2E).T[0:8, :]
        return cs[rows - 1:rows, :]

    lax.fori_loop(0, S // rows, body, jnp.zeros((1, LANE), F32))


def _fox_c(h, bias_pad, B, S):
    rows = min(256, S)
    return pl.pallas_call(
        functools.partial(_fox_c_kernel, rows=rows),
        name="fox_c",
        out_shape=jax.ShapeDtypeStruct((B, 8, S), F32),
        grid=(B,),
        in_specs=[pl.BlockSpec((S, LANE), lambda b: (b, H_FF0 // LANE)),
                  pl.BlockSpec((1, LANE), lambda b: (0, 0))],
        out_specs=pl.BlockSpec((None, 8, S), lambda b: (b, 0, 0)),
        compiler_params=_cparams(("parallel",)),
    )(h, bias_pad)


def _fox_attn_kernel(q_ref, k_ref, v_ref, c_ref, o_ref, s0_ref, *, tq, tk):
    i = pl.program_id(2)
    d = q_ref.shape[1]
    q0 = pl.multiple_of(i * tq, tq)
    c0 = c_ref[:, pl.ds(q0, tq)][:, 0:1]
    q = q_ref[...]

    def scores(start):
        bias = c0 - c_ref[:, pl.ds(start, tk)]
        return _dot_nt(q, k_ref[pl.ds(start, tk), :]) + bias

    ones = jnp.ones((tk, d), BF16)

    def update(start, sc, carry, masked):
        m, acc = carry
        if masked:
            row = lax.broadcasted_iota(jnp.int32, (tq, tk), 0) + q0
            col = lax.broadcasted_iota(jnp.int32, (tq, tk), 1) + start
            sc = jnp.where(col <= row, sc, NEG_BIG)
        m_new = jnp.maximum(m, jnp.max(sc, axis=1, keepdims=True))
        alpha = jnp.exp2(m - m_new)
        p = jnp.exp2((sc - m_new).astype(BF16))
        vb = jnp.concatenate([v_ref[pl.ds(start, tk), :], ones], axis=1)
        acc = alpha * acc + _dot(p, vb)
        return m_new, acc

    U = FOX_BLOCKS_PER_TRIP

    D = tq // tk

    def walk(base, n, carry, last):
        s_cur = s0_ref[...]
        for u in range(n):
            start = pl.multiple_of(base + u * tk, tk)
            final = last and u == n - 1
            s_next = None if final else scores(pl.multiple_of(start + tk, tk))
            carry = update(start, s_cur, carry, last and u >= n - D)
            s_cur = s_next
        if not last:
            s0_ref[...] = s_cur
        return carry

    s0_ref[...] = scores(0)
    init = (jnp.full((tq, 1), NEG_BIG, F32), jnp.zeros((tq, 2 * d), F32))
    n_full = i * D
    trips = n_full // U
    carry = lax.fori_loop(
        0, trips, lambda t, c: walk(pl.multiple_of(t * (U * tk), tk), U, c, False), init)

    base = pl.multiple_of(trips * (U * tk), tk)
    rems = list(range(0, U, D))
    tails = [functools.partial(walk, base, r + D, last=True) for r in rems]
    m, acc = lax.switch((n_full - trips * U) // D, tails, carry)
    o_ref[...] = (acc[:, :d] / acc[:, d:]).astype(o_ref.dtype)


def _fox_attention(hq, c3, B, S):
    tk = min(512, S)
    tq = min(FOX_Q_ROWS, S)
    nq = S // tq
    nh = FOX_HEADS
    return pl.pallas_call(
        functools.partial(_fox_attn_kernel, tq=tq, tk=tk),
        name="fox_attn",
        scratch_shapes=[pltpu.VMEM((tq, tk), F32)],
        out_shape=jax.ShapeDtypeStruct((B * S, FOX_W), BF16),
        grid=(B, FOX_HEADS, nq),
        in_specs=[pl.BlockSpec((tq, LANE), lambda b, hd, i: (b * nq + i, hd)),
                  pl.BlockSpec((S, LANE), lambda b, hd, i: (b, nh + hd)),
                  pl.BlockSpec((S, LANE), lambda b, hd, i: (b, 2 * nh + hd)),
                  pl.BlockSpec((None, 1, S), lambda b, hd, i: (b * nh + hd, 0, 0))],
        out_specs=pl.BlockSpec((tq, LANE), lambda b, hd, i: (b * nq + i, hd)),
        compiler_params=_cparams(("parallel", "parallel", "arbitrary")),
    )(hq, hq, hq, c3)


def _rwkv_premix(h, prev, mu, w0, wup, a0, aup, gup):
    rows = lax.broadcasted_iota(jnp.int32, h.shape, 0)
    hp = jnp.where(rows == 0, prev, pltpu.roll(h, 1, 0))
    hs = h + (hp - h) * mu
    W3 = 3 * RWKV_W
    r = hs[:, :RWKV_W]
    k = hs[:, RWKV_W:2 * RWKV_W]
    v = hs[:, 2 * RWKV_W:W3]
    wd = hs[:, W3:W3 + DECAY_LORA]
    ad = hs[:, W3 + DECAY_LORA:W3 + DECAY_LORA + ICLR_LORA]
    gd = hs[:, W3 + DECAY_LORA + ICLR_LORA:]
    z = w0 + _dot(jnp.tanh(wd).astype(BF16), wup)
    w = -_softplus(-z) - 0.5
    lw = -jnp.exp(w)
    a = _sigmoid(a0 + _dot(ad.astype(BF16), aup))
    g = _dot(_sigmoid(gd).astype(BF16), gup)
    return r, k, v, lw, a, g


def _head_sums(x, e_ref):
    hi, lo = _split2(x)
    e = e_ref[...]
    e_last = e[:LANE, :LANE]
    outs = []
    for c0 in range(0, RWKV_W, 2 * LANE):
        w = min(2 * LANE, RWKV_W - c0)
        ee = e if w == 2 * LANE else e_last
        outs.append(_dot(hi[:, c0:c0 + w], ee) + _dot(lo[:, c0:c0 + w], ee))
    return jnp.concatenate(outs, axis=1)


def _rwkv_scan_kernel(h_ref, hprev_ref, mu_ref, w0_ref, wup_ref, a0_ref, aup_ref, gup_ref,
                      kk_ref, ka_ref, rk_ref, gw_ref, gb_ref, e_ref, o_ref, st_ref, *, nchunk):
    C = RWKV_CHUNK
    N = RWKV_HEAD_DIM
    T = nchunk * C
    first = pl.program_id(1) == 0

    @pl.when(first)
    def _():
        st_ref[...] = jnp.zeros(st_ref.shape, F32)

    prev = jnp.where(first, 0.0, hprev_ref[7:8, :])
    r, k, v, lw, a, g = _rwkv_premix(h_ref[...], prev, mu_ref[...], w0_ref[...], wup_ref[...],
                                     a0_ref[...], aup_ref[...], gup_ref[...])

    ri = lax.broadcasted_iota(jnp.int32, (C, C), 0)
    ci = lax.broadcasted_iota(jnp.int32, (C, C), 1)
    incl = ci <= ri
    strict = ci < ri

    rt = lax.broadcasted_iota(jnp.int32, (T, T), 0)
    ct = lax.broadcasted_iota(jnp.int32, (T, T), 1)
    tri = ((ct <= rt) & (ct >= (rt // C) * C)).astype(BF16)
    lw_hi, lw_lo = _split2(lw)
    cum = _dot(tri, lw_hi) + _dot(tri, lw_lo)
    cls = [cum[(c + 1) * C - 1:(c + 1) * C, :] for c in range(nchunk)]
    cl_rows = jnp.concatenate([jnp.broadcast_to(t, (C, RWKV_W)) for t in cls], axis=0)
    e_in = jnp.exp(cum)
    e_ex = jnp.exp(cum - lw)
    e_neg = jnp.exp(-cum)
    e_tail = jnp.exp(cl_rows - cum)
    pcs = [jnp.exp(t) for t in cls]

    kk = k * kk_ref[...]
    kkn = kk / jnp.maximum(jnp.sqrt(_head_sums(kk * kk, e_ref)), 1e-12)
    k2 = k * (1.0 + (a - 1.0) * ka_ref[...])
    bvec = kkn * a
    At_f = -kkn * e_ex
    Rt_f = r * e_in
    AtB = At_f.astype(BF16)
    RtB = Rt_f.astype(BF16)
    BtB = (bvec * e_neg).astype(BF16)
    KtB = (k2 * e_neg).astype(BF16)
    BpB = (bvec * e_tail).astype(BF16)
    KpB = (k2 * e_tail).astype(BF16)
    vB = v.astype(BF16)

    HD = range(RWKV_HEADS)
    hsl = [slice(hd * N, (hd + 1) * N) for hd in HD]
    sls = [(slice(c * C, (c + 1) * C), hs) for c in range(nchunk) for hs in hsl]
    At = [At_f[rs, hs] for rs, hs in sls]
    Rt = [Rt_f[rs, hs] for rs, hs in sls]
    vb = [vB[rs, hs] for rs, hs in sls]
    Bt = [BtB[rs, hs] for rs, hs in sls]
    Kt = [KtB[rs, hs] for rs, hs in sls]
    Bp = [BpB[rs, hs] for rs, hs in sls]
    Kp = [KpB[rs, hs] for rs, hs in sls]

    AR = [jnp.concatenate([AtB[rs, hs], RtB[rs, hs]], axis=0) for rs, hs in sls]
    BK = [jnp.concatenate([x, y], axis=0) for x, y in zip(Bt, Kt)]
    G = [_dot_nt(x, y) for x, y in zip(AR, BK)]
    r2 = lax.broadcasted_iota(jnp.int32, (C, 2 * C), 0)
    c2 = lax.broadcasted_iota(jnp.int32, (C, 2 * C), 1)
    incl2 = jnp.where(c2 >= C, c2 - C, c2) <= r2
    P = [jnp.where(strict, t[:C, :C], 0.0) for t in G]
    Lak = [jnp.where(strict, t[:C, C:], 0.0).astype(BF16) for t in G]
    L2 = [jnp.where(incl2, t[C:], 0.0).astype(BF16) for t in G]

    X = [jnp.concatenate([x, _dot(y, z)], axis=1) for x, y, z in zip(At, Lak, vb)]
    steps = C.bit_length() - 1
    for it in range(steps):
        Pb = [t.astype(BF16) for t in P]
        Xb = [x.astype(BF16) for x in X]
        if it + 1 < steps:
            PX = [_dot(p, jnp.concatenate([x, p], axis=1)) for p, x in zip(Pb, Xb)]
            X = [x + t[:, :2 * N] for x, t in zip(X, PX)]
            P = [t[:, 2 * N:] for t in PX]
        else:
            X = [x + _dot(p, xb) for x, p, xb in zip(X, Pb, Xb)]
    Xb = [x.astype(BF16) for x in X]

    zpad = jnp.zeros((C, N), BF16)
    RHS = [jnp.concatenate([x, jnp.concatenate([zpad, z], axis=1)], axis=0)
           for x, z in zip(Xb, vb)]
    LX = [_dot(x, y) for x, y in zip(L2, RHS)]
    Qp = [(x + y[:, :N]).astype(BF16) for x, y in zip(Rt, LX)]
    Y0 = [y[:, N:] for y in LX]

    XB = [_dot_tn(x, b) for x, b in zip(Xb, Bp)]
    Mw = [t[:N].astype(BF16) for t in XB]
    Nn = [t[N:] + _dot_tn(x, y) for t, x, y in zip(XB, vb, Kp)]

    S = [st_ref[hd] for hd in HD]
    y_rows = []
    for c in range(nchunk):
        ch = [c * RWKV_HEADS + hd for hd in HD]
        Ssp = [_split2(s) for s in S]
        ys = [_dot_nt(Qp[j], sp[0]) + Y0[j] for j, sp in zip(ch, Ssp)]
        y_rows.append(jnp.concatenate(ys, axis=1))
        SM = [_dot(jnp.concatenate(sp, axis=0), Mw[j]) for sp, j in zip(Ssp, ch)]
        S = [s * pcs[c][:, hs] + sm[:N] + sm[N:] + Nn[j]
             for s, sm, j, hs in zip(S, SM, ch, hsl)]
    for hd in HD:
        st_ref[hd] = S[hd]

    y = jnp.concatenate(y_rows, axis=0)
    inv_n = 1.0 / N
    yc = y - _head_sums(y, e_ref) * inv_n
    var = _head_sums(yc * yc, e_ref) * inv_n
    yn = yc * lax.rsqrt(var + RWKV_GN_EPS) * gw_ref[...] + gb_ref[...]
    bonus = _head_sums(r * k2 * rk_ref[...], e_ref) * v
    o_ref[...] = ((yn + bonus) * g).astype(o_ref.dtype)


def _rwkv_mix(h, mu, w0, wup, a0, aup, gup, k_k, k_a, r_k, gn_w, gn_b, B, S):
    nchunk = RWKV_CHUNKS_PER_STEP
    T = nchunk * RWKV_CHUNK
    nc = S // T
    row = lambda b, c: (b * nc + c, 0)
    prev_row = lambda b, c: (jnp.maximum((b * nc + c) * (T // 8) - 1, 0), 0)
    const = lambda b, c: (0, 0)
    vec = pl.BlockSpec((1, RWKV_W), const)
    hid = jnp.arange(2 * LANE, dtype=jnp.int32) // RWKV_HEAD_DIM
    head_ones = (hid[:, None] == hid[None, :]).astype(BF16)
    return pl.pallas_call(
        functools.partial(_rwkv_scan_kernel, nchunk=nchunk),
        name="rwkv_scan",
        out_shape=jax.ShapeDtypeStruct((B * S, RWKV_W), BF16),
        grid=(B, nc),
        in_specs=[pl.BlockSpec((T, RWKV_COLS), row),
                  pl.BlockSpec((8, RWKV_COLS), prev_row),
                  pl.BlockSpec((1, RWKV_COLS), const),
                  vec,
                  pl.BlockSpec((DECAY_LORA, RWKV_W), const),
                  vec,
                  pl.BlockSpec((ICLR_LORA, RWKV_W), const),
                  pl.BlockSpec((GATE_LORA, RWKV_W), const),
                  vec, vec, vec, vec, vec,
                  pl.BlockSpec((2 * LANE, 2 * LANE), const)],
        out_specs=pl.BlockSpec((T, RWKV_W), row),
        scratch_shapes=[pltpu.VMEM((RWKV_HEADS, RWKV_HEAD_DIM, RWKV_HEAD_DIM), F32)],
        compiler_params=_cparams(("parallel", "arbitrary")),
    )(h, h, mu, w0, wup, a0, aup, gup, k_k, k_a, r_k, gn_w, gn_b, head_ones)


def _ret_kernel(h_ref, cos_ref, sin_ref, dmat_ref, qd_ref, kd_ref, cd_ref, gw_ref, gb_ref,
                o_ref, R_ref, *, nchunk):
    C = RET_CHUNK
    d = RET_HEAD_DIM
    H = RET_HEADS
    W = RET_W

    @pl.when(pl.program_id(1) == 0)
    def _():
        R_ref[...] = jnp.zeros(R_ref.shape, F32)

    pairs = [(c, hd) for c in range(nchunk) for hd in range(H)]
    rows = lambda c: slice(c * C, (c + 1) * C)
    cols = lambda part, hd: slice(part * W + hd * d, part * W + (hd + 1) * d)
    cos = [cos_ref[rows(c), :] for c in range(nchunk)]
    sin = [sin_ref[rows(c), :] for c in range(nchunk)]
    q = [h_ref[rows(c), cols(0, hd)] for c, hd in pairs]
    k = [h_ref[rows(c), cols(1, hd)] for c, hd in pairs]
    qr = [x * cos[c] + pltpu.roll(x, d // 2, 1) * sin[c] for x, (c, hd) in zip(q, pairs)]
    kr = [(x * cos[c] + pltpu.roll(x, d // 2, 1) * sin[c]) * (d ** -0.5)
          for x, (c, hd) in zip(k, pairs)]
    vb = [h_ref[rows(c), cols(2, hd)].astype(BF16) for c, hd in pairs]
    qb = [x.astype(BF16) for x in qr]
    kb = [x.astype(BF16) for x in kr]
    kdb = [(x * kd_ref[hd]).astype(BF16) for x, (c, hd) in zip(kr, pairs)]
    inner = [(_dot_nt(x, y) * dmat_ref[hd]).astype(BF16) for x, y, (c, hd) in zip(qb, kb, pairs)]
    o_in = [_dot(x, y) for x, y in zip(inner, vb)]
    kv = [_dot_tn(x, y) for x, y in zip(kdb, vb)]

    R = [R_ref[hd] for hd in range(H)]
    outs = []
    for c in range(nchunk):
        ch = [c * H + hd for hd in range(H)]
        Rsp = [_split2(x) for x in R]
        outs += [o_in[j] + (_dot(qb[j], sp[0]) + _dot(qb[j], sp[1])) * qd_ref[hd]
                 for hd, (j, sp) in enumerate(zip(ch, Rsp))]
        R = [cd_ref[hd][0:1, :] * x + kv[j] for hd, (x, j) in enumerate(zip(R, ch))]
    for hd in range(H):
        R_ref[hd] = R[hd]

    for o, (c, hd) in zip(outs, pairs):
        mu = jnp.mean(o, axis=1, keepdims=True)
        oc = o - mu
        var = jnp.mean(oc * oc, axis=1, keepdims=True)
        gsl = slice(hd * d, (hd + 1) * d)
        on = oc * lax.rsqrt(var + RET_GN_EPS) * gw_ref[:, gsl] + gb_ref[:, gsl]
        gate = h_ref[rows(c), cols(3, hd)]
        o_ref[rows(c), gsl] = (gate * _sigmoid(gate) * on).astype(o_ref.dtype)


def _ret_tables(S):
    C, d, H = RET_CHUNK, RET_HEAD_DIM, RET_HEADS
    half = d // 2
    inv = 1.0 / (ROPE_BASE ** (jnp.arange(half, dtype=F32) / half))
    ang = jnp.arange(S, dtype=F32)[:, None] * inv[None, :]
    cos = jnp.cos(ang)
    sin = jnp.sin(ang)
    cos2 = jnp.concatenate([cos, cos], axis=1)
    sin2 = jnp.concatenate([-sin, sin], axis=1)
    log_g = jnp.log(1.0 - 2.0 ** (-5.0 - jnp.arange(H, dtype=F32)))
    pos = jnp.arange(C, dtype=F32)
    rel = pos[:, None] - pos[None, :]
    dmat = jnp.where(rel >= 0, jnp.exp(log_g[:, None, None] * jnp.maximum(rel, 0.0)), 0.0)
    qd = jnp.broadcast_to(jnp.exp(log_g[:, None] * (pos + 1.0))[..., None], (H, C, d))
    kd = jnp.broadcast_to(jnp.exp(log_g[:, None] * (C - 1.0 - pos))[..., None], (H, C, d))
    cd = jnp.broadcast_to(jnp.exp(log_g * C)[:, None, None], (H, 8, d))
    return cos2, sin2, dmat, qd, kd, cd


def _retention(h, tables, gn_w, gn_b, B, S):
    cos2, sin2, dmat, qd, kd, cd = tables
    C = RET_CHUNK
    T = min(512, S)
    nt = S // T
    H = RET_HEADS
    tab = pl.BlockSpec((T, LANE), lambda b, t: (t, 0))
    per_head = lambda rows: pl.BlockSpec((H, rows, LANE), lambda b, t: (0, 0, 0))
    vec = pl.BlockSpec((1, RET_W), lambda b, t: (0, 0))
    return pl.pallas_call(
        functools.partial(_ret_kernel, nchunk=T // C),
        name="retention",
        out_shape=jax.ShapeDtypeStruct((B * S, RET_W), BF16),
        grid=(B, nt),
        in_specs=[pl.BlockSpec((T, RET_COLS), lambda b, t: (b * nt + t, 0)),
                  tab, tab, per_head(C), per_head(C), per_head(C), per_head(8), vec, vec],
        out_specs=pl.BlockSpec((T, RET_W), lambda b, t: (b * nt + t, 0)),
        scratch_shapes=[pltpu.VMEM((H, RET_HEAD_DIM, RET_HEAD_DIM), F32)],
        compiler_params=_cparams(("parallel", "arbitrary")),
    )(h, cos2, sin2, dmat, qd, kd, cd, gn_w, gn_b)


def _outproj_kernel(x_ref, yf_ref, yr_ref, yt_ref, w_ref, g_ref, b_ref, o_ref):
    r0, r1 = FOX_W, FOX_W + RWKV_W
    y = (_dot(yf_ref[...], w_ref[:r0, :]) + _dot(yr_ref[...], w_ref[r0:r1, :])
         + _dot(yt_ref[...], w_ref[r1:, :]))
    o_ref[...] = _layer_norm_rows(ALPHA * x_ref[...] + y, g_ref[...], b_ref[...])


def _out_proj_ln(x2d, y_fox, y_rwkv, y_ret, w_out, l, g, b):
    M, D = x2d.shape
    tm = min(512, M)
    row = lambda i: (i, 0)
    const = lambda i: (0, 0)
    return pl.pallas_call(
        _outproj_kernel,
        name="out_proj_ln",
        out_shape=jax.ShapeDtypeStruct((M, D), F32),
        grid=(M // tm,),
        in_specs=[pl.BlockSpec((tm, D), row), pl.BlockSpec((tm, FOX_W), row),
                  pl.BlockSpec((tm, RWKV_W), row), pl.BlockSpec((tm, RET_W), row),
                  pl.BlockSpec((None, D, D), lambda i: (l, 0, 0)),
                  pl.BlockSpec((1, D), const), pl.BlockSpec((1, D), const)],
        out_specs=pl.BlockSpec((tm, D), row),
        compiler_params=_cparams(("parallel",)),
    )(x2d, y_fox, y_rwkv, y_ret, w_out, g, b)


def _ffn_kernel(x_ref, wu_ref, wd_ref, g_ref, b_ref, o_ref, ob_ref, acc_ref, xb_ref):
    f = pl.program_id(1)

    @pl.when(f == 0)
    def _():
        xb_ref[...] = x_ref[...].astype(BF16)
        acc_ref[...] = jnp.zeros(acc_ref.shape, F32)

    hid = jnp.maximum(_dot(xb_ref[...], wu_ref[...]), 0.0)
    acc_ref[...] += _dot((hid * hid).astype(BF16), wd_ref[...])

    @pl.when(f == pl.num_programs(1) - 1)
    def _():
        y = _layer_norm_rows(ALPHA * x_ref[...] + acc_ref[...], g_ref[...], b_ref[...])
        o_ref[...] = y
        ob_ref[...] = y.astype(BF16)


def _ffn_ln(x2d, w_up, w_down, l, g, b):
    M, D = x2d.shape
    F = w_up.shape[2]
    tm = min(512, M)
    tf = 1024
    return pl.pallas_call(
        _ffn_kernel,
        name="ffn_ln",
        out_shape=(jax.ShapeDtypeStruct((M, D), F32), jax.ShapeDtypeStruct((M, D), BF16)),
        grid=(M // tm, F // tf),
        in_specs=[pl.BlockSpec((tm, D), lambda i, f: (i, 0)),
                  pl.BlockSpec((None, D, tf), lambda i, f: (l, 0, f)),
                  pl.BlockSpec((None, tf, D), lambda i, f: (l, f, 0)),
                  pl.BlockSpec((1, D), lambda i, f: (0, 0)),
                  pl.BlockSpec((1, D), lambda i, f: (0, 0))],
        out_specs=(pl.BlockSpec((tm, D), lambda i, f: (i, 0)),
                   pl.BlockSpec((tm, D), lambda i, f: (i, 0))),
        scratch_shapes=[pltpu.VMEM((tm, D), F32), pltpu.VMEM((tm, D), BF16)],
        compiler_params=_cparams(("parallel", "arbitrary")),
    )(x2d, w_up, w_down, g, b)


def _permute_w_in(w_in):
    L, D, _ = w_in.shape
    w_in = w_in.astype(BF16)
    w_fox = w_in[:, :, :FOX_COLS]
    w_rwkv = w_in[:, :, FOX_COLS:FOX_COLS + RWKV_COLS]
    w_ret = w_in[:, :, FOX_COLS + RWKV_COLS:]
    w_ff = w_fox[:, :, 3 * FOX_W:]
    pad = jnp.zeros((L, D, LANE - FOX_HEADS), BF16)
    w_q = w_fox[:, :, :3 * FOX_W]
    w_rw = jnp.concatenate([w_rwkv, w_ff, pad], axis=-1)
    return w_q, w_rw, w_ret


def _layer(x2d, xb, l, p, tables, B, S):
    hq = _in_proj(xb, p['w_q'], l, p['q_scale'])
    h = _in_proj(xb, p['w_rw'], l)
    ht = _in_proj(xb, p['w_ret'], l)
    c = _fox_c(h, p['fox_bias'], B, S)
    c3 = c[:, :FOX_HEADS, :].reshape(B * FOX_HEADS, 1, S)
    y_fox = _fox_attention(hq, c3, B, S)
    y_rwkv = _rwkv_mix(h, p['mu'], p['w0'], p['w_up_lora'], p['a0'], p['a_up'], p['g_up'],
                       p['k_k'], p['k_a'], p['r_k'], p['rgn_w'], p['rgn_b'], B, S)
    y_ret = _retention(ht, tables, p['tgn_w'], p['tgn_b'], B, S)
    x2d = _out_proj_ln(x2d, y_fox, y_rwkv, y_ret, p['w_out'], l, p['ln1_g'], p['ln1_b'])
    return _ffn_ln(x2d, p['w_up'], p['w_down'], l, p['ln2_g'], p['ln2_b'])


def kernel(x, w_in, fox_forget_bias, rwkv_mu, rwkv_w0, rwkv_w_up, rwkv_a0, rwkv_a_up, rwkv_g_up,
           rwkv_k_k, rwkv_k_a, rwkv_r_k, rwkv_gn_w, rwkv_gn_b, ret_gn_w, ret_gn_b, w_out, ln1_g,
           ln1_b, w_up, w_down, ln2_g, ln2_b):
    B, S, D = x.shape
    L = w_in.shape[0]
    w_q, w_rw, w_ret = _permute_w_in(w_in)
    q_scale = jnp.concatenate([jnp.full((1, FOX_W), FOX_Q_SCALE, F32),
                               jnp.ones((1, 2 * FOX_W), F32)], axis=1)
    w_out_b = w_out.astype(BF16)
    w_up_b = w_up.astype(BF16)
    w_down_b = w_down.astype(BF16)
    bias_pad = jnp.pad(fox_forget_bias, ((0, 0), (0, LANE - FOX_HEADS)))
    tables = _ret_tables(S)
    row = lambda t: t.reshape(1, -1)
    x2d = x.reshape(B * S, D)
    xb = x2d
    for l in range(L):
        p = {
            'w_rw': w_rw, 'w_ret': w_ret, 'w_q': w_q, 'q_scale': q_scale,
            'fox_bias': row(bias_pad[l]),
            'mu': row(rwkv_mu[l]), 'w0': row(rwkv_w0[l]), 'w_up_lora': rwkv_w_up[l].astype(BF16),
            'a0': row(rwkv_a0[l]), 'a_up': rwkv_a_up[l].astype(BF16),
            'g_up': rwkv_g_up[l].astype(BF16),
            'k_k': row(rwkv_k_k[l]), 'k_a': row(rwkv_k_a[l]), 'r_k': row(rwkv_r_k[l]),
            'rgn_w': row(rwkv_gn_w[l]), 'rgn_b': row(rwkv_gn_b[l]),
            'tgn_w': row(ret_gn_w[l]), 'tgn_b': row(ret_gn_b[l]),
            'w_out': w_out_b,
            'ln1_g': row(ln1_g[l]), 'ln1_b': row(ln1_b[l]),
            'w_up': w_up_b, 'w_down': w_down_b,
            'ln2_g': row(ln2_g[l]), 'ln2_b': row(ln2_b[l]),
        }
        x2d, xb = _layer(x2d, xb, l, p, tables, B, S)
    return x2d.reshape(B, S, D)
```

```python
import functools

import jax
import jax.numpy as jnp
from jax import lax
from jax.experimental import pallas as pl
from jax.experimental.pallas import tpu as pltpu

F32 = jnp.float32
BF16 = jnp.bfloat16

D_MODEL = 2048
DEPTH = 4
FOX_W = 768
RWKV_W = 640
RET_W = 640
FOX_HEAD_DIM = 128
FOX_HEADS = 6
RWKV_HEAD_DIM = 64
RWKV_HEADS = 10
RET_HEAD_DIM = 128
RET_HEADS = 5
DECAY_LORA = 64
ICLR_LORA = 64
GATE_LORA = 128
D_FF = 4 * D_MODEL
RET_CHUNK = 128
ROPE_BASE = 10000.0
LN_EPS = 1e-5
RWKV_GN_EPS = 64e-5
RET_GN_EPS = 1e-5
ALPHA = (2 * DEPTH) ** 0.25

FOX_COLS = 3 * FOX_W + FOX_HEADS
RWKV_COLS = 3 * RWKV_W + DECAY_LORA + ICLR_LORA + GATE_LORA
RET_COLS = 4 * RET_W

LANE = 128
H_FF0 = RWKV_COLS
LOG2E = 1.4426950408889634
FOX_Q_SCALE = FOX_HEAD_DIM ** -0.5 * LOG2E

FOX_BLOCKS_PER_TRIP = 4
FOX_Q_ROWS = 1024
RWKV_CHUNK = 64
RWKV_CHUNKS_PER_STEP = 4
NEG_BIG = -1e30

VMEM_LIMIT = 56 * 1024 * 1024


def _cparams(sem):
    return pltpu.CompilerParams(dimension_semantics=sem, vmem_limit_bytes=VMEM_LIMIT)


def _dot(a, b):
    return jnp.dot(a, b, preferred_element_type=F32)


def _dot_nt(a, b):
    return lax.dot_general(a, b, (((1,), (1,)), ((), ())), preferred_element_type=F32)


def _dot_tn(a, b):
    return lax.dot_general(a, b, (((0,), (0,)), ((), ())), preferred_element_type=F32)


def _split3(x):
    hi = x.astype(BF16)
    r1 = x - hi.astype(F32)
    mid = r1.astype(BF16)
    lo = (r1 - mid.astype(F32)).astype(BF16)
    return hi, mid, lo


def _split2(x):
    hi = x.astype(BF16)
    lo = (x - hi.astype(F32)).astype(BF16)
    return hi, lo


def _mask_dot(mask_bf16, x):
    hi, mid, lo = _split3(x)
    return _dot(mask_bf16, hi) + _dot(mask_bf16, mid) + _dot(mask_bf16, lo)


def _softplus(z):
    return jnp.maximum(z, 0.0) + jnp.log1p(jnp.exp(-jnp.abs(z)))


def _sigmoid(z):
    return 1.0 / (1.0 + jnp.exp(-z))


def _layer_norm_rows(y, g, b):
    mu = jnp.mean(y, axis=-1, keepdims=True)
    yc = y - mu
    var = jnp.mean(yc * yc, axis=-1, keepdims=True)
    return yc * lax.rsqrt(var + LN_EPS) * g + b


def _inproj_kernel(x_ref, w_ref, o_ref):
    o_ref[...] = _dot(x_ref[...].astype(BF16), w_ref[...])


def _inproj_scaled_kernel(x_ref, w_ref, s_ref, o_ref):
    o_ref[...] = (_dot(x_ref[...].astype(BF16), w_ref[...]) * s_ref[...]).astype(o_ref.dtype)


def _in_proj(xb, w, l, col_scale=None):
    M, K = xb.shape
    N = w.shape[2]
    tm = min(512, M)
    tn = N
    in_specs = [pl.BlockSpec((tm, K), lambda j, i: (i, 0)),
                pl.BlockSpec((None, K, tn), lambda j, i: (l, 0, j))]
    args = [xb, w]
    if col_scale is not None:
        in_specs.append(pl.BlockSpec((1, tn), lambda j, i: (0, j)))
        args.append(col_scale)
    return pl.pallas_call(
        _inproj_kernel if col_scale is None else _inproj_scaled_kernel,
        name="in_proj",
        out_shape=jax.ShapeDtypeStruct((M, N), F32 if col_scale is None else BF16),
        grid=(N // tn, M // tm),
        in_specs=in_specs,
        out_specs=pl.BlockSpec((tm, tn), lambda j, i: (i, j)),
        compiler_params=_cparams(("parallel", "parallel")),
    )(*args)


def _fox_c_kernel(f_ref, bias_ref, c_ref, *, rows):
    S = f_ref.shape[0]
    r = lax.broadcasted_iota(jnp.int32, (rows, rows), 0)
    c = lax.broadcasted_iota(jnp.int32, (rows, rows), 1)
    tri = (c <= r).astype(BF16)

    def body(i, carry):
        start = pl.multiple_of(i * rows, rows)
        z = f_ref[pl.ds(start, rows), :] + bias_ref[...]
        ls = -_softplus(-z)
        cs = _mask_dot(tri, ls) + carry
        c_ref[:, pl.ds(start, rows)] = (cs * LOG2E).T[0:8, :]
        return cs[rows - 1:rows, :]

    lax.fori_loop(0, S // rows, body, jnp.zeros((1, LANE), F32))


def _fox_c(h, bias_pad, B, S):
    rows = min(256, S)
    return pl.pallas_call(
        functools.partial(_fox_c_kernel, rows=rows),
        name="fox_c",
        out_shape=jax.ShapeDtypeStruct((B, 8, S), F32),
        grid=(B,),
        in_specs=[pl.BlockSpec((S, LANE), lambda b: (b, H_FF0 // LANE)),
                  pl.BlockSpec((1, LANE), lambda b: (0, 0))],
        out_specs=pl.BlockSpec((None, 8, S), lambda b: (b, 0, 0)),
        compiler_params=_cparams(("parallel",)),
    )(h, bias_pad)


def _fox_attn_kernel(q_ref, qn_ref, k_ref, v_ref, c_ref, o_ref, s0_ref, *, tq, tk):
    i = pl.program_id(2)
    d = q_ref.shape[1]
    q0 = pl.multiple_of(i * tq, tq)
    c0 = c_ref[:, pl.ds(q0, tq)][:, 0:1]
    q = q_ref[...]

    def scores(start):
        bias = c0 - c_ref[:, pl.ds(start, tk)]
        return _dot_nt(q, k_ref[pl.ds(start, tk), :]) + bias

    def next_step_scores():
        qn0 = pl.multiple_of(jnp.minimum(i + 1, pl.num_programs(2) - 1) * tq, tq)
        cn = c_ref[:, pl.ds(qn0, tq)][:, 0:1]
        return _dot_nt(qn_ref[...], k_ref[0:tk, :]) + (cn - c_ref[:, 0:tk])

    ones = jnp.ones((tk, d), BF16)

    def update(start, sc, carry, masked):
        m, acc = carry
        if masked:
            row = lax.broadcasted_iota(jnp.int32, (tq, tk), 0) + q0
            col = lax.broadcasted_iota(jnp.int32, (tq, tk), 1) + start
            sc = jnp.where(col <= row, sc, NEG_BIG)
        m_new = jnp.maximum(m, jnp.max(sc, axis=1, keepdims=True))
        alpha = jnp.exp2(m - m_new)
        p = jnp.exp2((sc - m_new).astype(BF16))
        vb = jnp.concatenate([v_ref[pl.ds(start, tk), :], ones], axis=1)
        acc = alpha * acc + _dot(p, vb)
        return m_new, acc

    U = FOX_BLOCKS_PER_TRIP

    D = tq // tk

    def walk(base, n, carry, last):
        s_cur = s0_ref[...]
        for u in range(n):
            start = pl.multiple_of(base + u * tk, tk)
            final = last and u == n - 1
            s_next = next_step_scores() if final else scores(pl.multiple_of(start + tk, tk))
            carry = update(start, s_cur, carry, last and u >= n - D)
            s_cur = s_next
        s0_ref[...] = s_cur
        return carry

    @pl.when(i == 0)
    def _():
        s0_ref[...] = scores(0)

    init = (jnp.full((tq, 1), NEG_BIG, F32), jnp.zeros((tq, 2 * d), F32))
    n_full = i * D
    trips = n_full // U
    carry = lax.fori_loop(
        0, trips, lambda t, c: walk(pl.multiple_of(t * (U * tk), tk), U, c, False), init)

    base = pl.multiple_of(trips * (U * tk), tk)
    rems = list(range(0, U, D))
    tails = [functools.partial(walk, base, r + D, last=True) for r in rems]
    m, acc = lax.switch((n_full - trips * U) // D, tails, carry)
    o_ref[...] = (acc[:, :d] / acc[:, d:]).astype(o_ref.dtype)


def _fox_attention(hq, c3, B, S):
    tk = min(512, S)
    tq = min(FOX_Q_ROWS, S)
    nq = S // tq
    nh = FOX_HEADS
    return pl.pallas_call(
        functools.partial(_fox_attn_kernel, tq=tq, tk=tk),
        name="fox_attn",
        scratch_shapes=[pltpu.VMEM((tq, tk), F32)],
        out_shape=jax.ShapeDtypeStruct((B * S, FOX_W), BF16),
        grid=(B, FOX_HEADS, nq),
        in_specs=[pl.BlockSpec((tq, LANE), lambda b, hd, i: (b * nq + i, hd)),
                  pl.BlockSpec((tq, LANE),
                               lambda b, hd, i: (b * nq + jnp.minimum(i + 1, nq - 1), hd)),
                  pl.BlockSpec((S, LANE), lambda b, hd, i: (b, nh + hd)),
                  pl.BlockSpec((S, LANE), lambda b, hd, i: (b, 2 * nh + hd)),
                  pl.BlockSpec((None, 1, S), lambda b, hd, i: (b * nh + hd, 0, 0))],
        out_specs=pl.BlockSpec((tq, LANE), lambda b, hd, i: (b * nq + i, hd)),
        compiler_params=_cparams(("parallel", "parallel", "arbitrary")),
    )(hq, hq, hq, hq, c3)


def _rwkv_premix(h, prev, mu, w0, wup, a0, aup, gup):
    rows = lax.broadcasted_iota(jnp.int32, h.shape, 0)
    hp = jnp.where(rows == 0, prev, pltpu.roll(h, 1, 0))
    hs = h + (hp - h) * mu
    W3 = 3 * RWKV_W
    r = hs[:, :RWKV_W]
    k = hs[:, RWKV_W:2 * RWKV_W]
    v = hs[:, 2 * RWKV_W:W3]
    wd = hs[:, W3:W3 + DECAY_LORA]
    ad = hs[:, W3 + DECAY_LORA:W3 + DECAY_LORA + ICLR_LORA]
    gd = hs[:, W3 + DECAY_LORA + ICLR_LORA:]
    z = w0 + _dot(jnp.tanh(wd).astype(BF16), wup)
    w = -_softplus(-z) - 0.5
    lw = -jnp.exp(w)
    a = _sigmoid(a0 + _dot(ad.astype(BF16), aup))
    g = _dot(_sigmoid(gd).astype(BF16), gup)
    return r, k, v, lw, a, g


def _head_sums(x, e_ref):
    hi, lo = _split2(x)
    e = e_ref[...]
    e_last = e[:LANE, :LANE]
    outs = []
    for c0 in range(0, RWKV_W, 2 * LANE):
        w = min(2 * LANE, RWKV_W - c0)
        ee = e if w == 2 * LANE else e_last
        outs.append(_dot(hi[:, c0:c0 + w], ee) + _dot(lo[:, c0:c0 + w], ee))
    return jnp.concatenate(outs, axis=1)


def _rwkv_scan_kernel(h_ref, hprev_ref, mu_ref, w0_ref, wup_ref, a0_ref, aup_ref, gup_ref,
                      kk_ref, ka_ref, rk_ref, gw_ref, gb_ref, e_ref, o_ref, st_ref, *, nchunk):
    C = RWKV_CHUNK
    N = RWKV_HEAD_DIM
    T = nchunk * C
    first = pl.program_id(1) == 0

    @pl.when(first)
    def _():
        st_ref[...] = jnp.zeros(st_ref.shape, F32)

    prev = jnp.where(first, 0.0, hprev_ref[7:8, :])
    r, k, v, lw, a, g = _rwkv_premix(h_ref[...], prev, mu_ref[...], w0_ref[...], wup_ref[...],
                                     a0_ref[...], aup_ref[...], gup_ref[...])

    ri = lax.broadcasted_iota(jnp.int32, (C, C), 0)
    ci = lax.broadcasted_iota(jnp.int32, (C, C), 1)
    incl = ci <= ri
    strict = ci < ri

    rt = lax.broadcasted_iota(jnp.int32, (T, T), 0)
    ct = lax.broadcasted_iota(jnp.int32, (T, T), 1)
    tri = ((ct <= rt) & (ct >= (rt // C) * C)).astype(BF16)
    lw_hi, lw_lo = _split2(lw)
    cum = _dot(tri, lw_hi) + _dot(tri, lw_lo)
    cls = [cum[(c + 1) * C - 1:(c + 1) * C, :] for c in range(nchunk)]
    cl_rows = jnp.concatenate([jnp.broadcast_to(t, (C, RWKV_W)) for t in cls], axis=0)
    e_in = jnp.exp(cum)
    e_ex = jnp.exp(cum - lw)
    e_neg = jnp.exp(-cum)
    e_tail = jnp.exp(cl_rows - cum)
    pcs = [jnp.exp(t) for t in cls]

    kk = k * kk_ref[...]
    kkn = kk / jnp.maximum(jnp.sqrt(_head_sums(kk * kk, e_ref)), 1e-12)
    k2 = k * (1.0 + (a - 1.0) * ka_ref[...])
    bvec = kkn * a
    At_f = -kkn * e_ex
    Rt_f = r * e_in
    AtB = At_f.astype(BF16)
    RtB = Rt_f.astype(BF16)
    BtB = (bvec * e_neg).astype(BF16)
    KtB = (k2 * e_neg).astype(BF16)
    BpB = (bvec * e_tail).astype(BF16)
    KpB = (k2 * e_tail).astype(BF16)
    vB = v.astype(BF16)

    HD = range(RWKV_HEADS)
    hsl = [slice(hd * N, (hd + 1) * N) for hd in HD]
    sls = [(slice(c * C, (c + 1) * C), hs) for c in range(nchunk) for hs in hsl]
    At = [At_f[rs, hs] for rs, hs in sls]
    Rt = [Rt_f[rs, hs] for rs, hs in sls]
    vb = [vB[rs, hs] for rs, hs in sls]
    Bt = [BtB[rs, hs] for rs, hs in sls]
    Kt = [KtB[rs, hs] for rs, hs in sls]
    Bp = [BpB[rs, hs] for rs, hs in sls]
    Kp = [KpB[rs, hs] for rs, hs in sls]

    AR = [jnp.concatenate([AtB[rs, hs], RtB[rs, hs]], axis=0) for rs, hs in sls]
    BK = [jnp.concatenate([x, y], axis=0) for x, y in zip(Bt, Kt)]
    G = [_dot_nt(x, y) for x, y in zip(AR, BK)]
    r2 = lax.broadcasted_iota(jnp.int32, (C, 2 * C), 0)
    c2 = lax.broadcasted_iota(jnp.int32, (C, 2 * C), 1)
    incl2 = jnp.where(c2 >= C, c2 - C, c2) <= r2
    P = [jnp.where(strict, t[:C, :C], 0.0) for t in G]
    Lak = [jnp.where(strict, t[:C, C:], 0.0).astype(BF16) for t in G]
    L2 = [jnp.where(incl2, t[C:], 0.0).astype(BF16) for t in G]

    X = [jnp.concatenate([x, _dot(y, z)], axis=1) for x, y, z in zip(At, Lak, vb)]
    steps = C.bit_length() - 1
    for it in range(steps):
        Pb = [t.astype(BF16) for t in P]
        Xb = [x.astype(BF16) for x in X]
        if it + 1 < steps:
            PX = [_dot(p, jnp.concatenate([x, p], axis=1)) for p, x in zip(Pb, Xb)]
            X = [x + t[:, :2 * N] for x, t in zip(X, PX)]
            P = [t[:, 2 * N:] for t in PX]
        else:
            X = [x + _dot(p, xb) for x, p, xb in zip(X, Pb, Xb)]
    Xb = [x.astype(BF16) for x in X]

    zpad = jnp.zeros((C, N), BF16)
    RHS = [jnp.concatenate([x, jnp.concatenate([zpad, z], axis=1)], axis=0)
           for x, z in zip(Xb, vb)]
    LX = [_dot(x, y) for x, y in zip(L2, RHS)]
    Qp = [(x + y[:, :N]).astype(BF16) for x, y in zip(Rt, LX)]
    Y0 = [y[:, N:] for y in LX]

    XB = [_dot_tn(x, b) for x, b in zip(Xb, Bp)]
    Mw = [t[:N].astype(BF16) for t in XB]
    Nn = [t[N:] + _dot_tn(x, y) for t, x, y in zip(XB, vb, Kp)]

    S = [st_ref[hd] for hd in HD]
    y_rows = []
    for c in range(nchunk):
        ch = [c * RWKV_HEADS + hd for hd in HD]
        Ssp = [_split2(s) for s in S]
        ys = [_dot_nt(Qp[j], sp[0]) + Y0[j] for j, sp in zip(ch, Ssp)]
        y_rows.append(jnp.concatenate(ys, axis=1))
        SM = [_dot(jnp.concatenate(sp, axis=0), Mw[j]) for sp, j in zip(Ssp, ch)]
        S = [s * pcs[c][:, hs] + sm[:N] + sm[N:] + Nn[j]
             for s, sm, j, hs in zip(S, SM, ch, hsl)]
    for hd in HD:
        st_ref[hd] = S[hd]

    y = jnp.concatenate(y_rows, axis=0)
    inv_n = 1.0 / N
    yc = y - _head_sums(y, e_ref) * inv_n
    var = _head_sums(yc * yc, e_ref) * inv_n
    yn = yc * lax.rsqrt(var + RWKV_GN_EPS) * gw_ref[...] + gb_ref[...]
    bonus = _head_sums(r * k2 * rk_ref[...], e_ref) * v
    o_ref[...] = ((yn + bonus) * g).astype(o_ref.dtype)


def _rwkv_mix(h, mu, w0, wup, a0, aup, gup, k_k, k_a, r_k, gn_w, gn_b, B, S):
    nchunk = RWKV_CHUNKS_PER_STEP
    T = nchunk * RWKV_CHUNK
    nc = S // T
    row = lambda b, c: (b * nc + c, 0)
    prev_row = lambda b, c: (jnp.maximum((b * nc + c) * (T // 8) - 1, 0), 0)
    const = lambda b, c: (0, 0)
    vec = pl.BlockSpec((1, RWKV_W), const)
    hid = jnp.arange(2 * LANE, dtype=jnp.int32) // RWKV_HEAD_DIM
    head_ones = (hid[:, None] == hid[None, :]).astype(BF16)
    return pl.pallas_call(
        functools.partial(_rwkv_scan_kernel, nchunk=nchunk),
        name="rwkv_scan",
        out_shape=jax.ShapeDtypeStruct((B * S, RWKV_W), BF16),
        grid=(B, nc),
        in_specs=[pl.BlockSpec((T, RWKV_COLS), row),
                  pl.BlockSpec((8, RWKV_COLS), prev_row),
                  pl.BlockSpec((1, RWKV_COLS), const),
                  vec,
                  pl.BlockSpec((DECAY_LORA, RWKV_W), const),
                  vec,
                  pl.BlockSpec((ICLR_LORA, RWKV_W), const),
                  pl.BlockSpec((GATE_LORA, RWKV_W), const),
                  vec, vec, vec, vec, vec,
                  pl.BlockSpec((2 * LANE, 2 * LANE), const)],
        out_specs=pl.BlockSpec((T, RWKV_W), row),
        scratch_shapes=[pltpu.VMEM((RWKV_HEADS, RWKV_HEAD_DIM, RWKV_HEAD_DIM), F32)],
        compiler_params=_cparams(("parallel", "arbitrary")),
    )(h, h, mu, w0, wup, a0, aup, gup, k_k, k_a, r_k, gn_w, gn_b, head_ones)


def _ret_kernel(h_ref, cos_ref, sin_ref, dmat_ref, qd_ref, kd_ref, cd_ref, gw_ref, gb_ref,
                o_ref, R_ref, *, nchunk):
    C = RET_CHUNK
    d = RET_HEAD_DIM
    H = RET_HEADS
    W = RET_W

    @pl.when(pl.program_id(1) == 0)
    def _():
        R_ref[...] = jnp.zeros(R_ref.shape, F32)

    pairs = [(c, hd) for c in range(nchunk) for hd in range(H)]
    rows = lambda c: slice(c * C, (c + 1) * C)
    cols = lambda part, hd: slice(part * W + hd * d, part * W + (hd + 1) * d)
    cos = [cos_ref[rows(c), :] for c in range(nchunk)]
    sin = [sin_ref[rows(c), :] for c in range(nchunk)]
    q = [h_ref[rows(c), cols(0, hd)] for c, hd in pairs]
    k = [h_ref[rows(c), cols(1, hd)] for c, hd in pairs]
    qr = [x * cos[c] + pltpu.roll(x, d // 2, 1) * sin[c] for x, (c, hd) in zip(q, pairs)]
    kr = [(x * cos[c] + pltpu.roll(x, d // 2, 1) * sin[c]) * (d ** -0.5)
          for x, (c, hd) in zip(k, pairs)]
    vb = [h_ref[rows(c), cols(2, hd)].astype(BF16) for c, hd in pairs]
    qb = [x.astype(BF16) for x in qr]
    kb = [x.astype(BF16) for x in kr]
    kdb = [(x * kd_ref[hd]).astype(BF16) for x, (c, hd) in zip(kr, pairs)]
    inner = [(_dot_nt(x, y) * dmat_ref[hd]).astype(BF16) for x, y, (c, hd) in zip(qb, kb, pairs)]
    o_in = [_dot(x, y) for x, y in zip(inner, vb)]
    kv = [_dot_tn(x, y) for x, y in zip(kdb, vb)]

    R = [R_ref[hd] for hd in range(H)]
    outs = []
    for c in range(nchunk):
        ch = [c * H + hd for hd in range(H)]
        Rsp = [_split2(x) for x in R]
        outs += [o_in[j] + (_dot(qb[j], sp[0]) + _dot(qb[j], sp[1])) * qd_ref[hd]
                 for hd, (j, sp) in enumerate(zip(ch, Rsp))]
        R = [cd_ref[hd][0:1, :] * x + kv[j] for hd, (x, j) in enumerate(zip(R, ch))]
    for hd in range(H):
        R_ref[hd] = R[hd]

    for o, (c, hd) in zip(outs, pairs):
        mu = jnp.mean(o, axis=1, keepdims=True)
        oc = o - mu
        var = jnp.mean(oc * oc, axis=1, keepdims=True)
        gsl = slice(hd * d, (hd + 1) * d)
        on = oc * lax.rsqrt(var + RET_GN_EPS) * gw_ref[:, gsl] + gb_ref[:, gsl]
        gate = h_ref[rows(c), cols(3, hd)]
        o_ref[rows(c), gsl] = (gate * _sigmoid(gate) * on).astype(o_ref.dtype)


def _ret_tables(S):
    C, d, H = RET_CHUNK, RET_HEAD_DIM, RET_HEADS
    half = d // 2
    inv = 1.0 / (ROPE_BASE ** (jnp.arange(half, dtype=F32) / half))
    ang = jnp.arange(S, dtype=F32)[:, None] * inv[None, :]
    cos = jnp.cos(ang)
    sin = jnp.sin(ang)
    cos2 = jnp.concatenate([cos, cos], axis=1)
    sin2 = jnp.concatenate([-sin, sin], axis=1)
    log_g = jnp.log(1.0 - 2.0 ** (-5.0 - jnp.arange(H, dtype=F32)))
    pos = jnp.arange(C, dtype=F32)
    rel = pos[:, None] - pos[None, :]
    dmat = jnp.where(rel >= 0, jnp.exp(log_g[:, None, None] * jnp.maximum(rel, 0.0)), 0.0)
    qd = jnp.broadcast_to(jnp.exp(log_g[:, None] * (pos + 1.0))[..., None], (H, C, d))
    kd = jnp.broadcast_to(jnp.exp(log_g[:, None] * (C - 1.0 - pos))[..., None], (H, C, d))
    cd = jnp.broadcast_to(jnp.exp(log_g * C)[:, None, None], (H, 8, d))
    return cos2, sin2, dmat, qd, kd, cd


def _retention(h, tables, gn_w, gn_b, B, S):
    cos2, sin2, dmat, qd, kd, cd = tables
    C = RET_CHUNK
    T = min(512, S)
    nt = S // T
    H = RET_HEADS
    tab = pl.BlockSpec((T, LANE), lambda b, t: (t, 0))
    per_head = lambda rows: pl.BlockSpec((H, rows, LANE), lambda b, t: (0, 0, 0))
    vec = pl.BlockSpec((1, RET_W), lambda b, t: (0, 0))
    return pl.pallas_call(
        functools.partial(_ret_kernel, nchunk=T // C),
        name="retention",
        out_shape=jax.ShapeDtypeStruct((B * S, RET_W), BF16),
        grid=(B, nt),
        in_specs=[pl.BlockSpec((T, RET_COLS), lambda b, t: (b * nt + t, 0)),
                  tab, tab, per_head(C), per_head(C), per_head(C), per_head(8), vec, vec],
        out_specs=pl.BlockSpec((T, RET_W), lambda b, t: (b * nt + t, 0)),
        scratch_shapes=[pltpu.VMEM((H, RET_HEAD_DIM, RET_HEAD_DIM), F32)],
        compiler_params=_cparams(("parallel", "arbitrary")),
    )(h, cos2, sin2, dmat, qd, kd, cd, gn_w, gn_b)


def _outproj_kernel(x_ref, yf_ref, yr_ref, yt_ref, w_ref, g_ref, b_ref, o_ref):
    r0, r1 = FOX_W, FOX_W + RWKV_W
    y = (_dot(yf_ref[...], w_ref[:r0, :]) + _dot(yr_ref[...], w_ref[r0:r1, :])
         + _dot(yt_ref[...], w_ref[r1:, :]))
    o_ref[...] = _layer_norm_rows(ALPHA * x_ref[...] + y, g_ref[...], b_ref[...])


def _out_proj_ln(x2d, y_fox, y_rwkv, y_ret, w_out, l, g, b):
    M, D = x2d.shape
    tm = min(512, M)
    row = lambda i: (i, 0)
    const = lambda i: (0, 0)
    return pl.pallas_call(
        _outproj_kernel,
        name="out_proj_ln",
        out_shape=jax.ShapeDtypeStruct((M, D), F32),
        grid=(M // tm,),
        in_specs=[pl.BlockSpec((tm, D), row), pl.BlockSpec((tm, FOX_W), row),
                  pl.BlockSpec((tm, RWKV_W), row), pl.BlockSpec((tm, RET_W), row),
                  pl.BlockSpec((None, D, D), lambda i: (l, 0, 0)),
                  pl.BlockSpec((1, D), const), pl.BlockSpec((1, D), const)],
        out_specs=pl.BlockSpec((tm, D), row),
        compiler_params=_cparams(("parallel",)),
    )(x2d, y_fox, y_rwkv, y_ret, w_out, g, b)


def _ffn_kernel(x_ref, wu_ref, wd_ref, g_ref, b_ref, o_ref, ob_ref, acc_ref, xb_ref):
    f = pl.program_id(1)

    @pl.when(f == 0)
    def _():
        x = x_ref[...]
        xb_ref[...] = x.astype(BF16)
        acc_ref[...] = ALPHA * x

    hid = jnp.maximum(_dot(xb_ref[...], wu_ref[...]), 0.0)
    acc_ref[...] += _dot((hid * hid).astype(BF16), wd_ref[...])

    @pl.when(f == pl.num_programs(1) - 1)
    def _():
        y = _layer_norm_rows(acc_ref[...], g_ref[...], b_ref[...])
        o_ref[...] = y
        ob_ref[...] = y.astype(BF16)


def _ffn_ln(x2d, w_up, w_down, l, g, b):
    M, D = x2d.shape
    F = w_up.shape[2]
    tm = min(512, M)
    tf = 1024
    return pl.pallas_call(
        _ffn_kernel,
        name="ffn_ln",
        out_shape=(jax.ShapeDtypeStruct((M, D), F32), jax.ShapeDtypeStruct((M, D), BF16)),
        grid=(M // tm, F // tf),
        in_specs=[pl.BlockSpec((tm, D), lambda i, f: (i, 0)),
                  pl.BlockSpec((None, D, tf), lambda i, f: (l, 0, f)),
                  pl.BlockSpec((None, tf, D), lambda i, f: (l, f, 0)),
                  pl.BlockSpec((1, D), lambda i, f: (0, 0)),
                  pl.BlockSpec((1, D), lambda i, f: (0, 0))],
        out_specs=(pl.BlockSpec((tm, D), lambda i, f: (i, 0)),
                   pl.BlockSpec((tm, D), lambda i, f: (i, 0))),
        scratch_shapes=[pltpu.VMEM((tm, D), F32), pltpu.VMEM((tm, D), BF16)],
        compiler_params=_cparams(("parallel", "arbitrary")),
    )(x2d, w_up, w_down, g, b)


def _permute_w_in(w_in):
    L, D, _ = w_in.shape
    w_in = w_in.astype(BF16)
    w_fox = w_in[:, :, :FOX_COLS]
    w_rwkv = w_in[:, :, FOX_COLS:FOX_COLS + RWKV_COLS]
    w_ret = w_in[:, :, FOX_COLS + RWKV_COLS:]
    w_ff = w_fox[:, :, 3 * FOX_W:]
    pad = jnp.zeros((L, D, LANE - FOX_HEADS), BF16)
    w_q = w_fox[:, :, :3 * FOX_W]
    w_rw = jnp.concatenate([w_rwkv, w_ff, pad], axis=-1)
    return w_q, w_rw, w_ret


def _layer(x2d, xb, l, p, tables, B, S):
    hq = _in_proj(xb, p['w_q'], l, p['q_scale'])
    h = _in_proj(xb, p['w_rw'], l)
    ht = _in_proj(xb, p['w_ret'], l)
    c = _fox_c(h, p['fox_bias'], B, S)
    c3 = c[:, :FOX_HEADS, :].reshape(B * FOX_HEADS, 1, S)
    y_fox = _fox_attention(hq, c3, B, S)
    y_rwkv = _rwkv_mix(h, p['mu'], p['w0'], p['w_up_lora'], p['a0'], p['a_up'], p['g_up'],
                       p['k_k'], p['k_a'], p['r_k'], p['rgn_w'], p['rgn_b'], B, S)
    y_ret = _retention(ht, tables, p['tgn_w'], p['tgn_b'], B, S)
    x2d = _out_proj_ln(x2d, y_fox, y_rwkv, y_ret, p['w_out'], l, p['ln1_g'], p['ln1_b'])
    return _ffn_ln(x2d, p['w_up'], p['w_down'], l, p['ln2_g'], p['ln2_b'])


def kernel(x, w_in, fox_forget_bias, rwkv_mu, rwkv_w0, rwkv_w_up, rwkv_a0, rwkv_a_up, rwkv_g_up,
           rwkv_k_k, rwkv_k_a, rwkv_r_k, rwkv_gn_w, rwkv_gn_b, ret_gn_w, ret_gn_b, w_out, ln1_g,
           ln1_b, w_up, w_down, ln2_g, ln2_b):
    B, S, D = x.shape
    L = w_in.shape[0]
    w_q, w_rw, w_ret = _permute_w_in(w_in)
    q_scale = jnp.concatenate([jnp.full((1, FOX_W), FOX_Q_SCALE, F32),
                               jnp.ones((1, 2 * FOX_W), F32)], axis=1)
    w_out_b = w_out.astype(BF16)
    w_up_b = w_up.astype(BF16)
    w_down_b = w_down.astype(BF16)
    bias_pad = jnp.pad(fox_forget_bias, ((0, 0), (0, LANE - FOX_HEADS)))
    tables = _ret_tables(S)
    row = lambda t: t.reshape(1, -1)
    x2d = x.reshape(B * S, D)
    xb = x2d
    for l in range(L):
        p = {
            'w_rw': w_rw, 'w_ret': w_ret, 'w_q': w_q, 'q_scale': q_scale,
            'fox_bias': row(bias_pad[l]),
            'mu': row(rwkv_mu[l]), 'w0': row(rwkv_w0[l]), 'w_up_lora': rwkv_w_up[l].astype(BF16),
            'a0': row(rwkv_a0[l]), 'a_up': rwkv_a_up[l].astype(BF16),
            'g_up': rwkv_g_up[l].astype(BF16),
            'k_k': row(rwkv_k_k[l]), 'k_a': row(rwkv_k_a[l]), 'r_k': row(rwkv_r_k[l]),
            'rgn_w': row(rwkv_gn_w[l]), 'rgn_b': row(rwkv_gn_b[l]),
            'tgn_w': row(ret_gn_w[l]), 'tgn_b': row(ret_gn_b[l]),
            'w_out': w_out_b,
            'ln1_g': row(ln1_g[l]), 'ln1_b': row(ln1_b[l]),
            'w_up': w_up_b, 'w_down': w_down_b,
            'ln2_g': row(ln2_g[l]), 'ln2_b': row(ln2_b[l]),
        }
        x2d, xb = _layer(x2d, xb, l, p, tables, B, S)
    return x2d.reshape(B, S, D)
```

```python
import functools

import jax
import jax.numpy as jnp
from jax import lax
from jax.experimental import pallas as pl
from jax.experimental.pallas import tpu as pltpu

F32 = jnp.float32
BF16 = jnp.bfloat16

D_MODEL = 2048
DEPTH = 4
FOX_W = 768
RWKV_W = 640
RET_W = 640
FOX_HEAD_DIM = 128
FOX_HEADS = 6
RWKV_HEAD_DIM = 64
RWKV_HEADS = 10
RET_HEAD_DIM = 128
RET_HEADS = 5
DECAY_LORA = 64
ICLR_LORA = 64
GATE_LORA = 128
D_FF = 4 * D_MODEL
RET_CHUNK = 128
ROPE_BASE = 10000.0
LN_EPS = 1e-5
RWKV_GN_EPS = 64e-5
RET_GN_EPS = 1e-5
ALPHA = (2 * DEPTH) ** 0.25

FOX_COLS = 3 * FOX_W + FOX_HEADS
RWKV_COLS = 3 * RWKV_W + DECAY_LORA + ICLR_LORA + GATE_LORA
RET_COLS = 4 * RET_W

LANE = 128
H_FF0 = RWKV_COLS
LOG2E = 1.4426950408889634
FOX_Q_SCALE = FOX_HEAD_DIM ** -0.5 * LOG2E

FOX_BLOCKS_PER_TRIP = 4
FOX_Q_ROWS = 1024
RWKV_CHUNK = 64
RWKV_CHUNKS_PER_STEP = 4
NEG_BIG = -1e30

VMEM_LIMIT = 56 * 1024 * 1024


def _cparams(sem):
    return pltpu.CompilerParams(dimension_semantics=sem, vmem_limit_bytes=VMEM_LIMIT)


def _dot(a, b):
    return jnp.dot(a, b, preferred_element_type=F32)


def _dot_nt(a, b):
    return lax.dot_general(a, b, (((1,), (1,)), ((), ())), preferred_element_type=F32)


def _dot_tn(a, b):
    return lax.dot_general(a, b, (((0,), (0,)), ((), ())), preferred_element_type=F32)


def _split3(x):
    hi = x.astype(BF16)
    r1 = x - hi.astype(F32)
    mid = r1.astype(BF16)
    lo = (r1 - mid.astype(F32)).astype(BF16)
    return hi, mid, lo


def _split2(x):
    hi = x.astype(BF16)
    lo = (x - hi.astype(F32)).astype(BF16)
    return hi, lo


def _mask_dot(mask_bf16, x):
    hi, mid, lo = _split3(x)
    return _dot(mask_bf16, hi) + _dot(mask_bf16, mid) + _dot(mask_bf16, lo)


def _softplus(z):
    return jnp.maximum(z, 0.0) + jnp.log1p(jnp.exp(-jnp.abs(z)))


def _sigmoid(z):
    return 1.0 / (1.0 + jnp.exp(-z))


def _layer_norm_rows(y, g, b):
    mu = jnp.mean(y, axis=-1, keepdims=True)
    yc = y - mu
    var = jnp.mean(yc * yc, axis=-1, keepdims=True)
    return yc * lax.rsqrt(var + LN_EPS) * g + b


def _inproj_kernel(x_ref, w_ref, o_ref):
    o_ref[...] = _dot(x_ref[...].astype(BF16), w_ref[...])


def _inproj_scaled_kernel(x_ref, w_ref, s_ref, o_ref):
    o_ref[...] = (_dot(x_ref[...].astype(BF16), w_ref[...]) * s_ref[...]).astype(o_ref.dtype)


def _in_proj(xb, w, l, col_scale=None):
    M, K = xb.shape
    N = w.shape[2]
    tm = min(1024 if xb.dtype == BF16 else 512, M)
    tn = N
    in_specs = [pl.BlockSpec((tm, K), lambda j, i: (i, 0)),
                pl.BlockSpec((None, K, tn), lambda j, i: (l, 0, j))]
    args = [xb, w]
    if col_scale is not None:
        in_specs.append(pl.BlockSpec((1, tn), lambda j, i: (0, j)))
        args.append(col_scale)
    return pl.pallas_call(
        _inproj_kernel if col_scale is None else _inproj_scaled_kernel,
        name="in_proj",
        out_shape=jax.ShapeDtypeStruct((M, N), F32 if col_scale is None else BF16),
        grid=(N // tn, M // tm),
        in_specs=in_specs,
        out_specs=pl.BlockSpec((tm, tn), lambda j, i: (i, j)),
        compiler_params=_cparams(("parallel", "parallel")),
    )(*args)


def _fox_c_kernel(f_ref, bias_ref, c_ref, *, rows):
    S = f_ref.shape[0]
    r = lax.broadcasted_iota(jnp.int32, (rows, rows), 0)
    c = lax.broadcasted_iota(jnp.int32, (rows, rows), 1)
    tri = (c <= r).astype(BF16)

    def body(i, carry):
        start = pl.multiple_of(i * rows, rows)
        z = f_ref[pl.ds(start, rows), :] + bias_ref[...]
        ls = -_softplus(-z)
        cs = _mask_dot(tri, ls) + carry
        c_ref[:, pl.ds(start, rows)] = (cs * LOG2E).T[0:8, :]
        return cs[rows - 1:rows, :]

    lax.fori_loop(0, S // rows, body, jnp.zeros((1, LANE), F32))


def _fox_c(h, bias_pad, B, S):
    rows = min(256, S)
    return pl.pallas_call(
        functools.partial(_fox_c_kernel, rows=rows),
        name="fox_c",
        out_shape=jax.ShapeDtypeStruct((B, 8, S), F32),
        grid=(B,),
        in_specs=[pl.BlockSpec((S, LANE), lambda b: (b, H_FF0 // LANE)),
                  pl.BlockSpec((1, LANE), lambda b: (0, 0))],
        out_specs=pl.BlockSpec((None, 8, S), lambda b: (b, 0, 0)),
        compiler_params=_cparams(("parallel",)),
    )(h, bias_pad)


def _fox_attn_kernel(q_ref, qn_ref, k_ref, v_ref, c_ref, o_ref, s0_ref, *, tq, tk):
    i = pl.program_id(2)
    d = q_ref.shape[1]
    q0 = pl.multiple_of(i * tq, tq)
    c0 = c_ref[:, pl.ds(q0, tq)][:, 0:1]
    q = q_ref[...]

    def scores(start):
        bias = c0 - c_ref[:, pl.ds(start, tk)]
        return _dot_nt(q, k_ref[pl.ds(start, tk), :]) + bias

    def next_step_scores():
        qn0 = pl.multiple_of(jnp.minimum(i + 1, pl.num_programs(2) - 1) * tq, tq)
        cn = c_ref[:, pl.ds(qn0, tq)][:, 0:1]
        return _dot_nt(qn_ref[...], k_ref[0:tk, :]) + (cn - c_ref[:, 0:tk])

    ones = jnp.ones((tk, d), BF16)

    def update(start, sc, carry, masked):
        m, acc = carry
        if masked:
            row = lax.broadcasted_iota(jnp.int32, (tq, tk), 0) + q0
            col = lax.broadcasted_iota(jnp.int32, (tq, tk), 1) + start
            sc = jnp.where(col <= row, sc, NEG_BIG)
        m_new = jnp.maximum(m, jnp.max(sc, axis=1, keepdims=True))
        alpha = jnp.exp2(m - m_new)
        p = jnp.exp2((sc - m_new).astype(BF16))
        vb = jnp.concatenate([v_ref[pl.ds(start, tk), :], ones], axis=1)
        acc = alpha * acc + _dot(p, vb)
        return m_new, acc

    U = FOX_BLOCKS_PER_TRIP

    D = tq // tk

    def walk(base, n, carry, last):
        s_cur = s0_ref[...]
        for u in range(n):
            start = pl.multiple_of(base + u * tk, tk)
            final = last and u == n - 1
            s_next = next_step_scores() if final else scores(pl.multiple_of(start + tk, tk))
            carry = update(start, s_cur, carry, last and u >= n - D)
            s_cur = s_next
        s0_ref[...] = s_cur
        return carry

    @pl.when(i == 0)
    def _():
        s0_ref[...] = scores(0)

    init = (jnp.full((tq, 1), NEG_BIG, F32), jnp.zeros((tq, 2 * d), F32))
    n_full = i * D
    trips = n_full // U
    carry = lax.fori_loop(
        0, trips, lambda t, c: walk(pl.multiple_of(t * (U * tk), tk), U, c, False), init)

    base = pl.multiple_of(trips * (U * tk), tk)
    rems = list(range(0, U, D))
    tails = [functools.partial(walk, base, r + D, last=True) for r in rems]
    m, acc = lax.switch((n_full - trips * U) // D, tails, carry)
    o_ref[...] = (acc[:, :d] / acc[:, d:]).astype(o_ref.dtype)


def _fox_attention(hq, c3, B, S):
    tk = min(512, S)
    tq = min(FOX_Q_ROWS, S)
    nq = S // tq
    nh = FOX_HEADS
    return pl.pallas_call(
        functools.partial(_fox_attn_kernel, tq=tq, tk=tk),
        name="fox_attn",
        scratch_shapes=[pltpu.VMEM((tq, tk), F32)],
        out_shape=jax.ShapeDtypeStruct((B * S, FOX_W), BF16),
        grid=(B, FOX_HEADS, nq),
        in_specs=[pl.BlockSpec((tq, LANE), lambda b, hd, i: (b * nq + i, hd)),
                  pl.BlockSpec((tq, LANE),
                               lambda b, hd, i: (b * nq + jnp.minimum(i + 1, nq - 1), hd)),
                  pl.BlockSpec((S, LANE), lambda b, hd, i: (b, nh + hd)),
                  pl.BlockSpec((S, LANE), lambda b, hd, i: (b, 2 * nh + hd)),
                  pl.BlockSpec((None, 1, S), lambda b, hd, i: (b * nh + hd, 0, 0))],
        out_specs=pl.BlockSpec((tq, LANE), lambda b, hd, i: (b * nq + i, hd)),
        compiler_params=_cparams(("parallel", "parallel", "arbitrary")),
    )(hq, hq, hq, hq, c3)


def _rwkv_premix(h, prev, mu, w0, wup, a0, aup, gup):
    rows = lax.broadcasted_iota(jnp.int32, h.shape, 0)
    hp = jnp.where(rows == 0, prev, pltpu.roll(h, 1, 0))
    hs = h + (hp - h) * mu
    W3 = 3 * RWKV_W
    r = hs[:, :RWKV_W]
    k = hs[:, RWKV_W:2 * RWKV_W]
    v = hs[:, 2 * RWKV_W:W3]
    wd = hs[:, W3:W3 + DECAY_LORA]
    ad = hs[:, W3 + DECAY_LORA:W3 + DECAY_LORA + ICLR_LORA]
    gd = hs[:, W3 + DECAY_LORA + ICLR_LORA:]
    z = w0 + _dot(jnp.tanh(wd).astype(BF16), wup)
    w = -_softplus(-z) - 0.5
    lw = -jnp.exp(w)
    a = _sigmoid(a0 + _dot(ad.astype(BF16), aup))
    g = _dot(_sigmoid(gd).astype(BF16), gup)
    return r, k, v, lw, a, g


def _head_sums(x, e_ref):
    hi, lo = _split2(x)
    e = e_ref[...]
    e_last = e[:LANE, :LANE]
    outs = []
    for c0 in range(0, RWKV_W, 2 * LANE):
        w = min(2 * LANE, RWKV_W - c0)
        ee = e if w == 2 * LANE else e_last
        outs.append(_dot(hi[:, c0:c0 + w], ee) + _dot(lo[:, c0:c0 + w], ee))
    return jnp.concatenate(outs, axis=1)


def _rwkv_scan_kernel(h_ref, hprev_ref, mu_ref, w0_ref, wup_ref, a0_ref, aup_ref, gup_ref,
                      kk_ref, ka_ref, rk_ref, gw_ref, gb_ref, e_ref, o_ref, st_ref, *, nchunk):
    C = RWKV_CHUNK
    N = RWKV_HEAD_DIM
    T = nchunk * C
    first = pl.program_id(1) == 0

    @pl.when(first)
    def _():
        st_ref[...] = jnp.zeros(st_ref.shape, F32)

    prev = jnp.where(first, 0.0, hprev_ref[7:8, :])
    r, k, v, lw, a, g = _rwkv_premix(h_ref[...], prev, mu_ref[...], w0_ref[...], wup_ref[...],
                                     a0_ref[...], aup_ref[...], gup_ref[...])

    ri = lax.broadcasted_iota(jnp.int32, (C, C), 0)
    ci = lax.broadcasted_iota(jnp.int32, (C, C), 1)
    incl = ci <= ri
    strict = ci < ri

    rt = lax.broadcasted_iota(jnp.int32, (T, T), 0)
    ct = lax.broadcasted_iota(jnp.int32, (T, T), 1)
    tri = ((ct <= rt) & (ct >= (rt // C) * C)).astype(BF16)
    lw_hi, lw_lo = _split2(lw)
    cum = _dot(tri, lw_hi) + _dot(tri, lw_lo)
    cls = [cum[(c + 1) * C - 1:(c + 1) * C, :] for c in range(nchunk)]
    cl_rows = jnp.concatenate([jnp.broadcast_to(t, (C, RWKV_W)) for t in cls], axis=0)
    e_in = jnp.exp(cum)
    e_ex = jnp.exp(cum - lw)
    e_neg = jnp.exp(-cum)
    e_tail = jnp.exp(cl_rows - cum)
    pcs = [jnp.exp(t) for t in cls]

    kk = k * kk_ref[...]
    kkn = kk / jnp.maximum(jnp.sqrt(_head_sums(kk * kk, e_ref)), 1e-12)
    k2 = k * (1.0 + (a - 1.0) * ka_ref[...])
    bvec = kkn * a
    At_f = -kkn * e_ex
    Rt_f = r * e_in
    AtB = At_f.astype(BF16)
    RtB = Rt_f.astype(BF16)
    BtB = (bvec * e_neg).astype(BF16)
    KtB = (k2 * e_neg).astype(BF16)
    BpB = (bvec * e_tail).astype(BF16)
    KpB = (k2 * e_tail).astype(BF16)
    vB = v.astype(BF16)

    HD = range(RWKV_HEADS)
    hsl = [slice(hd * N, (hd + 1) * N) for hd in HD]
    sls = [(slice(c * C, (c + 1) * C), hs) for c in range(nchunk) for hs in hsl]
    At = [At_f[rs, hs] for rs, hs in sls]
    Rt = [Rt_f[rs, hs] for rs, hs in sls]
    vb = [vB[rs, hs] for rs, hs in sls]
    Bt = [BtB[rs, hs] for rs, hs in sls]
    Kt = [KtB[rs, hs] for rs, hs in sls]
    Bp = [BpB[rs, hs] for rs, hs in sls]
    Kp = [KpB[rs, hs] for rs, hs in sls]

    AR = [jnp.concatenate([AtB[rs, hs], RtB[rs, hs]], axis=0) for rs, hs in sls]
    BK = [jnp.concatenate([x, y], axis=0) for x, y in zip(Bt, Kt)]
    G = [_dot_nt(x, y) for x, y in zip(AR, BK)]
    r2 = lax.broadcasted_iota(jnp.int32, (C, 2 * C), 0)
    c2 = lax.broadcasted_iota(jnp.int32, (C, 2 * C), 1)
    incl2 = jnp.where(c2 >= C, c2 - C, c2) <= r2
    P = [jnp.where(strict, t[:C, :C], 0.0) for t in G]
    Lak = [jnp.where(strict, t[:C, C:], 0.0).astype(BF16) for t in G]
    L2 = [jnp.where(incl2, t[C:], 0.0).astype(BF16) for t in G]

    X = [jnp.concatenate([x, _dot(y, z)], axis=1) for x, y, z in zip(At, Lak, vb)]
    steps = C.bit_length() - 1
    for it in range(steps):
        Pb = [t.astype(BF16) for t in P]
        Xb = [x.astype(BF16) for x in X]
        if it + 1 < steps:
            PX = [_dot(p, jnp.concatenate([x, p], axis=1)) for p, x in zip(Pb, Xb)]
            X = [x + t[:, :2 * N] for x, t in zip(X, PX)]
            P = [t[:, 2 * N:] for t in PX]
        else:
            X = [x + _dot(p, xb) for x, p, xb in zip(X, Pb, Xb)]
    Xb = [x.astype(BF16) for x in X]

    zpad = jnp.zeros((C, N), BF16)
    RHS = [jnp.concatenate([x, jnp.concatenate([zpad, z], axis=1)], axis=0)
           for x, z in zip(Xb, vb)]
    LX = [_dot(x, y) for x, y in zip(L2, RHS)]
    Qp = [(x + y[:, :N]).astype(BF16) for x, y in zip(Rt, LX)]
    Y0 = [y[:, N:] for y in LX]

    XB = [_dot_tn(x, b) for x, b in zip(Xb, Bp)]
    Mw = [t[:N].astype(BF16) for t in XB]
    Nn = [t[N:] + _dot_tn(x, y) for t, x, y in zip(XB, vb, Kp)]

    S = [st_ref[hd] for hd in HD]
    y_rows = []
    for c in range(nchunk):
        ch = [c * RWKV_HEADS + hd for hd in HD]
        Ssp = [_split2(s) for s in S]
        ys = [_dot_nt(Qp[j], sp[0]) + Y0[j] for j, sp in zip(ch, Ssp)]
        y_rows.append(jnp.concatenate(ys, axis=1))
        SM = [_dot(jnp.concatenate(sp, axis=0), Mw[j]) for sp, j in zip(Ssp, ch)]
        S = [s * pcs[c][:, hs] + sm[:N] + sm[N:] + Nn[j]
             for s, sm, j, hs in zip(S, SM, ch, hsl)]
    for hd in HD:
        st_ref[hd] = S[hd]

    y = jnp.concatenate(y_rows, axis=0)
    inv_n = 1.0 / N
    yc = y - _head_sums(y, e_ref) * inv_n
    var = _head_sums(yc * yc, e_ref) * inv_n
    yn = yc * lax.rsqrt(var + RWKV_GN_EPS) * gw_ref[...] + gb_ref[...]
    bonus = _head_sums(r * k2 * rk_ref[...], e_ref) * v
    o_ref[...] = ((yn + bonus) * g).astype(o_ref.dtype)


def _rwkv_mix(h, mu, w0, wup, a0, aup, gup, k_k, k_a, r_k, gn_w, gn_b, B, S):
    nchunk = RWKV_CHUNKS_PER_STEP
    T = nchunk * RWKV_CHUNK
    nc = S // T
    row = lambda b, c: (b * nc + c, 0)
    prev_row = lambda b, c: (jnp.maximum((b * nc + c) * (T // 8) - 1, 0), 0)
    const = lambda b, c: (0, 0)
    vec = pl.BlockSpec((1, RWKV_W), const)
    hid = jnp.arange(2 * LANE, dtype=jnp.int32) // RWKV_HEAD_DIM
    head_ones = (hid[:, None] == hid[None, :]).astype(BF16)
    return pl.pallas_call(
        functools.partial(_rwkv_scan_kernel, nchunk=nchunk),
        name="rwkv_scan",
        out_shape=jax.ShapeDtypeStruct((B * S, RWKV_W), BF16),
        grid=(B, nc),
        in_specs=[pl.BlockSpec((T, RWKV_COLS), row),
                  pl.BlockSpec((8, RWKV_COLS), prev_row),
                  pl.BlockSpec((1, RWKV_COLS), const),
                  vec,
                  pl.BlockSpec((DECAY_LORA, RWKV_W), const),
                  vec,
                  pl.BlockSpec((ICLR_LORA, RWKV_W), const),
                  pl.BlockSpec((GATE_LORA, RWKV_W), const),
                  vec, vec, vec, vec, vec,
                  pl.BlockSpec((2 * LANE, 2 * LANE), const)],
        out_specs=pl.BlockSpec((T, RWKV_W), row),
        scratch_shapes=[pltpu.VMEM((RWKV_HEADS, RWKV_HEAD_DIM, RWKV_HEAD_DIM), F32)],
        compiler_params=_cparams(("parallel", "arbitrary")),
    )(h, h, mu, w0, wup, a0, aup, gup, k_k, k_a, r_k, gn_w, gn_b, head_ones)


def _ret_kernel(h_ref, cos_ref, sin_ref, dmat_ref, qd_ref, kd_ref, cd_ref, gw_ref, gb_ref,
                o_ref, R_ref, *, nchunk):
    C = RET_CHUNK
    d = RET_HEAD_DIM
    H = RET_HEADS
    W = RET_W

    @pl.when(pl.program_id(1) == 0)
    def _():
        R_ref[...] = jnp.zeros(R_ref.shape, F32)

    pairs = [(c, hd) for c in range(nchunk) for hd in range(H)]
    rows = lambda c: slice(c * C, (c + 1) * C)
    cols = lambda part, hd: slice(part * W + hd * d, part * W + (hd + 1) * d)
    cos = [cos_ref[rows(c), :] for c in range(nchunk)]
    sin = [sin_ref[rows(c), :] for c in range(nchunk)]
    q = [h_ref[rows(c), cols(0, hd)] for c, hd in pairs]
    k = [h_ref[rows(c), cols(1, hd)] for c, hd in pairs]
    qr = [x * cos[c] + pltpu.roll(x, d // 2, 1) * sin[c] for x, (c, hd) in zip(q, pairs)]
    kr = [(x * cos[c] + pltpu.roll(x, d // 2, 1) * sin[c]) * (d ** -0.5)
          for x, (c, hd) in zip(k, pairs)]
    vb = [h_ref[rows(c), cols(2, hd)].astype(BF16) for c, hd in pairs]
    qb = [x.astype(BF16) for x in qr]
    kb = [x.astype(BF16) for x in kr]
    kdb = [(x * kd_ref[hd]).astype(BF16) for x, (c, hd) in zip(kr, pairs)]
    inner = [(_dot_nt(x, y) * dmat_ref[hd]).astype(BF16) for x, y, (c, hd) in zip(qb, kb, pairs)]
    o_in = [_dot(x, y) for x, y in zip(inner, vb)]
    kv = [_dot_tn(x, y) for x, y in zip(kdb, vb)]

    R = [R_ref[hd] for hd in range(H)]
    outs = []
    for c in range(nchunk):
        ch = [c * H + hd for hd in range(H)]
        Rsp = [_split2(x) for x in R]
        outs += [o_in[j] + (_dot(qb[j], sp[0]) + _dot(qb[j], sp[1])) * qd_ref[hd]
                 for hd, (j, sp) in enumerate(zip(ch, Rsp))]
        R = [cd_ref[hd][0:1, :] * x + kv[j] for hd, (x, j) in enumerate(zip(R, ch))]
    for hd in range(H):
        R_ref[hd] = R[hd]

    for o, (c, hd) in zip(outs, pairs):
        mu = jnp.mean(o, axis=1, keepdims=True)
        oc = o - mu
        var = jnp.mean(oc * oc, axis=1, keepdims=True)
        gsl = slice(hd * d, (hd + 1) * d)
        on = oc * lax.rsqrt(var + RET_GN_EPS) * gw_ref[:, gsl] + gb_ref[:, gsl]
        gate = h_ref[rows(c), cols(3, hd)]
        o_ref[rows(c), gsl] = (gate * _sigmoid(gate) * on).astype(o_ref.dtype)


def _ret_tables(S):
    C, d, H = RET_CHUNK, RET_HEAD_DIM, RET_HEADS
    half = d // 2
    inv = 1.0 / (ROPE_BASE ** (jnp.arange(half, dtype=F32) / half))
    ang = jnp.arange(S, dtype=F32)[:, None] * inv[None, :]
    cos = jnp.cos(ang)
    sin = jnp.sin(ang)
    cos2 = jnp.concatenate([cos, cos], axis=1)
    sin2 = jnp.concatenate([-sin, sin], axis=1)
    log_g = jnp.log(1.0 - 2.0 ** (-5.0 - jnp.arange(H, dtype=F32)))
    pos = jnp.arange(C, dtype=F32)
    rel = pos[:, None] - pos[None, :]
    dmat = jnp.where(rel >= 0, jnp.exp(log_g[:, None, None] * jnp.maximum(rel, 0.0)), 0.0)
    qd = jnp.broadcast_to(jnp.exp(log_g[:, None] * (pos + 1.0))[..., None], (H, C, d))
    kd = jnp.broadcast_to(jnp.exp(log_g[:, None] * (C - 1.0 - pos))[..., None], (H, C, d))
    cd = jnp.broadcast_to(jnp.exp(log_g * C)[:, None, None], (H, 8, d))
    return cos2, sin2, dmat, qd, kd, cd


def _retention(h, tables, gn_w, gn_b, B, S):
    cos2, sin2, dmat, qd, kd, cd = tables
    C = RET_CHUNK
    T = min(512, S)
    nt = S // T
    H = RET_HEADS
    tab = pl.BlockSpec((T, LANE), lambda b, t: (t, 0))
    per_head = lambda rows: pl.BlockSpec((H, rows, LANE), lambda b, t: (0, 0, 0))
    vec = pl.BlockSpec((1, RET_W), lambda b, t: (0, 0))
    return pl.pallas_call(
        functools.partial(_ret_kernel, nchunk=T // C),
        name="retention",
        out_shape=jax.ShapeDtypeStruct((B * S, RET_W), BF16),
        grid=(B, nt),
        in_specs=[pl.BlockSpec((T, RET_COLS), lambda b, t: (b * nt + t, 0)),
                  tab, tab, per_head(C), per_head(C), per_head(C), per_head(8), vec, vec],
        out_specs=pl.BlockSpec((T, RET_W), lambda b, t: (b * nt + t, 0)),
        scratch_shapes=[pltpu.VMEM((H, RET_HEAD_DIM, RET_HEAD_DIM), F32)],
        compiler_params=_cparams(("parallel", "arbitrary")),
    )(h, cos2, sin2, dmat, qd, kd, cd, gn_w, gn_b)


def _outproj_kernel(x_ref, yf_ref, yr_ref, yt_ref, w_ref, g_ref, b_ref, o_ref):
    r0, r1 = FOX_W, FOX_W + RWKV_W
    y = (_dot(yf_ref[...], w_ref[:r0, :]) + _dot(yr_ref[...], w_ref[r0:r1, :])
         + _dot(yt_ref[...], w_ref[r1:, :]))
    o_ref[...] = _layer_norm_rows(ALPHA * x_ref[...] + y, g_ref[...], b_ref[...])


def _out_proj_ln(x2d, y_fox, y_rwkv, y_ret, w_out, l, g, b):
    M, D = x2d.shape
    tm = min(512, M)
    row = lambda i: (i, 0)
    const = lambda i: (0, 0)
    return pl.pallas_call(
        _outproj_kernel,
        name="out_proj_ln",
        out_shape=jax.ShapeDtypeStruct((M, D), F32),
        grid=(M // tm,),
        in_specs=[pl.BlockSpec((tm, D), row), pl.BlockSpec((tm, FOX_W), row),
                  pl.BlockSpec((tm, RWKV_W), row), pl.BlockSpec((tm, RET_W), row),
                  pl.BlockSpec((None, D, D), lambda i: (l, 0, 0)),
                  pl.BlockSpec((1, D), const), pl.BlockSpec((1, D), const)],
        out_specs=pl.BlockSpec((tm, D), row),
        compiler_params=_cparams(("parallel",)),
    )(x2d, y_fox, y_rwkv, y_ret, w_out, g, b)


def _ffn_kernel(x_ref, wu_ref, wd_ref, g_ref, b_ref, o_ref, ob_ref, acc_ref, xb_ref):
    f = pl.program_id(1)

    @pl.when(f == 0)
    def _():
        x = x_ref[...]
        xb_ref[...] = x.astype(BF16)
        acc_ref[...] = ALPHA * x

    hid = jnp.maximum(_dot(xb_ref[...], wu_ref[...]), 0.0)
    acc_ref[...] += _dot((hid * hid).astype(BF16), wd_ref[...])

    @pl.when(f == pl.num_programs(1) - 1)
    def _():
        y = _layer_norm_rows(acc_ref[...], g_ref[...], b_ref[...])
        o_ref[...] = y
        ob_ref[...] = y.astype(BF16)


def _ffn_ln(x2d, w_up, w_down, l, g, b):
    M, D = x2d.shape
    F = w_up.shape[2]
    tm = min(512, M)
    tf = 1024
    return pl.pallas_call(
        _ffn_kernel,
        name="ffn_ln",
        out_shape=(jax.ShapeDtypeStruct((M, D), F32), jax.ShapeDtypeStruct((M, D), BF16)),
        grid=(M // tm, F // tf),
        in_specs=[pl.BlockSpec((tm, D), lambda i, f: (i, 0)),
                  pl.BlockSpec((None, D, tf), lambda i, f: (l, 0, f)),
                  pl.BlockSpec((None, tf, D), lambda i, f: (l, f, 0)),
                  pl.BlockSpec((1, D), lambda i, f: (0, 0)),
                  pl.BlockSpec((1, D), lambda i, f: (0, 0))],
        out_specs=(pl.BlockSpec((tm, D), lambda i, f: (i, 0)),
                   pl.BlockSpec((tm, D), lambda i, f: (i, 0))),
        scratch_shapes=[pltpu.VMEM((tm, D), F32), pltpu.VMEM((tm, D), BF16)],
        compiler_params=_cparams(("parallel", "arbitrary")),
    )(x2d, w_up, w_down, g, b)


def _permute_w_in(w_in):
    L, D, _ = w_in.shape
    w_in = w_in.astype(BF16)
    w_fox = w_in[:, :, :FOX_COLS]
    w_rwkv = w_in[:, :, FOX_COLS:FOX_COLS + RWKV_COLS]
    w_ret = w_in[:, :, FOX_COLS + RWKV_COLS:]
    w_ff = w_fox[:, :, 3 * FOX_W:]
    pad = jnp.zeros((L, D, LANE - FOX_HEADS), BF16)
    w_q = w_fox[:, :, :3 * FOX_W]
    w_rw = jnp.concatenate([w_rwkv, w_ff, pad], axis=-1)
    return w_q, w_rw, w_ret


def _layer(x2d, xb, l, p, tables, B, S):
    hq = _in_proj(xb, p['w_q'], l, p['q_scale'])
    h = _in_proj(xb, p['w_rw'], l)
    ht = _in_proj(xb, p['w_ret'], l)
    c = _fox_c(h, p['fox_bias'], B, S)
    c3 = c[:, :FOX_HEADS, :].reshape(B * FOX_HEADS, 1, S)
    y_fox = _fox_attention(hq, c3, B, S)
    y_rwkv = _rwkv_mix(h, p['mu'], p['w0'], p['w_up_lora'], p['a0'], p['a_up'], p['g_up'],
                       p['k_k'], p['k_a'], p['r_k'], p['rgn_w'], p['rgn_b'], B, S)
    y_ret = _retention(ht, tables, p['tgn_w'], p['tgn_b'], B, S)
    x2d = _out_proj_ln(x2d, y_fox, y_rwkv, y_ret, p['w_out'], l, p['ln1_g'], p['ln1_b'])
    return _ffn_ln(x2d, p['w_up'], p['w_down'], l, p['ln2_g'], p['ln2_b'])


def kernel(x, w_in, fox_forget_bias, rwkv_mu, rwkv_w0, rwkv_w_up, rwkv_a0, rwkv_a_up, rwkv_g_up,
           rwkv_k_k, rwkv_k_a, rwkv_r_k, rwkv_gn_w, rwkv_gn_b, ret_gn_w, ret_gn_b, w_out, ln1_g,
           ln1_b, w_up, w_down, ln2_g, ln2_b):
    B, S, D = x.shape
    L = w_in.shape[0]
    w_q, w_rw, w_ret = _permute_w_in(w_in)
    q_scale = jnp.concatenate([jnp.full((1, FOX_W), FOX_Q_SCALE, F32),
                               jnp.ones((1, 2 * FOX_W), F32)], axis=1)
    w_out_b = w_out.astype(BF16)
    w_up_b = w_up.astype(BF16)
    w_down_b = w_down.astype(BF16)
    bias_pad = jnp.pad(fox_forget_bias, ((0, 0), (0, LANE - FOX_HEADS)))
    tables = _ret_tables(S)
    row = lambda t: t.reshape(1, -1)
    x2d = x.reshape(B * S, D)
    xb = x2d
    for l in range(L):
        p = {
            'w_rw': w_rw, 'w_ret': w_ret, 'w_q': w_q, 'q_scale': q_scale,
            'fox_bias': row(bias_pad[l]),
            'mu': row(rwkv_mu[l]), 'w0': row(rwkv_w0[l]), 'w_up_lora': rwkv_w_up[l].astype(BF16),
            'a0': row(rwkv_a0[l]), 'a_up': rwkv_a_up[l].astype(BF16),
            'g_up': rwkv_g_up[l].astype(BF16),
            'k_k': row(rwkv_k_k[l]), 'k_a': row(rwkv_k_a[l]), 'r_k': row(rwkv_r_k[l]),
            'rgn_w': row(rwkv_gn_w[l]), 'rgn_b': row(rwkv_gn_b[l]),
            'tgn_w': row(ret_gn_w[l]), 'tgn_b': row(ret_gn_b[l]),
            'w_out': w_out_b,
            'ln1_g': row(ln1_g[l]), 'ln1_b': row(ln1_b[l]),
            'w_up': w_up_b, 'w_down': w_down_b,
            'ln2_g': row(ln2_g[l]), 'ln2_b': row(ln2_b[l]),
        }
        x2d, xb = _layer(x2d, xb, l, p, tables, B, S)
    return x2d.reshape(B, S, D)
```

```python
import functools

import jax
import jax.numpy as jnp
from jax import lax
from jax.experimental import pallas as pl
from jax.experimental.pallas import tpu as pltpu

F32 = jnp.float32
BF16 = jnp.bfloat16

D_MODEL = 2048
DEPTH = 4
FOX_W = 768
RWKV_W = 640
RET_W = 640
FOX_HEAD_DIM = 128
FOX_HEADS = 6
RWKV_HEAD_DIM = 64
RWKV_HEADS = 10
RET_HEAD_DIM = 128
RET_HEADS = 5
DECAY_LORA = 64
ICLR_LORA = 64
GATE_LORA = 128
D_FF = 4 * D_MODEL
RET_CHUNK = 128
ROPE_BASE = 10000.0
LN_EPS = 1e-5
RWKV_GN_EPS = 64e-5
RET_GN_EPS = 1e-5
ALPHA = (2 * DEPTH) ** 0.25

FOX_COLS = 3 * FOX_W + FOX_HEADS
RWKV_COLS = 3 * RWKV_W + DECAY_LORA + ICLR_LORA + GATE_LORA
RET_COLS = 4 * RET_W

LANE = 128
H_FF0 = RWKV_COLS
LOG2E = 1.4426950408889634
FOX_Q_SCALE = FOX_HEAD_DIM ** -0.5 * LOG2E

FOX_BLOCKS_PER_TRIP = 4
FOX_Q_ROWS = 1024
RWKV_CHUNK = 64
RWKV_CHUNKS_PER_STEP = 4
NEG_BIG = -1e30

VMEM_LIMIT = 56 * 1024 * 1024


def _cparams(sem):
    return pltpu.CompilerParams(dimension_semantics=sem, vmem_limit_bytes=VMEM_LIMIT)


def _dot(a, b):
    return jnp.dot(a, b, preferred_element_type=F32)


def _dot_nt(a, b):
    return lax.dot_general(a, b, (((1,), (1,)), ((), ())), preferred_element_type=F32)


def _dot_tn(a, b):
    return lax.dot_general(a, b, (((0,), (0,)), ((), ())), preferred_element_type=F32)


def _split3(x):
    hi = x.astype(BF16)
    r1 = x - hi.astype(F32)
    mid = r1.astype(BF16)
    lo = (r1 - mid.astype(F32)).astype(BF16)
    return hi, mid, lo


def _split2(x):
    hi = x.astype(BF16)
    lo = (x - hi.astype(F32)).astype(BF16)
    return hi, lo


def _mask_dot(mask_bf16, x):
    hi, mid, lo = _split3(x)
    return _dot(mask_bf16, hi) + _dot(mask_bf16, mid) + _dot(mask_bf16, lo)


def _softplus(z):
    return jnp.maximum(z, 0.0) + jnp.log1p(jnp.exp(-jnp.abs(z)))


def _sigmoid(z):
    return 1.0 / (1.0 + jnp.exp(-z))


def _layer_norm_rows(y, g, b):
    mu = jnp.mean(y, axis=-1, keepdims=True)
    yc = y - mu
    var = jnp.mean(yc * yc, axis=-1, keepdims=True)
    return yc * lax.rsqrt(var + LN_EPS) * g + b


def _inproj_kernel(x_ref, w_ref, o_ref):
    o_ref[...] = _dot(x_ref[...].astype(BF16), w_ref[...])


def _inproj_scaled_kernel(x_ref, w_ref, s_ref, o_ref):
    o_ref[...] = (_dot(x_ref[...].astype(BF16), w_ref[...]) * s_ref[...]).astype(o_ref.dtype)


def _in_proj(xb, w, l, col_scale=None):
    M, K = xb.shape
    N = w.shape[2]
    tm = min(1024 if xb.dtype == BF16 else 512, M)
    tn = N
    in_specs = [pl.BlockSpec((tm, K), lambda j, i: (i, 0)),
                pl.BlockSpec((None, K, tn), lambda j, i: (l, 0, j))]
    args = [xb, w]
    if col_scale is not None:
        in_specs.append(pl.BlockSpec((1, tn), lambda j, i: (0, j)))
        args.append(col_scale)
    return pl.pallas_call(
        _inproj_kernel if col_scale is None else _inproj_scaled_kernel,
        name="in_proj",
        out_shape=jax.ShapeDtypeStruct((M, N), F32 if col_scale is None else BF16),
        grid=(N // tn, M // tm),
        in_specs=in_specs,
        out_specs=pl.BlockSpec((tm, tn), lambda j, i: (i, j)),
        compiler_params=_cparams(("parallel", "parallel")),
    )(*args)


def _fox_c_kernel(f_ref, bias_ref, c_ref, *, rows):
    S = f_ref.shape[0]
    r = lax.broadcasted_iota(jnp.int32, (rows, rows), 0)
    c = lax.broadcasted_iota(jnp.int32, (rows, rows), 1)
    tri = (c <= r).astype(BF16)

    def body(i, carry):
        start = pl.multiple_of(i * rows, rows)
        z = f_ref[pl.ds(start, rows), :] + bias_ref[...]
        ls = -_softplus(-z)
        cs = _mask_dot(tri, ls) + carry
        c_ref[:, pl.ds(start, rows)] = (cs * LOG2E).T[0:8, :]
        return cs[rows - 1:rows, :]

    lax.fori_loop(0, S // rows, body, jnp.zeros((1, LANE), F32))


def _fox_c(h, bias_pad, B, S):
    rows = min(256, S)
    return pl.pallas_call(
        functools.partial(_fox_c_kernel, rows=rows),
        name="fox_c",
        out_shape=jax.ShapeDtypeStruct((B, 8, S), F32),
        grid=(B,),
        in_specs=[pl.BlockSpec((S, LANE), lambda b: (b, H_FF0 // LANE)),
                  pl.BlockSpec((1, LANE), lambda b: (0, 0))],
        out_specs=pl.BlockSpec((None, 8, S), lambda b: (b, 0, 0)),
        compiler_params=_cparams(("parallel",)),
    )(h, bias_pad)


def _fox_attn_kernel(q_ref, qn_ref, k_ref, v_ref, c_ref, o_ref, s0_ref, *, tq, tk):
    i = pl.program_id(2)
    d = q_ref.shape[1]
    q0 = pl.multiple_of(i * tq, tq)
    c0 = c_ref[:, pl.ds(q0, tq)][:, 0:1]
    q = q_ref[...]

    def scores(start, row_lo=0):
        bias = c0 - c_ref[:, pl.ds(start, tk)]
        return _dot_nt(q[row_lo:], k_ref[pl.ds(start, tk), :]) + bias

    def next_step_scores():
        qn0 = pl.multiple_of(jnp.minimum(i + 1, pl.num_programs(2) - 1) * tq, tq)
        cn = c_ref[:, pl.ds(qn0, tq)][:, 0:1]
        return _dot_nt(qn_ref[...], k_ref[0:tk, :]) + (cn - c_ref[:, 0:tk])

    ones = jnp.ones((tk, d), BF16)

    def update(start, sc, carry, masked, row_lo=0):
        m_all, acc_all = carry
        m, acc = m_all[row_lo:], acc_all[row_lo:]
        if masked:
            row = lax.broadcasted_iota(jnp.int32, sc.shape, 0) + (q0 + row_lo)
            col = lax.broadcasted_iota(jnp.int32, sc.shape, 1) + start
            sc = jnp.where(col <= row, sc, NEG_BIG)
        m_new = jnp.maximum(m, jnp.max(sc, axis=1, keepdims=True))
        alpha = jnp.exp2(m - m_new)
        p = jnp.exp2((sc - m_new).astype(BF16))
        vb = jnp.concatenate([v_ref[pl.ds(start, tk), :], ones], axis=1)
        acc = alpha * acc + _dot(p, vb)
        if row_lo:
            m_new = jnp.concatenate([m_all[:row_lo], m_new], axis=0)
            acc = jnp.concatenate([acc_all[:row_lo], acc], axis=0)
        return m_new, acc

    U = FOX_BLOCKS_PER_TRIP

    D = tq // tk

    def walk(base, n, carry, last):
        skip = lambda u: max(u - (n - D), 0) * tk if last else 0
        s_cur = s0_ref[...]
        for u in range(n):
            start = pl.multiple_of(base + u * tk, tk)
            final = last and u == n - 1
            s_next = (next_step_scores() if final
                      else scores(pl.multiple_of(start + tk, tk), skip(u + 1)))
            carry = update(start, s_cur, carry, last and u >= n - D, skip(u))
            s_cur = s_next
        s0_ref[...] = s_cur
        return carry

    @pl.when(i == 0)
    def _():
        s0_ref[...] = scores(0)

    init = (jnp.full((tq, 1), NEG_BIG, F32), jnp.zeros((tq, 2 * d), F32))
    n_full = i * D
    trips = n_full // U
    carry = lax.fori_loop(
        0, trips, lambda t, c: walk(pl.multiple_of(t * (U * tk), tk), U, c, False), init)

    base = pl.multiple_of(trips * (U * tk), tk)
    rems = list(range(0, U, D))
    tails = [functools.partial(walk, base, r + D, last=True) for r in rems]
    m, acc = lax.switch((n_full - trips * U) // D, tails, carry)
    o_ref[...] = (acc[:, :d] / acc[:, d:]).astype(o_ref.dtype)


def _fox_attention(hq, c3, B, S):
    tk = min(512, S)
    tq = min(FOX_Q_ROWS, S)
    nq = S // tq
    nh = FOX_HEADS
    return pl.pallas_call(
        functools.partial(_fox_attn_kernel, tq=tq, tk=tk),
        name="fox_attn",
        scratch_shapes=[pltpu.VMEM((tq, tk), F32)],
        out_shape=jax.ShapeDtypeStruct((B * S, FOX_W), BF16),
        grid=(B, FOX_HEADS, nq),
        in_specs=[pl.BlockSpec((tq, LANE), lambda b, hd, i: (b * nq + i, hd)),
                  pl.BlockSpec((tq, LANE),
                               lambda b, hd, i: (b * nq + jnp.minimum(i + 1, nq - 1), hd)),
                  pl.BlockSpec((S, LANE), lambda b, hd, i: (b, nh + hd)),
                  pl.BlockSpec((S, LANE), lambda b, hd, i: (b, 2 * nh + hd)),
                  pl.BlockSpec((None, 1, S), lambda b, hd, i: (b * nh + hd, 0, 0))],
        out_specs=pl.BlockSpec((tq, LANE), lambda b, hd, i: (b * nq + i, hd)),
        compiler_params=_cparams(("parallel", "parallel", "arbitrary")),
    )(hq, hq, hq, hq, c3)


def _rwkv_premix(h, prev, mu, w0, wup, a0, aup, gup):
    rows = lax.broadcasted_iota(jnp.int32, h.shape, 0)
    hp = jnp.where(rows == 0, prev, pltpu.roll(h, 1, 0))
    hs = h + (hp - h) * mu
    W3 = 3 * RWKV_W
    r = hs[:, :RWKV_W]
    k = hs[:, RWKV_W:2 * RWKV_W]
    v = hs[:, 2 * RWKV_W:W3]
    wd = hs[:, W3:W3 + DECAY_LORA]
    ad = hs[:, W3 + DECAY_LORA:W3 + DECAY_LORA + ICLR_LORA]
    gd = hs[:, W3 + DECAY_LORA + ICLR_LORA:]
    z = w0 + _dot(jnp.tanh(wd).astype(BF16), wup)
    w = -_softplus(-z) - 0.5
    lw = -jnp.exp(w)
    a = _sigmoid(a0 + _dot(ad.astype(BF16), aup))
    g = _dot(_sigmoid(gd).astype(BF16), gup)
    return r, k, v, lw, a, g


def _head_sums(x, e_ref):
    hi, lo = _split2(x)
    e = e_ref[...]
    e_last = e[:LANE, :LANE]
    outs = []
    for c0 in range(0, RWKV_W, 2 * LANE):
        w = min(2 * LANE, RWKV_W - c0)
        ee = e if w == 2 * LANE else e_last
        outs.append(_dot(hi[:, c0:c0 + w], ee) + _dot(lo[:, c0:c0 + w], ee))
    return jnp.concatenate(outs, axis=1)


def _rwkv_scan_kernel(h_ref, hprev_ref, mu_ref, w0_ref, wup_ref, a0_ref, aup_ref, gup_ref,
                      kk_ref, ka_ref, rk_ref, gw_ref, gb_ref, e_ref, o_ref, st_ref, *, nchunk):
    C = RWKV_CHUNK
    N = RWKV_HEAD_DIM
    T = nchunk * C
    first = pl.program_id(1) == 0

    @pl.when(first)
    def _():
        st_ref[...] = jnp.zeros(st_ref.shape, F32)

    prev = jnp.where(first, 0.0, hprev_ref[7:8, :])
    r, k, v, lw, a, g = _rwkv_premix(h_ref[...], prev, mu_ref[...], w0_ref[...], wup_ref[...],
                                     a0_ref[...], aup_ref[...], gup_ref[...])

    ri = lax.broadcasted_iota(jnp.int32, (C, C), 0)
    ci = lax.broadcasted_iota(jnp.int32, (C, C), 1)
    incl = ci <= ri
    strict = ci < ri

    rt = lax.broadcasted_iota(jnp.int32, (T, T), 0)
    ct = lax.broadcasted_iota(jnp.int32, (T, T), 1)
    tri = ((ct <= rt) & (ct >= (rt // C) * C)).astype(BF16)
    lw_hi, lw_lo = _split2(lw)
    cum = _dot(tri, lw_hi) + _dot(tri, lw_lo)
    cls = [cum[(c + 1) * C - 1:(c + 1) * C, :] for c in range(nchunk)]
    cl_rows = jnp.concatenate([jnp.broadcast_to(t, (C, RWKV_W)) for t in cls], axis=0)
    e_in = jnp.exp(cum)
    e_ex = jnp.exp(cum - lw)
    e_neg = jnp.exp(-cum)
    e_tail = jnp.exp(cl_rows - cum)
    pcs = [jnp.exp(t) for t in cls]

    kk = k * kk_ref[...]
    kkn = kk / jnp.maximum(jnp.sqrt(_head_sums(kk * kk, e_ref)), 1e-12)
    k2 = k * (1.0 + (a - 1.0) * ka_ref[...])
    bvec = kkn * a
    At_f = -kkn * e_ex
    Rt_f = r * e_in
    AtB = At_f.astype(BF16)
    RtB = Rt_f.astype(BF16)
    BtB = (bvec * e_neg).astype(BF16)
    KtB = (k2 * e_neg).astype(BF16)
    BpB = (bvec * e_tail).astype(BF16)
    KpB = (k2 * e_tail).astype(BF16)
    vB = v.astype(BF16)

    HD = range(RWKV_HEADS)
    hsl = [slice(hd * N, (hd + 1) * N) for hd in HD]
    sls = [(slice(c * C, (c + 1) * C), hs) for c in range(nchunk) for hs in hsl]
    At = [At_f[rs, hs] for rs, hs in sls]
    Rt = [Rt_f[rs, hs] for rs, hs in sls]
    vb = [vB[rs, hs] for rs, hs in sls]
    Bt = [BtB[rs, hs] for rs, hs in sls]
    Kt = [KtB[rs, hs] for rs, hs in sls]
    Bp = [BpB[rs, hs] for rs, hs in sls]
    Kp = [KpB[rs, hs] for rs, hs in sls]

    AR = [jnp.concatenate([AtB[rs, hs], RtB[rs, hs]], axis=0) for rs, hs in sls]
    BK = [jnp.concatenate([x, y], axis=0) for x, y in zip(Bt, Kt)]
    G = [_dot_nt(x, y) for x, y in zip(AR, BK)]
    r2 = lax.broadcasted_iota(jnp.int32, (C, 2 * C), 0)
    c2 = lax.broadcasted_iota(jnp.int32, (C, 2 * C), 1)
    incl2 = jnp.where(c2 >= C, c2 - C, c2) <= r2
    P = [jnp.where(strict, t[:C, :C], 0.0) for t in G]
    Lak = [jnp.where(strict, t[:C, C:], 0.0).astype(BF16) for t in G]
    L2 = [jnp.where(incl2, t[C:], 0.0).astype(BF16) for t in G]

    X = [jnp.concatenate([x, _dot(y, z)], axis=1) for x, y, z in zip(At, Lak, vb)]
    steps = C.bit_length() - 1
    for it in range(steps):
        Pb = [t.astype(BF16) for t in P]
        Xb = [x.astype(BF16) for x in X]
        if it + 1 < steps:
            PX = [_dot(p, jnp.concatenate([x, p], axis=1)) for p, x in zip(Pb, Xb)]
            X = [x + t[:, :2 * N] for x, t in zip(X, PX)]
            P = [t[:, 2 * N:] for t in PX]
        else:
            X = [x + _dot(p, xb) for x, p, xb in zip(X, Pb, Xb)]
    Xb = [x.astype(BF16) for x in X]

    zpad = jnp.zeros((C, N), BF16)
    RHS = [jnp.concatenate([x, jnp.concatenate([zpad, z], axis=1)], axis=0)
           for x, z in zip(Xb, vb)]
    LX = [_dot(x, y) for x, y in zip(L2, RHS)]
    Qp = [(x + y[:, :N]).astype(BF16) for x, y in zip(Rt, LX)]
    Y0 = [y[:, N:] for y in LX]

    XB = [_dot_tn(x, b) for x, b in zip(Xb, Bp)]
    Mw = [t[:N].astype(BF16) for t in XB]
    Nn = [t[N:] + _dot_tn(x, y) for t, x, y in zip(XB, vb, Kp)]

    S = [st_ref[hd] for hd in HD]
    y_rows = []
    for c in range(nchunk):
        ch = [c * RWKV_HEADS + hd for hd in HD]
        Ssp = [_split2(s) for s in S]
        ys = [_dot_nt(Qp[j], sp[0]) + Y0[j] for j, sp in zip(ch, Ssp)]
        y_rows.append(jnp.concatenate(ys, axis=1))
        SM = [_dot(jnp.concatenate(sp, axis=0), Mw[j]) for sp, j in zip(Ssp, ch)]
        S = [s * pcs[c][:, hs] + sm[:N] + sm[N:] + Nn[j]
             for s, sm, j, hs in zip(S, SM, ch, hsl)]
    for hd in HD:
        st_ref[hd] = S[hd]

    y = jnp.concatenate(y_rows, axis=0)
    inv_n = 1.0 / N
    yc = y - _head_sums(y, e_ref) * inv_n
    var = _head_sums(yc * yc, e_ref) * inv_n
    yn = yc * lax.rsqrt(var + RWKV_GN_EPS) * gw_ref[...] + gb_ref[...]
    bonus = _head_sums(r * k2 * rk_ref[...], e_ref) * v
    o_ref[...] = ((yn + bonus) * g).astype(o_ref.dtype)


def _rwkv_mix(h, mu, w0, wup, a0, aup, gup, k_k, k_a, r_k, gn_w, gn_b, B, S):
    nchunk = RWKV_CHUNKS_PER_STEP
    T = nchunk * RWKV_CHUNK
    nc = S // T
    row = lambda b, c: (b * nc + c, 0)
    prev_row = lambda b, c: (jnp.maximum((b * nc + c) * (T // 8) - 1, 0), 0)
    const = lambda b, c: (0, 0)
    vec = pl.BlockSpec((1, RWKV_W), const)
    hid = jnp.arange(2 * LANE, dtype=jnp.int32) // RWKV_HEAD_DIM
    head_ones = (hid[:, None] == hid[None, :]).astype(BF16)
    return pl.pallas_call(
        functools.partial(_rwkv_scan_kernel, nchunk=nchunk),
        name="rwkv_scan",
        out_shape=jax.ShapeDtypeStruct((B * S, RWKV_W), BF16),
        grid=(B, nc),
        in_specs=[pl.BlockSpec((T, RWKV_COLS), row),
                  pl.BlockSpec((8, RWKV_COLS), prev_row),
                  pl.BlockSpec((1, RWKV_COLS), const),
                  vec,
                  pl.BlockSpec((DECAY_LORA, RWKV_W), const),
                  vec,
                  pl.BlockSpec((ICLR_LORA, RWKV_W), const),
                  pl.BlockSpec((GATE_LORA, RWKV_W), const),
                  vec, vec, vec, vec, vec,
                  pl.BlockSpec((2 * LANE, 2 * LANE), const)],
        out_specs=pl.BlockSpec((T, RWKV_W), row),
        scratch_shapes=[pltpu.VMEM((RWKV_HEADS, RWKV_HEAD_DIM, RWKV_HEAD_DIM), F32)],
        compiler_params=_cparams(("parallel", "arbitrary")),
    )(h, h, mu, w0, wup, a0, aup, gup, k_k, k_a, r_k, gn_w, gn_b, head_ones)


def _ret_kernel(h_ref, cos_ref, sin_ref, dmat_ref, qd_ref, kd_ref, cd_ref, gw_ref, gb_ref,
                o_ref, R_ref, *, nchunk):
    C = RET_CHUNK
    d = RET_HEAD_DIM
    H = RET_HEADS
    W = RET_W

    @pl.when(pl.program_id(1) == 0)
    def _():
        R_ref[...] = jnp.zeros(R_ref.shape, F32)

    pairs = [(c, hd) for c in range(nchunk) for hd in range(H)]
    rows = lambda c: slice(c * C, (c + 1) * C)
    cols = lambda part, hd: slice(part * W + hd * d, part * W + (hd + 1) * d)
    cos = [cos_ref[rows(c), :] for c in range(nchunk)]
    sin = [sin_ref[rows(c), :] for c in range(nchunk)]
    q = [h_ref[rows(c), cols(0, hd)] for c, hd in pairs]
    k = [h_ref[rows(c), cols(1, hd)] for c, hd in pairs]
    qr = [x * cos[c] + pltpu.roll(x, d // 2, 1) * sin[c] for x, (c, hd) in zip(q, pairs)]
    kr = [(x * cos[c] + pltpu.roll(x, d // 2, 1) * sin[c]) * (d ** -0.5)
          for x, (c, hd) in zip(k, pairs)]
    vb = [h_ref[rows(c), cols(2, hd)].astype(BF16) for c, hd in pairs]
    qb = [x.astype(BF16) for x in qr]
    kb = [x.astype(BF16) for x in kr]
    kdb = [(x * kd_ref[hd]).astype(BF16) for x, (c, hd) in zip(kr, pairs)]
    inner = [(_dot_nt(x, y) * dmat_ref[hd]).astype(BF16) for x, y, (c, hd) in zip(qb, kb, pairs)]
    o_in = [_dot(x, y) for x, y in zip(inner, vb)]
    kv = [_dot_tn(x, y) for x, y in zip(kdb, vb)]

    R = [R_ref[hd] for hd in range(H)]
    outs = []
    for c in range(nchunk):
        ch = [c * H + hd for hd in range(H)]
        Rsp = [_split2(x) for x in R]
        outs += [o_in[j] + (_dot(qb[j], sp[0]) + _dot(qb[j], sp[1])) * qd_ref[hd]
                 for hd, (j, sp) in enumerate(zip(ch, Rsp))]
        R = [cd_ref[hd][0:1, :] * x + kv[j] for hd, (x, j) in enumerate(zip(R, ch))]
    for hd in range(H):
        R_ref[hd] = R[hd]

    for o, (c, hd) in zip(outs, pairs):
        mu = jnp.mean(o, axis=1, keepdims=True)
        oc = o - mu
        var = jnp.mean(oc * oc, axis=1, keepdims=True)
        gsl = slice(hd * d, (hd + 1) * d)
        on = oc * lax.rsqrt(var + RET_GN_EPS) * gw_ref[:, gsl] + gb_ref[:, gsl]
        gate = h_ref[rows(c), cols(3, hd)]
        o_ref[rows(c), gsl] = (gate * _sigmoid(gate) * on).astype(o_ref.dtype)


def _ret_tables(S):
    C, d, H = RET_CHUNK, RET_HEAD_DIM, RET_HEADS
    half = d // 2
    inv = 1.0 / (ROPE_BASE ** (jnp.arange(half, dtype=F32) / half))
    ang = jnp.arange(S, dtype=F32)[:, None] * inv[None, :]
    cos = jnp.cos(ang)
    sin = jnp.sin(ang)
    cos2 = jnp.concatenate([cos, cos], axis=1)
    sin2 = jnp.concatenate([-sin, sin], axis=1)
    log_g = jnp.log(1.0 - 2.0 ** (-5.0 - jnp.arange(H, dtype=F32)))
    pos = jnp.arange(C, dtype=F32)
    rel = pos[:, None] - pos[None, :]
    dmat = jnp.where(rel >= 0, jnp.exp(log_g[:, None, None] * jnp.maximum(rel, 0.0)), 0.0)
    qd = jnp.broadcast_to(jnp.exp(log_g[:, None] * (pos + 1.0))[..., None], (H, C, d))
    kd = jnp.broadcast_to(jnp.exp(log_g[:, None] * (C - 1.0 - pos))[..., None], (H, C, d))
    cd = jnp.broadcast_to(jnp.exp(log_g * C)[:, None, None], (H, 8, d))
    return cos2, sin2, dmat, qd, kd, cd


def _retention(h, tables, gn_w, gn_b, B, S):
    cos2, sin2, dmat, qd, kd, cd = tables
    C = RET_CHUNK
    T = min(512, S)
    nt = S // T
    H = RET_HEADS
    tab = pl.BlockSpec((T, LANE), lambda b, t: (t, 0))
    per_head = lambda rows: pl.BlockSpec((H, rows, LANE), lambda b, t: (0, 0, 0))
    vec = pl.BlockSpec((1, RET_W), lambda b, t: (0, 0))
    return pl.pallas_call(
        functools.partial(_ret_kernel, nchunk=T // C),
        name="retention",
        out_shape=jax.ShapeDtypeStruct((B * S, RET_W), BF16),
        grid=(B, nt),
        in_specs=[pl.BlockSpec((T, RET_COLS), lambda b, t: (b * nt + t, 0)),
                  tab, tab, per_head(C), per_head(C), per_head(C), per_head(8), vec, vec],
        out_specs=pl.BlockSpec((T, RET_W), lambda b, t: (b * nt + t, 0)),
        scratch_shapes=[pltpu.VMEM((H, RET_HEAD_DIM, RET_HEAD_DIM), F32)],
        compiler_params=_cparams(("parallel", "arbitrary")),
    )(h, cos2, sin2, dmat, qd, kd, cd, gn_w, gn_b)


def _outproj_kernel(x_ref, yf_ref, yr_ref, yt_ref, w_ref, g_ref, b_ref, o_ref):
    r0, r1 = FOX_W, FOX_W + RWKV_W
    y = (_dot(yf_ref[...], w_ref[:r0, :]) + _dot(yr_ref[...], w_ref[r0:r1, :])
         + _dot(yt_ref[...], w_ref[r1:, :]))
    o_ref[...] = _layer_norm_rows(ALPHA * x_ref[...] + y, g_ref[...], b_ref[...])


def _out_proj_ln(x2d, y_fox, y_rwkv, y_ret, w_out, l, g, b):
    M, D = x2d.shape
    tm = min(512, M)
    row = lambda i: (i, 0)
    const = lambda i: (0, 0)
    return pl.pallas_call(
        _outproj_kernel,
        name="out_proj_ln",
        out_shape=jax.ShapeDtypeStruct((M, D), F32),
        grid=(M // tm,),
        in_specs=[pl.BlockSpec((tm, D), row), pl.BlockSpec((tm, FOX_W), row),
                  pl.BlockSpec((tm, RWKV_W), row), pl.BlockSpec((tm, RET_W), row),
                  pl.BlockSpec((None, D, D), lambda i: (l, 0, 0)),
                  pl.BlockSpec((1, D), const), pl.BlockSpec((1, D), const)],
        out_specs=pl.BlockSpec((tm, D), row),
        compiler_params=_cparams(("parallel",)),
    )(x2d, y_fox, y_rwkv, y_ret, w_out, g, b)


def _ffn_kernel(x_ref, wu_ref, wd_ref, g_ref, b_ref, o_ref, ob_ref, acc_ref, xb_ref):
    f = pl.program_id(1)

    @pl.when(f == 0)
    def _():
        x = x_ref[...]
        xb_ref[...] = x.astype(BF16)
        acc_ref[...] = ALPHA * x

    hid = jnp.maximum(_dot(xb_ref[...], wu_ref[...]), 0.0)
    acc_ref[...] += _dot((hid * hid).astype(BF16), wd_ref[...])

    @pl.when(f == pl.num_programs(1) - 1)
    def _():
        y = _layer_norm_rows(acc_ref[...], g_ref[...], b_ref[...])
        o_ref[...] = y
        ob_ref[...] = y.astype(BF16)


def _ffn_ln(x2d, w_up, w_down, l, g, b):
    M, D = x2d.shape
    F = w_up.shape[2]
    tm = min(512, M)
    tf = 1024
    return pl.pallas_call(
        _ffn_kernel,
        name="ffn_ln",
        out_shape=(jax.ShapeDtypeStruct((M, D), F32), jax.ShapeDtypeStruct((M, D), BF16)),
        grid=(M // tm, F // tf),
        in_specs=[pl.BlockSpec((tm, D), lambda i, f: (i, 0)),
                  pl.BlockSpec((None, D, tf), lambda i, f: (l, 0, f)),
                  pl.BlockSpec((None, tf, D), lambda i, f: (l, f, 0)),
                  pl.BlockSpec((1, D), lambda i, f: (0, 0)),
                  pl.BlockSpec((1, D), lambda i, f: (0, 0))],
        out_specs=(pl.BlockSpec((tm, D), lambda i, f: (i, 0)),
                   pl.BlockSpec((tm, D), lambda i, f: (i, 0))),
        scratch_shapes=[pltpu.VMEM((tm, D), F32), pltpu.VMEM((tm, D), BF16)],
        compiler_params=_cparams(("parallel", "arbitrary")),
    )(x2d, w_up, w_down, g, b)


def _permute_w_in(w_in):
    L, D, _ = w_in.shape
    w_in = w_in.astype(BF16)
    w_fox = w_in[:, :, :FOX_COLS]
    w_rwkv = w_in[:, :, FOX_COLS:FOX_COLS + RWKV_COLS]
    w_ret = w_in[:, :, FOX_COLS + RWKV_COLS:]
    w_ff = w_fox[:, :, 3 * FOX_W:]
    pad = jnp.zeros((L, D, LANE - FOX_HEADS), BF16)
    w_q = w_fox[:, :, :3 * FOX_W]
    w_rw = jnp.concatenate([w_rwkv, w_ff, pad], axis=-1)
    return w_q, w_rw, w_ret


def _layer(x2d, xb, l, p, tables, B, S):
    hq = _in_proj(xb, p['w_q'], l, p['q_scale'])
    h = _in_proj(xb, p['w_rw'], l)
    ht = _in_proj(xb, p['w_ret'], l)
    c = _fox_c(h, p['fox_bias'], B, S)
    c3 = c[:, :FOX_HEADS, :].reshape(B * FOX_HEADS, 1, S)
    y_fox = _fox_attention(hq, c3, B, S)
    y_rwkv = _rwkv_mix(h, p['mu'], p['w0'], p['w_up_lora'], p['a0'], p['a_up'], p['g_up'],
                       p['k_k'], p['k_a'], p['r_k'], p['rgn_w'], p['rgn_b'], B, S)
    y_ret = _retention(ht, tables, p['tgn_w'], p['tgn_b'], B, S)
    x2d = _out_proj_ln(x2d, y_fox, y_rwkv, y_ret, p['w_out'], l, p['ln1_g'], p['ln1_b'])
    return _ffn_ln(x2d, p['w_up'], p['w_down'], l, p['ln2_g'], p['ln2_b'])


def kernel(x, w_in, fox_forget_bias, rwkv_mu, rwkv_w0, rwkv_w_up, rwkv_a0, rwkv_a_up, rwkv_g_up,
           rwkv_k_k, rwkv_k_a, rwkv_r_k, rwkv_gn_w, rwkv_gn_b, ret_gn_w, ret_gn_b, w_out, ln1_g,
           ln1_b, w_up, w_down, ln2_g, ln2_b):
    B, S, D = x.shape
    L = w_in.shape[0]
    w_q, w_rw, w_ret = _permute_w_in(w_in)
    q_scale = jnp.concatenate([jnp.full((1, FOX_W), FOX_Q_SCALE, F32),
                               jnp.ones((1, 2 * FOX_W), F32)], axis=1)
    w_out_b = w_out.astype(BF16)
    w_up_b = w_up.astype(BF16)
    w_down_b = w_down.astype(BF16)
    bias_pad = jnp.pad(fox_forget_bias, ((0, 0), (0, LANE - FOX_HEADS)))
    tables = _ret_tables(S)
    row = lambda t: t.reshape(1, -1)
    x2d = x.reshape(B * S, D)
    xb = x2d
    for l in range(L):
        p = {
            'w_rw': w_rw, 'w_ret': w_ret, 'w_q': w_q, 'q_scale': q_scale,
            'fox_bias': row(bias_pad[l]),
            'mu': row(rwkv_mu[l]), 'w0': row(rwkv_w0[l]), 'w_up_lora': rwkv_w_up[l].astype(BF16),
            'a0': row(rwkv_a0[l]), 'a_up': rwkv_a_up[l].astype(BF16),
            'g_up': rwkv_g_up[l].astype(BF16),
            'k_k': row(rwkv_k_k[l]), 'k_a': row(rwkv_k_a[l]), 'r_k': row(rwkv_r_k[l]),
            'rgn_w': row(rwkv_gn_w[l]), 'rgn_b': row(rwkv_gn_b[l]),
            'tgn_w': row(ret_gn_w[l]), 'tgn_b': row(ret_gn_b[l]),
            'w_out': w_out_b,
            'ln1_g': row(ln1_g[l]), 'ln1_b': row(ln1_b[l]),
            'w_up': w_up_b, 'w_down': w_down_b,
            'ln2_g': row(ln2_g[l]), 'ln2_b': row(ln2_b[l]),
        }
        x2d, xb = _layer(x2d, xb, l, p, tables, B, S)
    return x2d.reshape(B, S, D)
```

```python
import functools

import jax
import jax.numpy as jnp
from jax import lax
from jax.experimental import pallas as pl
from jax.experimental.pallas import tpu as pltpu

F32 = jnp.float32
BF16 = jnp.bfloat16

D_MODEL = 2048
DEPTH = 4
FOX_W = 768
RWKV_W = 640
RET_W = 640
FOX_HEAD_DIM = 128
FOX_HEADS = 6
RWKV_HEAD_DIM = 64
RWKV_HEADS = 10
RET_HEAD_DIM = 128
RET_HEADS = 5
DECAY_LORA = 64
ICLR_LORA = 64
GATE_LORA = 128
D_FF = 4 * D_MODEL
RET_CHUNK = 128
ROPE_BASE = 10000.0
LN_EPS = 1e-5
RWKV_GN_EPS = 64e-5
RET_GN_EPS = 1e-5
ALPHA = (2 * DEPTH) ** 0.25

FOX_COLS = 3 * FOX_W + FOX_HEADS
RWKV_COLS = 3 * RWKV_W + DECAY_LORA + ICLR_LORA + GATE_LORA
RET_COLS = 4 * RET_W

LANE = 128
H_FF0 = RWKV_COLS
LOG2E = 1.4426950408889634
FOX_Q_SCALE = FOX_HEAD_DIM ** -0.5 * LOG2E

FOX_BLOCKS_PER_TRIP = 4
FOX_Q_ROWS = 1024
RWKV_CHUNK = 64
RWKV_CHUNKS_PER_STEP = 4
NEG_BIG = -1e30

VMEM_LIMIT = 56 * 1024 * 1024


def _cparams(sem):
    return pltpu.CompilerParams(dimension_semantics=sem, vmem_limit_bytes=VMEM_LIMIT)


def _dot(a, b):
    return jnp.dot(a, b, preferred_element_type=F32)


def _dot_nt(a, b):
    return lax.dot_general(a, b, (((1,), (1,)), ((), ())), preferred_element_type=F32)


def _dot_tn(a, b):
    return lax.dot_general(a, b, (((0,), (0,)), ((), ())), preferred_element_type=F32)


def _split3(x):
    hi = x.astype(BF16)
    r1 = x - hi.astype(F32)
    mid = r1.astype(BF16)
    lo = (r1 - mid.astype(F32)).astype(BF16)
    return hi, mid, lo


def _split2(x):
    hi = x.astype(BF16)
    lo = (x - hi.astype(F32)).astype(BF16)
    return hi, lo


def _mask_dot(mask_bf16, x):
    hi, mid, lo = _split3(x)
    return _dot(mask_bf16, hi) + _dot(mask_bf16, mid) + _dot(mask_bf16, lo)


def _softplus(z):
    return jnp.maximum(z, 0.0) + jnp.log1p(jnp.exp(-jnp.abs(z)))


def _sigmoid(z):
    return 1.0 / (1.0 + jnp.exp(-z))


def _layer_norm_rows(y, g, b):
    mu = jnp.mean(y, axis=-1, keepdims=True)
    yc = y - mu
    var = jnp.mean(yc * yc, axis=-1, keepdims=True)
    return yc * lax.rsqrt(var + LN_EPS) * g + b


def _inproj_kernel(x_ref, w_ref, o_ref):
    o_ref[...] = _dot(x_ref[...].astype(BF16), w_ref[...])


def _inproj_scaled_kernel(x_ref, w_ref, s_ref, o_ref):
    o_ref[...] = (_dot(x_ref[...].astype(BF16), w_ref[...]) * s_ref[...]).astype(o_ref.dtype)


def _in_proj(xb, w, l, col_scale=None):
    M, K = xb.shape
    N = w.shape[2]
    tm = min(1024 if xb.dtype == BF16 else 512, M)
    tn = N
    in_specs = [pl.BlockSpec((tm, K), lambda j, i: (i, 0)),
                pl.BlockSpec((None, K, tn), lambda j, i: (l, 0, j))]
    args = [xb, w]
    if col_scale is not None:
        in_specs.append(pl.BlockSpec((1, tn), lambda j, i: (0, j)))
        args.append(col_scale)
    return pl.pallas_call(
        _inproj_kernel if col_scale is None else _inproj_scaled_kernel,
        name="in_proj",
        out_shape=jax.ShapeDtypeStruct((M, N), F32 if col_scale is None else BF16),
        grid=(N // tn, M // tm),
        in_specs=in_specs,
        out_specs=pl.BlockSpec((tm, tn), lambda j, i: (i, j)),
        compiler_params=_cparams(("parallel", "parallel")),
    )(*args)


def _fox_c_kernel(f_ref, bias_ref, c_ref, *, rows):
    S = f_ref.shape[0]
    r = lax.broadcasted_iota(jnp.int32, (rows, rows), 0)
    c = lax.broadcasted_iota(jnp.int32, (rows, rows), 1)
    tri = (c <= r).astype(BF16)

    def body(i, carry):
        start = pl.multiple_of(i * rows, rows)
        z = f_ref[pl.ds(start, rows), :] + bias_ref[...]
        ls = -_softplus(-z)
        cs = _mask_dot(tri, ls) + carry
        c_ref[:, pl.ds(start, rows)] = (cs * LOG2E).T[0:8, :]
        return cs[rows - 1:rows, :]

    lax.fori_loop(0, S // rows, body, jnp.zeros((1, LANE), F32))


def _fox_c(h, bias_pad, B, S):
    rows = min(256, S)
    return pl.pallas_call(
        functools.partial(_fox_c_kernel, rows=rows),
        name="fox_c",
        out_shape=jax.ShapeDtypeStruct((B, 8, S), F32),
        grid=(B,),
        in_specs=[pl.BlockSpec((S, LANE), lambda b: (b, H_FF0 // LANE)),
                  pl.BlockSpec((1, LANE), lambda b: (0, 0))],
        out_specs=pl.BlockSpec((None, 8, S), lambda b: (b, 0, 0)),
        compiler_params=_cparams(("parallel",)),
    )(h, bias_pad)


def _fox_attn_kernel(q_ref, qn_ref, k_ref, v_ref, c_ref, o_ref, s0_ref, *, tq, tk):
    i = pl.program_id(2)
    d = q_ref.shape[1]
    q0 = pl.multiple_of(i * tq, tq)
    c0 = c_ref[:, pl.ds(q0, tq)][:, 0:1]
    q = q_ref[...]

    def scores(start, row_lo=0):
        bias = c0 - c_ref[:, pl.ds(start, tk)]
        return _dot_nt(q[row_lo:], k_ref[pl.ds(start, tk), :]) + bias

    def next_step_scores():
        qn0 = pl.multiple_of(jnp.minimum(i + 1, pl.num_programs(2) - 1) * tq, tq)
        cn = c_ref[:, pl.ds(qn0, tq)][:, 0:1]
        return _dot_nt(qn_ref[...], k_ref[0:tk, :]) + (cn - c_ref[:, 0:tk])

    ones = jnp.ones((tk, d), BF16)

    def update(start, sc, carry, masked, row_lo=0):
        m_all, acc_all = carry
        m, acc = m_all[row_lo:], acc_all[row_lo:]
        if masked:
            row = lax.broadcasted_iota(jnp.int32, sc.shape, 0) + (q0 + row_lo)
            col = lax.broadcasted_iota(jnp.int32, sc.shape, 1) + start
            sc = jnp.where(col <= row, sc, NEG_BIG)
        m_new = jnp.maximum(m, jnp.max(sc, axis=1, keepdims=True))
        alpha = jnp.exp2(m - m_new)
        p = jnp.exp2((sc - m_new).astype(BF16))
        vb = jnp.concatenate([v_ref[pl.ds(start, tk), :], ones], axis=1)
        acc = alpha * acc + _dot(p, vb)
        if row_lo:
            m_new = jnp.concatenate([m_all[:row_lo], m_new], axis=0)
            acc = jnp.concatenate([acc_all[:row_lo], acc], axis=0)
        return m_new, acc

    U = FOX_BLOCKS_PER_TRIP

    D = tq // tk

    def walk(base, n, carry, last):
        skip = lambda u: max(u - (n - D), 0) * tk if last else 0
        s_cur = s0_ref[...]
        for u in range(n):
            start = pl.multiple_of(base + u * tk, tk)
            final = last and u == n - 1
            s_next = (next_step_scores() if final
                      else scores(pl.multiple_of(start + tk, tk), skip(u + 1)))
            carry = update(start, s_cur, carry, last and u >= n - D, skip(u))
            s_cur = s_next
        s0_ref[...] = s_cur
        return carry

    @pl.when(i == 0)
    def _():
        s0_ref[...] = scores(0)

    init = (jnp.full((tq, 1), NEG_BIG, F32), jnp.zeros((tq, 2 * d), F32))
    n_full = i * D
    trips = n_full // U
    carry = lax.fori_loop(
        0, trips, lambda t, c: walk(pl.multiple_of(t * (U * tk), tk), U, c, False), init)

    base = pl.multiple_of(trips * (U * tk), tk)
    rems = list(range(0, U, D))
    tails = [functools.partial(walk, base, r + D, last=True) for r in rems]
    m, acc = lax.switch((n_full - trips * U) // D, tails, carry)
    o_ref[...] = (acc[:, :d] / acc[:, d:]).astype(o_ref.dtype)


def _fox_attention(hq, c3, B, S):
    tk = min(512, S)
    tq = min(FOX_Q_ROWS, S)
    nq = S // tq
    nh = FOX_HEADS
    return pl.pallas_call(
        functools.partial(_fox_attn_kernel, tq=tq, tk=tk),
        name="fox_attn",
        scratch_shapes=[pltpu.VMEM((tq, tk), F32)],
        out_shape=jax.ShapeDtypeStruct((B * S, FOX_W), BF16),
        grid=(B, FOX_HEADS, nq),
        in_specs=[pl.BlockSpec((tq, LANE), lambda b, hd, i: (b * nq + i, hd)),
                  pl.BlockSpec((tq, LANE),
                               lambda b, hd, i: (b * nq + jnp.minimum(i + 1, nq - 1), hd)),
                  pl.BlockSpec((S, LANE), lambda b, hd, i: (b, nh + hd)),
                  pl.BlockSpec((S, LANE), lambda b, hd, i: (b, 2 * nh + hd)),
                  pl.BlockSpec((None, 1, S), lambda b, hd, i: (b * nh + hd, 0, 0))],
        out_specs=pl.BlockSpec((tq, LANE), lambda b, hd, i: (b * nq + i, hd)),
        compiler_params=_cparams(("parallel", "parallel", "arbitrary")),
    )(hq, hq, hq, hq, c3)


def _rwkv_premix(h, prev, mu, w0, wup, a0, aup, gup):
    rows = lax.broadcasted_iota(jnp.int32, h.shape, 0)
    hp = jnp.where(rows == 0, prev, pltpu.roll(h, 1, 0))
    hs = h + (hp - h) * mu
    W3 = 3 * RWKV_W
    r = hs[:, :RWKV_W]
    k = hs[:, RWKV_W:2 * RWKV_W]
    v = hs[:, 2 * RWKV_W:W3]
    wd = hs[:, W3:W3 + DECAY_LORA]
    ad = hs[:, W3 + DECAY_LORA:W3 + DECAY_LORA + ICLR_LORA]
    gd = hs[:, W3 + DECAY_LORA + ICLR_LORA:]
    z = w0 + _dot(jnp.tanh(wd).astype(BF16), wup)
    w = -_softplus(-z) - 0.5
    lw = -jnp.exp(w)
    a = _sigmoid(a0 + _dot(ad.astype(BF16), aup))
    g = _dot(_sigmoid(gd).astype(BF16), gup)
    return r, k, v, lw, a, g


def _head_sums(x, e_ref):
    hi, lo = _split2(x)
    e = e_ref[...]
    e_last = e[:LANE, :LANE]
    outs = []
    for c0 in range(0, RWKV_W, 2 * LANE):
        w = min(2 * LANE, RWKV_W - c0)
        ee = e if w == 2 * LANE else e_last
        outs.append(_dot(hi[:, c0:c0 + w], ee) + _dot(lo[:, c0:c0 + w], ee))
    return jnp.concatenate(outs, axis=1)


def _rwkv_scan_kernel(h_ref, hprev_ref, mu_ref, w0_ref, wup_ref, a0_ref, aup_ref, gup_ref,
                      kk_ref, ka_ref, rk_ref, gw_ref, gb_ref, e_ref, o_ref, st_ref, *, nchunk):
    C = RWKV_CHUNK
    N = RWKV_HEAD_DIM
    T = nchunk * C
    first = pl.program_id(1) == 0

    @pl.when(first)
    def _():
        st_ref[...] = jnp.zeros(st_ref.shape, F32)

    prev = jnp.where(first, 0.0, hprev_ref[7:8, :])
    r, k, v, lw, a, g = _rwkv_premix(h_ref[...], prev, mu_ref[...], w0_ref[...], wup_ref[...],
                                     a0_ref[...], aup_ref[...], gup_ref[...])

    ri = lax.broadcasted_iota(jnp.int32, (C, C), 0)
    ci = lax.broadcasted_iota(jnp.int32, (C, C), 1)
    incl = ci <= ri
    strict = ci < ri

    rt = lax.broadcasted_iota(jnp.int32, (T, T), 0)
    ct = lax.broadcasted_iota(jnp.int32, (T, T), 1)
    tri = ((ct <= rt) & (ct >= (rt // C) * C)).astype(BF16)
    lw_hi, lw_lo = _split2(lw)
    cum = _dot(tri, lw_hi) + _dot(tri, lw_lo)
    cls = [cum[(c + 1) * C - 1:(c + 1) * C, :] for c in range(nchunk)]
    cl_rows = jnp.concatenate([jnp.broadcast_to(t, (C, RWKV_W)) for t in cls], axis=0)
    e_in = jnp.exp(cum)
    e_ex = jnp.exp(cum - lw)
    e_neg = jnp.exp(-cum)
    e_tail = jnp.exp(cl_rows - cum)
    pcs = [jnp.exp(t) for t in cls]

    kk = k * kk_ref[...]
    kkn = kk / jnp.maximum(jnp.sqrt(_head_sums(kk * kk, e_ref)), 1e-12)
    k2 = k * (1.0 + (a - 1.0) * ka_ref[...])
    bvec = kkn * a
    At_f = -kkn * e_ex
    Rt_f = r * e_in
    AtB = At_f.astype(BF16)
    RtB = Rt_f.astype(BF16)
    BtB = (bvec * e_neg).astype(BF16)
    KtB = (k2 * e_neg).astype(BF16)
    BpB = (bvec * e_tail).astype(BF16)
    KpB = (k2 * e_tail).astype(BF16)
    vB = v.astype(BF16)

    HD = range(RWKV_HEADS)
    hsl = [slice(hd * N, (hd + 1) * N) for hd in HD]
    sls = [(slice(c * C, (c + 1) * C), hs) for c in range(nchunk) for hs in hsl]
    At = [At_f[rs, hs] for rs, hs in sls]
    Rt = [Rt_f[rs, hs] for rs, hs in sls]
    vb = [vB[rs, hs] for rs, hs in sls]
    Bt = [BtB[rs, hs] for rs, hs in sls]
    Kt = [KtB[rs, hs] for rs, hs in sls]
    Bp = [BpB[rs, hs] for rs, hs in sls]
    Kp = [KpB[rs, hs] for rs, hs in sls]

    AR = [jnp.concatenate([AtB[rs, hs], RtB[rs, hs]], axis=0) for rs, hs in sls]
    BK = [jnp.concatenate([x, y], axis=0) for x, y in zip(Bt, Kt)]
    G = [_dot_nt(x, y) for x, y in zip(AR, BK)]
    r2 = lax.broadcasted_iota(jnp.int32, (C, 2 * C), 0)
    c2 = lax.broadcasted_iota(jnp.int32, (C, 2 * C), 1)
    incl2 = jnp.where(c2 >= C, c2 - C, c2) <= r2
    P = [jnp.where(strict, t[:C, :C], 0.0) for t in G]
    Lak = [jnp.where(strict, t[:C, C:], 0.0).astype(BF16) for t in G]
    L2 = [jnp.where(incl2, t[C:], 0.0).astype(BF16) for t in G]

    X = [jnp.concatenate([x, _dot(y, z)], axis=1) for x, y, z in zip(At, Lak, vb)]
    steps = C.bit_length() - 1
    for it in range(steps):
        Pb = [t.astype(BF16) for t in P]
        Xb = [x.astype(BF16) for x in X]
        if it + 1 < steps:
            PX = [_dot(p, jnp.concatenate([x, p], axis=1)) for p, x in zip(Pb, Xb)]
            X = [x + t[:, :2 * N] for x, t in zip(X, PX)]
            P = [t[:, 2 * N:] for t in PX]
        else:
            X = [x + _dot(p, xb) for x, p, xb in zip(X, Pb, Xb)]
    Xb = [x.astype(BF16) for x in X]

    zpad = jnp.zeros((C, N), BF16)
    RHS = [jnp.concatenate([x, jnp.concatenate([zpad, z], axis=1)], axis=0)
           for x, z in zip(Xb, vb)]
    LX = [_dot(x, y) for x, y in zip(L2, RHS)]
    Qp = [(x + y[:, :N]).astype(BF16) for x, y in zip(Rt, LX)]
    Y0 = [y[:, N:] for y in LX]

    XB = [_dot_tn(x, b) for x, b in zip(Xb, Bp)]
    Mw = [t[:N].astype(BF16) for t in XB]
    Nn = [t[N:] + _dot_tn(x, y) for t, x, y in zip(XB, vb, Kp)]

    S = [st_ref[hd] for hd in HD]
    y_rows = []
    for c in range(nchunk):
        ch = [c * RWKV_HEADS + hd for hd in HD]
        Ssp = [_split2(s) for s in S]
        ys = [_dot_nt(Qp[j], sp[0]) + Y0[j] for j, sp in zip(ch, Ssp)]
        y_rows.append(jnp.concatenate(ys, axis=1))
        SM = [_dot(jnp.concatenate(sp, axis=0), Mw[j]) for sp, j in zip(Ssp, ch)]
        S = [s * pcs[c][:, hs] + sm[:N] + sm[N:] + Nn[j]
             for s, sm, j, hs in zip(S, SM, ch, hsl)]
    for hd in HD:
        st_ref[hd] = S[hd]

    y = jnp.concatenate(y_rows, axis=0)
    inv_n = 1.0 / N
    yc = y - _head_sums(y, e_ref) * inv_n
    var = _head_sums(yc * yc, e_ref) * inv_n
    yn = yc * lax.rsqrt(var + RWKV_GN_EPS) * gw_ref[...] + gb_ref[...]
    bonus = _head_sums(r * k2 * rk_ref[...], e_ref) * v
    o_ref[...] = ((yn + bonus) * g).astype(o_ref.dtype)


def _rwkv_mix(h, mu, w0, wup, a0, aup, gup, k_k, k_a, r_k, gn_w, gn_b, B, S):
    nchunk = RWKV_CHUNKS_PER_STEP
    T = nchunk * RWKV_CHUNK
    nc = S // T
    row = lambda b, c: (b * nc + c, 0)
    prev_row = lambda b, c: (jnp.maximum((b * nc + c) * (T // 8) - 1, 0), 0)
    const = lambda b, c: (0, 0)
    vec = pl.BlockSpec((1, RWKV_W), const)
    hid = jnp.arange(2 * LANE, dtype=jnp.int32) // RWKV_HEAD_DIM
    head_ones = (hid[:, None] == hid[None, :]).astype(BF16)
    return pl.pallas_call(
        functools.partial(_rwkv_scan_kernel, nchunk=nchunk),
        name="rwkv_scan",
        out_shape=jax.ShapeDtypeStruct((B * S, RWKV_W), BF16),
        grid=(B, nc),
        in_specs=[pl.BlockSpec((T, RWKV_COLS), row),
                  pl.BlockSpec((8, RWKV_COLS), prev_row),
                  pl.BlockSpec((1, RWKV_COLS), const),
                  vec,
                  pl.BlockSpec((DECAY_LORA, RWKV_W), const),
                  vec,
                  pl.BlockSpec((ICLR_LORA, RWKV_W), const),
                  pl.BlockSpec((GATE_LORA, RWKV_W), const),
                  vec, vec, vec, vec, vec,
                  pl.BlockSpec((2 * LANE, 2 * LANE), const)],
        out_specs=pl.BlockSpec((T, RWKV_W), row),
        scratch_shapes=[pltpu.VMEM((RWKV_HEADS, RWKV_HEAD_DIM, RWKV_HEAD_DIM), F32)],
        compiler_params=_cparams(("parallel", "arbitrary")),
    )(h, h, mu, w0, wup, a0, aup, gup, k_k, k_a, r_k, gn_w, gn_b, head_ones)


def _ret_kernel(h_ref, cos_ref, sin_ref, dmat_ref, qd_ref, kd_ref, cd_ref, gw_ref, gb_ref,
                o_ref, R_ref, *, nchunk):
    C = RET_CHUNK
    d = RET_HEAD_DIM
    H = RET_HEADS
    W = RET_W

    @pl.when(pl.program_id(1) == 0)
    def _():
        R_ref[...] = jnp.zeros(R_ref.shape, F32)

    pairs = [(c, hd) for c in range(nchunk) for hd in range(H)]
    rows = lambda c: slice(c * C, (c + 1) * C)
    cols = lambda part, hd: slice(part * W + hd * d, part * W + (hd + 1) * d)
    cos = [cos_ref[rows(c), :] for c in range(nchunk)]
    sin = [sin_ref[rows(c), :] for c in range(nchunk)]
    q = [h_ref[rows(c), cols(0, hd)] for c, hd in pairs]
    k = [h_ref[rows(c), cols(1, hd)] for c, hd in pairs]
    qr = [x * cos[c] + pltpu.roll(x, d // 2, 1) * sin[c] for x, (c, hd) in zip(q, pairs)]
    kr = [(x * cos[c] + pltpu.roll(x, d // 2, 1) * sin[c]) * (d ** -0.5)
          for x, (c, hd) in zip(k, pairs)]
    vb = [h_ref[rows(c), cols(2, hd)].astype(BF16) for c, hd in pairs]
    qb = [x.astype(BF16) for x in qr]
    kb = [x.astype(BF16) for x in kr]
    kdb = [(x * kd_ref[hd]).astype(BF16) for x, (c, hd) in zip(kr, pairs)]
    inner = [(_dot_nt(x, y) * dmat_ref[hd]).astype(BF16) for x, y, (c, hd) in zip(qb, kb, pairs)]
    o_in = [_dot(x, y) for x, y in zip(inner, vb)]
    kv = [_dot_tn(x, y) for x, y in zip(kdb, vb)]

    R = [R_ref[hd] for hd in range(H)]
    outs = []
    for c in range(nchunk):
        ch = [c * H + hd for hd in range(H)]
        Rsp = [_split2(x) for x in R]
        outs += [o_in[j] + (_dot(qb[j], sp[0]) + _dot(qb[j], sp[1])) * qd_ref[hd]
                 for hd, (j, sp) in enumerate(zip(ch, Rsp))]
        R = [cd_ref[hd][0:1, :] * x + kv[j] for hd, (x, j) in enumerate(zip(R, ch))]
    for hd in range(H):
        R_ref[hd] = R[hd]

    mus = [jnp.mean(o, axis=1, keepdims=True) for o in outs]
    ocs = [o - mu for o, mu in zip(outs, mus)]
    rstd = [lax.rsqrt(jnp.mean(oc * oc, axis=1, keepdims=True) + RET_GN_EPS) for oc in ocs]
    for oc, rs, (c, hd) in zip(ocs, rstd, pairs):
        gsl = slice(hd * d, (hd + 1) * d)
        on = oc * rs * gw_ref[:, gsl] + gb_ref[:, gsl]
        gate = h_ref[rows(c), cols(3, hd)]
        o_ref[rows(c), gsl] = (gate * _sigmoid(gate) * on).astype(o_ref.dtype)


def _ret_tables(S):
    C, d, H = RET_CHUNK, RET_HEAD_DIM, RET_HEADS
    half = d // 2
    inv = 1.0 / (ROPE_BASE ** (jnp.arange(half, dtype=F32) / half))
    ang = jnp.arange(S, dtype=F32)[:, None] * inv[None, :]
    cos = jnp.cos(ang)
    sin = jnp.sin(ang)
    cos2 = jnp.concatenate([cos, cos], axis=1)
    sin2 = jnp.concatenate([-sin, sin], axis=1)
    log_g = jnp.log(1.0 - 2.0 ** (-5.0 - jnp.arange(H, dtype=F32)))
    pos = jnp.arange(C, dtype=F32)
    rel = pos[:, None] - pos[None, :]
    dmat = jnp.where(rel >= 0, jnp.exp(log_g[:, None, None] * jnp.maximum(rel, 0.0)), 0.0)
    qd = jnp.broadcast_to(jnp.exp(log_g[:, None] * (pos + 1.0))[..., None], (H, C, d))
    kd = jnp.broadcast_to(jnp.exp(log_g[:, None] * (C - 1.0 - pos))[..., None], (H, C, d))
    cd = jnp.broadcast_to(jnp.exp(log_g * C)[:, None, None], (H, 8, d))
    return cos2, sin2, dmat, qd, kd, cd


def _retention(h, tables, gn_w, gn_b, B, S):
    cos2, sin2, dmat, qd, kd, cd = tables
    C = RET_CHUNK
    T = min(512, S)
    nt = S // T
    H = RET_HEADS
    tab = pl.BlockSpec((T, LANE), lambda b, t: (t, 0))
    per_head = lambda rows: pl.BlockSpec((H, rows, LANE), lambda b, t: (0, 0, 0))
    vec = pl.BlockSpec((1, RET_W), lambda b, t: (0, 0))
    return pl.pallas_call(
        functools.partial(_ret_kernel, nchunk=T // C),
        name="retention",
        out_shape=jax.ShapeDtypeStruct((B * S, RET_W), BF16),
        grid=(B, nt),
        in_specs=[pl.BlockSpec((T, RET_COLS), lambda b, t: (b * nt + t, 0)),
                  tab, tab, per_head(C), per_head(C), per_head(C), per_head(8), vec, vec],
        out_specs=pl.BlockSpec((T, RET_W), lambda b, t: (b * nt + t, 0)),
        scratch_shapes=[pltpu.VMEM((H, RET_HEAD_DIM, RET_HEAD_DIM), F32)],
        compiler_params=_cparams(("parallel", "arbitrary")),
    )(h, cos2, sin2, dmat, qd, kd, cd, gn_w, gn_b)


def _outproj_kernel(x_ref, yf_ref, yr_ref, yt_ref, w_ref, g_ref, b_ref, o_ref):
    r0, r1 = FOX_W, FOX_W + RWKV_W
    y = (_dot(yf_ref[...], w_ref[:r0, :]) + _dot(yr_ref[...], w_ref[r0:r1, :])
         + _dot(yt_ref[...], w_ref[r1:, :]))
    o_ref[...] = _layer_norm_rows(ALPHA * x_ref[...] + y, g_ref[...], b_ref[...])


def _out_proj_ln(x2d, y_fox, y_rwkv, y_ret, w_out, l, g, b):
    M, D = x2d.shape
    tm = min(512, M)
    row = lambda i: (i, 0)
    const = lambda i: (0, 0)
    return pl.pallas_call(
        _outproj_kernel,
        name="out_proj_ln",
        out_shape=jax.ShapeDtypeStruct((M, D), F32),
        grid=(M // tm,),
        in_specs=[pl.BlockSpec((tm, D), row), pl.BlockSpec((tm, FOX_W), row),
                  pl.BlockSpec((tm, RWKV_W), row), pl.BlockSpec((tm, RET_W), row),
                  pl.BlockSpec((None, D, D), lambda i: (l, 0, 0)),
                  pl.BlockSpec((1, D), const), pl.BlockSpec((1, D), const)],
        out_specs=pl.BlockSpec((tm, D), row),
        compiler_params=_cparams(("parallel",)),
    )(x2d, y_fox, y_rwkv, y_ret, w_out, g, b)


def _ffn_kernel(x_ref, wu_ref, wd_ref, g_ref, b_ref, o_ref, ob_ref, acc_ref, xb_ref):
    f = pl.program_id(1)

    @pl.when(f == 0)
    def _():
        x = x_ref[...]
        xb_ref[...] = x.astype(BF16)
        acc_ref[...] = ALPHA * x

    hid = jnp.maximum(_dot(xb_ref[...], wu_ref[...]), 0.0)
    acc_ref[...] += _dot((hid * hid).astype(BF16), wd_ref[...])

    @pl.when(f == pl.num_programs(1) - 1)
    def _():
        y = _layer_norm_rows(acc_ref[...], g_ref[...], b_ref[...])
        o_ref[...] = y
        ob_ref[...] = y.astype(BF16)


def _ffn_ln(x2d, w_up, w_down, l, g, b):
    M, D = x2d.shape
    F = w_up.shape[2]
    tm = min(512, M)
    tf = 1024
    return pl.pallas_call(
        _ffn_kernel,
        name="ffn_ln",
        out_shape=(jax.ShapeDtypeStruct((M, D), F32), jax.ShapeDtypeStruct((M, D), BF16)),
        grid=(M // tm, F // tf),
        in_specs=[pl.BlockSpec((tm, D), lambda i, f: (i, 0)),
                  pl.BlockSpec((None, D, tf), lambda i, f: (l, 0, f)),
                  pl.BlockSpec((None, tf, D), lambda i, f: (l, f, 0)),
                  pl.BlockSpec((1, D), lambda i, f: (0, 0)),
                  pl.BlockSpec((1, D), lambda i, f: (0, 0))],
        out_specs=(pl.BlockSpec((tm, D), lambda i, f: (i, 0)),
                   pl.BlockSpec((tm, D), lambda i, f: (i, 0))),
        scratch_shapes=[pltpu.VMEM((tm, D), F32), pltpu.VMEM((tm, D), BF16)],
        compiler_params=_cparams(("parallel", "arbitrary")),
    )(x2d, w_up, w_down, g, b)


def _permute_w_in(w_in):
    L, D, _ = w_in.shape
    w_in = w_in.astype(BF16)
    w_fox = w_in[:, :, :FOX_COLS]
    w_rwkv = w_in[:, :, FOX_COLS:FOX_COLS + RWKV_COLS]
    w_ret = w_in[:, :, FOX_COLS + RWKV_COLS:]
    w_ff = w_fox[:, :, 3 * FOX_W:]
    pad = jnp.zeros((L, D, LANE - FOX_HEADS), BF16)
    w_q = w_fox[:, :, :3 * FOX_W]
    w_rw = jnp.concatenate([w_rwkv, w_ff, pad], axis=-1)
    return w_q, w_rw, w_ret


def _layer(x2d, xb, l, p, tables, B, S):
    hq = _in_proj(xb, p['w_q'], l, p['q_scale'])
    h = _in_proj(xb, p['w_rw'], l)
    ht = _in_proj(xb, p['w_ret'], l)
    c = _fox_c(h, p['fox_bias'], B, S)
    c3 = c[:, :FOX_HEADS, :].reshape(B * FOX_HEADS, 1, S)
    y_fox = _fox_attention(hq, c3, B, S)
    y_rwkv = _rwkv_mix(h, p['mu'], p['w0'], p['w_up_lora'], p['a0'], p['a_up'], p['g_up'],
                       p['k_k'], p['k_a'], p['r_k'], p['rgn_w'], p['rgn_b'], B, S)
    y_ret = _retention(ht, tables, p['tgn_w'], p['tgn_b'], B, S)
    x2d = _out_proj_ln(x2d, y_fox, y_rwkv, y_ret, p['w_out'], l, p['ln1_g'], p['ln1_b'])
    return _ffn_ln(x2d, p['w_up'], p['w_down'], l, p['ln2_g'], p['ln2_b'])


def kernel(x, w_in, fox_forget_bias, rwkv_mu, rwkv_w0, rwkv_w_up, rwkv_a0, rwkv_a_up, rwkv_g_up,
           rwkv_k_k, rwkv_k_a, rwkv_r_k, rwkv_gn_w, rwkv_gn_b, ret_gn_w, ret_gn_b, w_out, ln1_g,
           ln1_b, w_up, w_down, ln2_g, ln2_b):
    B, S, D = x.shape
    L = w_in.shape[0]
    w_q, w_rw, w_ret = _permute_w_in(w_in)
    q_scale = jnp.concatenate([jnp.full((1, FOX_W), FOX_Q_SCALE, F32),
                               jnp.ones((1, 2 * FOX_W), F32)], axis=1)
    w_out_b = w_out.astype(BF16)
    w_up_b = w_up.astype(BF16)
    w_down_b = w_down.astype(BF16)
    bias_pad = jnp.pad(fox_forget_bias, ((0, 0), (0, LANE - FOX_HEADS)))
    tables = _ret_tables(S)
    row = lambda t: t.reshape(1, -1)
    x2d = x.reshape(B * S, D)
    xb = x2d
    for l in range(L):
        p = {
            'w_rw': w_rw, 'w_ret': w_ret, 'w_q': w_q, 'q_scale': q_scale,
            'fox_bias': row(bias_pad[l]),
            'mu': row(rwkv_mu[l]), 'w0': row(rwkv_w0[l]), 'w_up_lora': rwkv_w_up[l].astype(BF16),
            'a0': row(rwkv_a0[l]), 'a_up': rwkv_a_up[l].astype(BF16),
            'g_up': rwkv_g_up[l].astype(BF16),
            'k_k': row(rwkv_k_k[l]), 'k_a': row(rwkv_k_a[l]), 'r_k': row(rwkv_r_k[l]),
            'rgn_w': row(rwkv_gn_w[l]), 'rgn_b': row(rwkv_gn_b[l]),
            'tgn_w': row(ret_gn_w[l]), 'tgn_b': row(ret_gn_b[l]),
            'w_out': w_out_b,
            'ln1_g': row(ln1_g[l]), 'ln1_b': row(ln1_b[l]),
            'w_up': w_up_b, 'w_down': w_down_b,
            'ln2_g': row(ln2_g[l]), 'ln2_b': row(ln2_b[l]),
        }
        x2d, xb = _layer(x2d, xb, l, p, tables, B, S)
    return x2d.reshape(B, S, D)
```

```python
import functools

import jax
import jax.numpy as jnp
from jax import lax
from jax.experimental import pallas as pl
from jax.experimental.pallas import tpu as pltpu

F32 = jnp.float32
BF16 = jnp.bfloat16

D_MODEL = 2048
DEPTH = 4
FOX_W = 768
RWKV_W = 640
RET_W = 640
FOX_HEAD_DIM = 128
FOX_HEADS = 6
RWKV_HEAD_DIM = 64
RWKV_HEADS = 10
RET_HEAD_DIM = 128
RET_HEADS = 5
DECAY_LORA = 64
ICLR_LORA = 64
GATE_LORA = 128
D_FF = 4 * D_MODEL
RET_CHUNK = 128
ROPE_BASE = 10000.0
LN_EPS = 1e-5
RWKV_GN_EPS = 64e-5
RET_GN_EPS = 1e-5
ALPHA = (2 * DEPTH) ** 0.25

FOX_COLS = 3 * FOX_W + FOX_HEADS
RWKV_COLS = 3 * RWKV_W + DECAY_LORA + ICLR_LORA + GATE_LORA
RET_COLS = 4 * RET_W

LANE = 128
H_FF0 = RWKV_COLS
LOG2E = 1.4426950408889634
FOX_Q_SCALE = FOX_HEAD_DIM ** -0.5 * LOG2E

FOX_BLOCKS_PER_TRIP = 4
FOX_Q_ROWS = 1024
RWKV_CHUNK = 64
RWKV_CHUNKS_PER_STEP = 4
NEG_BIG = -1e30

VMEM_LIMIT = 56 * 1024 * 1024


def _cparams(sem):
    return pltpu.CompilerParams(dimension_semantics=sem, vmem_limit_bytes=VMEM_LIMIT)


def _dot(a, b):
    return jnp.dot(a, b, preferred_element_type=F32)


def _dot_nt(a, b):
    return lax.dot_general(a, b, (((1,), (1,)), ((), ())), preferred_element_type=F32)


def _dot_tn(a, b):
    return lax.dot_general(a, b, (((0,), (0,)), ((), ())), preferred_element_type=F32)


def _split3(x):
    hi = x.astype(BF16)
    r1 = x - hi.astype(F32)
    mid = r1.astype(BF16)
    lo = (r1 - mid.astype(F32)).astype(BF16)
    return hi, mid, lo


def _split2(x):
    hi = x.astype(BF16)
    lo = (x - hi.astype(F32)).astype(BF16)
    return hi, lo


def _mask_dot(mask_bf16, x):
    hi, mid, lo = _split3(x)
    return _dot(mask_bf16, hi) + _dot(mask_bf16, mid) + _dot(mask_bf16, lo)


def _softplus(z):
    return jnp.maximum(z, 0.0) + jnp.log1p(jnp.exp(-jnp.abs(z)))


def _sigmoid(z):
    return 1.0 / (1.0 + jnp.exp(-z))


def _layer_norm_rows(y, g, b):
    mu = jnp.mean(y, axis=-1, keepdims=True)
    yc = y - mu
    var = jnp.mean(yc * yc, axis=-1, keepdims=True)
    return yc * lax.rsqrt(var + LN_EPS) * g + b


def _inproj_kernel(x_ref, w_ref, o_ref):
    o_ref[...] = _dot(x_ref[...].astype(BF16), w_ref[...])


def _inproj_scaled_kernel(x_ref, w_ref, s_ref, o_ref):
    o_ref[...] = (_dot(x_ref[...].astype(BF16), w_ref[...]) * s_ref[...]).astype(o_ref.dtype)


def _in_proj(xb, w, l, col_scale=None):
    M, K = xb.shape
    N = w.shape[2]
    tm = min(1024 if xb.dtype == BF16 else 512, M)
    tn = N
    in_specs = [pl.BlockSpec((tm, K), lambda j, i: (i, 0)),
                pl.BlockSpec((None, K, tn), lambda j, i: (l, 0, j))]
    args = [xb, w]
    if col_scale is not None:
        in_specs.append(pl.BlockSpec((1, tn), lambda j, i: (0, j)))
        args.append(col_scale)
    return pl.pallas_call(
        _inproj_kernel if col_scale is None else _inproj_scaled_kernel,
        name="in_proj",
        out_shape=jax.ShapeDtypeStruct((M, N), F32 if col_scale is None else BF16),
        grid=(N // tn, M // tm),
        in_specs=in_specs,
        out_specs=pl.BlockSpec((tm, tn), lambda j, i: (i, j)),
        compiler_params=_cparams(("parallel", "parallel")),
    )(*args)


def _fox_c_kernel(f_ref, bias_ref, c_ref, *, rows):
    S = f_ref.shape[0]
    r = lax.broadcasted_iota(jnp.int32, (rows, rows), 0)
    c = lax.broadcasted_iota(jnp.int32, (rows, rows), 1)
    tri = (c <= r).astype(BF16)

    bias = bias_ref[...]
    local = [_mask_dot(tri, -_softplus(-(f_ref[s:s + rows, :] + bias))) for s in range(0, S, rows)]
    carry = jnp.zeros((1, LANE), F32)
    for j, cs in enumerate(local):
        c_ref[:, j * rows:(j + 1) * rows] = ((cs + carry) * LOG2E).T[0:8, :]
        carry = carry + cs[rows - 1:rows, :]


def _fox_c(h, bias_pad, B, S):
    rows = min(256, S)
    return pl.pallas_call(
        functools.partial(_fox_c_kernel, rows=rows),
        name="fox_c",
        out_shape=jax.ShapeDtypeStruct((B, 8, S), F32),
        grid=(B,),
        in_specs=[pl.BlockSpec((S, LANE), lambda b: (b, H_FF0 // LANE)),
                  pl.BlockSpec((1, LANE), lambda b: (0, 0))],
        out_specs=pl.BlockSpec((None, 8, S), lambda b: (b, 0, 0)),
        compiler_params=_cparams(("parallel",)),
    )(h, bias_pad)


def _fox_attn_kernel(q_ref, qn_ref, k_ref, v_ref, c_ref, o_ref, s0_ref, *, tq, tk):
    i = pl.program_id(2)
    d = q_ref.shape[1]
    q0 = pl.multiple_of(i * tq, tq)
    c0 = c_ref[:, pl.ds(q0, tq)][:, 0:1]
    q = q_ref[...]

    def scores(start, row_lo=0):
        bias = c0 - c_ref[:, pl.ds(start, tk)]
        return _dot_nt(q[row_lo:], k_ref[pl.ds(start, tk), :]) + bias

    def next_step_scores():
        qn0 = pl.multiple_of(jnp.minimum(i + 1, pl.num_programs(2) - 1) * tq, tq)
        cn = c_ref[:, pl.ds(qn0, tq)][:, 0:1]
        return _dot_nt(qn_ref[...], k_ref[0:tk, :]) + (cn - c_ref[:, 0:tk])

    ones = jnp.ones((tk, d), BF16)

    def update(start, sc, carry, masked, row_lo=0):
        m_all, acc_all = carry
        m, acc = m_all[row_lo:], acc_all[row_lo:]
        if masked:
            row = lax.broadcasted_iota(jnp.int32, sc.shape, 0) + (q0 + row_lo)
            col = lax.broadcasted_iota(jnp.int32, sc.shape, 1) + start
            sc = jnp.where(col <= row, sc, NEG_BIG)
        m_new = jnp.maximum(m, jnp.max(sc, axis=1, keepdims=True))
        alpha = jnp.exp2(m - m_new)
        p = jnp.exp2((sc - m_new).astype(BF16))
        vb = jnp.concatenate([v_ref[pl.ds(start, tk), :], ones], axis=1)
        acc = alpha * acc + _dot(p, vb)
        if row_lo:
            m_new = jnp.concatenate([m_all[:row_lo], m_new], axis=0)
            acc = jnp.concatenate([acc_all[:row_lo], acc], axis=0)
        return m_new, acc

    U = FOX_BLOCKS_PER_TRIP

    D = tq // tk

    def walk(base, n, carry, last):
        skip = lambda u: max(u - (n - D), 0) * tk if last else 0
        s_cur = s0_ref[...]
        for u in range(n):
            start = pl.multiple_of(base + u * tk, tk)
            final = last and u == n - 1
            s_next = (next_step_scores() if final
                      else scores(pl.multiple_of(start + tk, tk), skip(u + 1)))
            carry = update(start, s_cur, carry, last and u >= n - D, skip(u))
            s_cur = s_next
        s0_ref[...] = s_cur
        return carry

    @pl.when(i == 0)
    def _():
        s0_ref[...] = scores(0)

    init = (jnp.full((tq, 1), NEG_BIG, F32), jnp.zeros((tq, 2 * d), F32))
    n_full = i * D
    trips = n_full // U
    carry = lax.fori_loop(
        0, trips, lambda t, c: walk(pl.multiple_of(t * (U * tk), tk), U, c, False), init)

    base = pl.multiple_of(trips * (U * tk), tk)
    rems = list(range(0, U, D))
    tails = [functools.partial(walk, base, r + D, last=True) for r in rems]
    m, acc = lax.switch((n_full - trips * U) // D, tails, carry)
    o_ref[...] = (acc[:, :d] / acc[:, d:]).astype(o_ref.dtype)


def _fox_attention(hq, c3, B, S):
    tk = min(512, S)
    tq = min(FOX_Q_ROWS, S)
    nq = S // tq
    nh = FOX_HEADS
    return pl.pallas_call(
        functools.partial(_fox_attn_kernel, tq=tq, tk=tk),
        name="fox_attn",
        scratch_shapes=[pltpu.VMEM((tq, tk), F32)],
        out_shape=jax.ShapeDtypeStruct((B * S, FOX_W), BF16),
        grid=(B, FOX_HEADS, nq),
        in_specs=[pl.BlockSpec((tq, LANE), lambda b, hd, i: (b * nq + i, hd)),
                  pl.BlockSpec((tq, LANE),
                               lambda b, hd, i: (b * nq + jnp.minimum(i + 1, nq - 1), hd)),
                  pl.BlockSpec((S, LANE), lambda b, hd, i: (b, nh + hd)),
                  pl.BlockSpec((S, LANE), lambda b, hd, i: (b, 2 * nh + hd)),
                  pl.BlockSpec((None, 1, S), lambda b, hd, i: (b * nh + hd, 0, 0))],
        out_specs=pl.BlockSpec((tq, LANE), lambda b, hd, i: (b * nq + i, hd)),
        compiler_params=_cparams(("parallel", "parallel", "arbitrary")),
    )(hq, hq, hq, hq, c3)


def _rwkv_premix(h, prev, mu, w0, wup, a0, aup, gup):
    rows = lax.broadcasted_iota(jnp.int32, h.shape, 0)
    hp = jnp.where(rows == 0, prev, pltpu.roll(h, 1, 0))
    hs = h + (hp - h) * mu
    W3 = 3 * RWKV_W
    r = hs[:, :RWKV_W]
    k = hs[:, RWKV_W:2 * RWKV_W]
    v = hs[:, 2 * RWKV_W:W3]
    wd = hs[:, W3:W3 + DECAY_LORA]
    ad = hs[:, W3 + DECAY_LORA:W3 + DECAY_LORA + ICLR_LORA]
    gd = hs[:, W3 + DECAY_LORA + ICLR_LORA:]
    z = w0 + _dot(jnp.tanh(wd).astype(BF16), wup)
    w = -_softplus(-z) - 0.5
    lw = -jnp.exp(w)
    a = _sigmoid(a0 + _dot(ad.astype(BF16), aup))
    g = _dot(_sigmoid(gd).astype(BF16), gup)
    return r, k, v, lw, a, g


def _head_sums(x, e_ref):
    hi, lo = _split2(x)
    e = e_ref[...]
    e_last = e[:LANE, :LANE]
    outs = []
    for c0 in range(0, RWKV_W, 2 * LANE):
        w = min(2 * LANE, RWKV_W - c0)
        ee = e if w == 2 * LANE else e_last
        outs.append(_dot(hi[:, c0:c0 + w], ee) + _dot(lo[:, c0:c0 + w], ee))
    return jnp.concatenate(outs, axis=1)


def _rwkv_scan_kernel(h_ref, hprev_ref, mu_ref, w0_ref, wup_ref, a0_ref, aup_ref, gup_ref,
                      kk_ref, ka_ref, rk_ref, gw_ref, gb_ref, e_ref, o_ref, st_ref, *, nchunk):
    C = RWKV_CHUNK
    N = RWKV_HEAD_DIM
    T = nchunk * C
    first = pl.program_id(1) == 0

    @pl.when(first)
    def _():
        st_ref[...] = jnp.zeros(st_ref.shape, F32)

    prev = jnp.where(first, 0.0, hprev_ref[7:8, :])
    r, k, v, lw, a, g = _rwkv_premix(h_ref[...], prev, mu_ref[...], w0_ref[...], wup_ref[...],
                                     a0_ref[...], aup_ref[...], gup_ref[...])

    ri = lax.broadcasted_iota(jnp.int32, (C, C), 0)
    ci = lax.broadcasted_iota(jnp.int32, (C, C), 1)
    incl = ci <= ri
    strict = ci < ri

    rt = lax.broadcasted_iota(jnp.int32, (T, T), 0)
    ct = lax.broadcasted_iota(jnp.int32, (T, T), 1)
    tri = ((ct <= rt) & (ct >= (rt // C) * C)).astype(BF16)
    lw_hi, lw_lo = _split2(lw)
    cum = _dot(tri, lw_hi) + _dot(tri, lw_lo)
    cls = [cum[(c + 1) * C - 1:(c + 1) * C, :] for c in range(nchunk)]
    cl_rows = jnp.concatenate([jnp.broadcast_to(t, (C, RWKV_W)) for t in cls], axis=0)
    e_in = jnp.exp(cum)
    e_ex = jnp.exp(cum - lw)
    e_neg = jnp.exp(-cum)
    e_tail = jnp.exp(cl_rows - cum)
    pcs = [jnp.exp(t) for t in cls]

    kk = k * kk_ref[...]
    kkn = kk / jnp.maximum(jnp.sqrt(_head_sums(kk * kk, e_ref)), 1e-12)
    k2 = k * (1.0 + (a - 1.0) * ka_ref[...])
    bvec = kkn * a
    At_f = -kkn * e_ex
    Rt_f = r * e_in
    AtB = At_f.astype(BF16)
    RtB = Rt_f.astype(BF16)
    BtB = (bvec * e_neg).astype(BF16)
    KtB = (k2 * e_neg).astype(BF16)
    BpB = (bvec * e_tail).astype(BF16)
    KpB = (k2 * e_tail).astype(BF16)
    vB = v.astype(BF16)

    HD = range(RWKV_HEADS)
    hsl = [slice(hd * N, (hd + 1) * N) for hd in HD]
    sls = [(slice(c * C, (c + 1) * C), hs) for c in range(nchunk) for hs in hsl]
    At = [At_f[rs, hs] for rs, hs in sls]
    Rt = [Rt_f[rs, hs] for rs, hs in sls]
    vb = [vB[rs, hs] for rs, hs in sls]
    Bt = [BtB[rs, hs] for rs, hs in sls]
    Kt = [KtB[rs, hs] for rs, hs in sls]
    Bp = [BpB[rs, hs] for rs, hs in sls]
    Kp = [KpB[rs, hs] for rs, hs in sls]

    AR = [jnp.concatenate([AtB[rs, hs], RtB[rs, hs]], axis=0) for rs, hs in sls]
    BK = [jnp.concatenate([x, y], axis=0) for x, y in zip(Bt, Kt)]
    G = [_dot_nt(x, y) for x, y in zip(AR, BK)]
    r2 = lax.broadcasted_iota(jnp.int32, (C, 2 * C), 0)
    c2 = lax.broadcasted_iota(jnp.int32, (C, 2 * C), 1)
    incl2 = jnp.where(c2 >= C, c2 - C, c2) <= r2
    P = [jnp.where(strict, t[:C, :C], 0.0) for t in G]
    Lak = [jnp.where(strict, t[:C, C:], 0.0).astype(BF16) for t in G]
    L2 = [jnp.where(incl2, t[C:], 0.0).astype(BF16) for t in G]

    X = [jnp.concatenate([x, _dot(y, z)], axis=1) for x, y, z in zip(At, Lak, vb)]
    steps = C.bit_length() - 1
    for it in range(steps):
        Pb = [t.astype(BF16) for t in P]
        Xb = [x.astype(BF16) for x in X]
        if it + 1 < steps:
            PX = [_dot(p, jnp.concatenate([x, p], axis=1)) for p, x in zip(Pb, Xb)]
            X = [x + t[:, :2 * N] for x, t in zip(X, PX)]
            P = [t[:, 2 * N:] for t in PX]
        else:
            X = [x + _dot(p, xb) for x, p, xb in zip(X, Pb, Xb)]
    Xb = [x.astype(BF16) for x in X]

    zpad = jnp.zeros((C, N), BF16)
    RHS = [jnp.concatenate([x, jnp.concatenate([zpad, z], axis=1)], axis=0)
           for x, z in zip(Xb, vb)]
    LX = [_dot(x, y) for x, y in zip(L2, RHS)]
    Qp = [(x + y[:, :N]).astype(BF16) for x, y in zip(Rt, LX)]
    Y0 = [y[:, N:] for y in LX]

    XB = [_dot_tn(x, b) for x, b in zip(Xb, Bp)]
    Mw = [t[:N].astype(BF16) for t in XB]
    Nn = [t[N:] + _dot_tn(x, y) for t, x, y in zip(XB, vb, Kp)]

    S = [st_ref[hd] for hd in HD]
    y_rows = []
    for c in range(nchunk):
        ch = [c * RWKV_HEADS + hd for hd in HD]
        Ssp = [_split2(s) for s in S]
        ys = [_dot_nt(Qp[j], sp[0]) + Y0[j] for j, sp in zip(ch, Ssp)]
        y_rows.append(jnp.concatenate(ys, axis=1))
        SM = [_dot(jnp.concatenate(sp, axis=0), Mw[j]) for sp, j in zip(Ssp, ch)]
        S = [s * pcs[c][:, hs] + sm[:N] + sm[N:] + Nn[j]
             for s, sm, j, hs in zip(S, SM, ch, hsl)]
    for hd in HD:
        st_ref[hd] = S[hd]

    y = jnp.concatenate(y_rows, axis=0)
    inv_n = 1.0 / N
    yc = y - _head_sums(y, e_ref) * inv_n
    var = _head_sums(yc * yc, e_ref) * inv_n
    yn = yc * lax.rsqrt(var + RWKV_GN_EPS) * gw_ref[...] + gb_ref[...]
    bonus = _head_sums(r * k2 * rk_ref[...], e_ref) * v
    o_ref[...] = ((yn + bonus) * g).astype(o_ref.dtype)


def _rwkv_mix(h, mu, w0, wup, a0, aup, gup, k_k, k_a, r_k, gn_w, gn_b, B, S):
    nchunk = RWKV_CHUNKS_PER_STEP
    T = nchunk * RWKV_CHUNK
    nc = S // T
    row = lambda b, c: (b * nc + c, 0)
    prev_row = lambda b, c: (jnp.maximum((b * nc + c) * (T // 8) - 1, 0), 0)
    const = lambda b, c: (0, 0)
    vec = pl.BlockSpec((1, RWKV_W), const)
    hid = jnp.arange(2 * LANE, dtype=jnp.int32) // RWKV_HEAD_DIM
    head_ones = (hid[:, None] == hid[None, :]).astype(BF16)
    return pl.pallas_call(
        functools.partial(_rwkv_scan_kernel, nchunk=nchunk),
        name="rwkv_scan",
        out_shape=jax.ShapeDtypeStruct((B * S, RWKV_W), BF16),
        grid=(B, nc),
        in_specs=[pl.BlockSpec((T, RWKV_COLS), row),
                  pl.BlockSpec((8, RWKV_COLS), prev_row),
                  pl.BlockSpec((1, RWKV_COLS), const),
                  vec,
                  pl.BlockSpec((DECAY_LORA, RWKV_W), const),
                  vec,
                  pl.BlockSpec((ICLR_LORA, RWKV_W), const),
                  pl.BlockSpec((GATE_LORA, RWKV_W), const),
                  vec, vec, vec, vec, vec,
                  pl.BlockSpec((2 * LANE, 2 * LANE), const)],
        out_specs=pl.BlockSpec((T, RWKV_W), row),
        scratch_shapes=[pltpu.VMEM((RWKV_HEADS, RWKV_HEAD_DIM, RWKV_HEAD_DIM), F32)],
        compiler_params=_cparams(("parallel", "arbitrary")),
    )(h, h, mu, w0, wup, a0, aup, gup, k_k, k_a, r_k, gn_w, gn_b, head_ones)


def _ret_kernel(h_ref, cos_ref, sin_ref, dmat_ref, qd_ref, kd_ref, cd_ref, gw_ref, gb_ref,
                o_ref, R_ref, *, nchunk):
    C = RET_CHUNK
    d = RET_HEAD_DIM
    H = RET_HEADS
    W = RET_W

    @pl.when(pl.program_id(1) == 0)
    def _():
        R_ref[...] = jnp.zeros(R_ref.shape, F32)

    pairs = [(c, hd) for c in range(nchunk) for hd in range(H)]
    rows = lambda c: slice(c * C, (c + 1) * C)
    cols = lambda part, hd: slice(part * W + hd * d, part * W + (hd + 1) * d)
    cos = [cos_ref[rows(c), :] for c in range(nchunk)]
    sin = [sin_ref[rows(c), :] for c in range(nchunk)]
    q = [h_ref[rows(c), cols(0, hd)] for c, hd in pairs]
    k = [h_ref[rows(c), cols(1, hd)] for c, hd in pairs]
    qr = [x * cos[c] + pltpu.roll(x, d // 2, 1) * sin[c] for x, (c, hd) in zip(q, pairs)]
    kr = [(x * cos[c] + pltpu.roll(x, d // 2, 1) * sin[c]) * (d ** -0.5)
          for x, (c, hd) in zip(k, pairs)]
    vb = [h_ref[rows(c), cols(2, hd)].astype(BF16) for c, hd in pairs]
    qb = [x.astype(BF16) for x in qr]
    kb = [x.astype(BF16) for x in kr]
    kdb = [(x * kd_ref[hd]).astype(BF16) for x, (c, hd) in zip(kr, pairs)]
    inner = [(_dot_nt(x, y) * dmat_ref[hd]).astype(BF16) for x, y, (c, hd) in zip(qb, kb, pairs)]
    o_in = [_dot(x, y) for x, y in zip(inner, vb)]
    kv = [_dot_tn(x, y) for x, y in zip(kdb, vb)]

    R = [R_ref[hd] for hd in range(H)]
    outs = []
    for c in range(nchunk):
        ch = [c * H + hd for hd in range(H)]
        Rsp = [_split2(x) for x in R]
        outs += [o_in[j] + (_dot(qb[j], sp[0]) + _dot(qb[j], sp[1])) * qd_ref[hd]
                 for hd, (j, sp) in enumerate(zip(ch, Rsp))]
        R = [cd_ref[hd][0:1, :] * x + kv[j] for hd, (x, j) in enumerate(zip(R, ch))]
    for hd in range(H):
        R_ref[hd] = R[hd]

    mus = [jnp.mean(o, axis=1, keepdims=True) for o in outs]
    ocs = [o - mu for o, mu in zip(outs, mus)]
    rstd = [lax.rsqrt(jnp.mean(oc * oc, axis=1, keepdims=True) + RET_GN_EPS) for oc in ocs]
    for oc, rs, (c, hd) in zip(ocs, rstd, pairs):
        gsl = slice(hd * d, (hd + 1) * d)
        on = oc * rs * gw_ref[:, gsl] + gb_ref[:, gsl]
        gate = h_ref[rows(c), cols(3, hd)]
        o_ref[rows(c), gsl] = (gate * _sigmoid(gate) * on).astype(o_ref.dtype)


def _ret_tables(S):
    C, d, H = RET_CHUNK, RET_HEAD_DIM, RET_HEADS
    half = d // 2
    inv = 1.0 / (ROPE_BASE ** (jnp.arange(half, dtype=F32) / half))
    ang = jnp.arange(S, dtype=F32)[:, None] * inv[None, :]
    cos = jnp.cos(ang)
    sin = jnp.sin(ang)
    cos2 = jnp.concatenate([cos, cos], axis=1)
    sin2 = jnp.concatenate([-sin, sin], axis=1)
    log_g = jnp.log(1.0 - 2.0 ** (-5.0 - jnp.arange(H, dtype=F32)))
    pos = jnp.arange(C, dtype=F32)
    rel = pos[:, None] - pos[None, :]
    dmat = jnp.where(rel >= 0, jnp.exp(log_g[:, None, None] * jnp.maximum(rel, 0.0)), 0.0)
    qd = jnp.broadcast_to(jnp.exp(log_g[:, None] * (pos + 1.0))[..., None], (H, C, d))
    kd = jnp.broadcast_to(jnp.exp(log_g[:, None] * (C - 1.0 - pos))[..., None], (H, C, d))
    cd = jnp.broadcast_to(jnp.exp(log_g * C)[:, None, None], (H, 8, d))
    return cos2, sin2, dmat, qd, kd, cd


def _retention(h, tables, gn_w, gn_b, B, S):
    cos2, sin2, dmat, qd, kd, cd = tables
    C = RET_CHUNK
    T = min(512, S)
    nt = S // T
    H = RET_HEADS
    tab = pl.BlockSpec((T, LANE), lambda b, t: (t, 0))
    per_head = lambda rows: pl.BlockSpec((H, rows, LANE), lambda b, t: (0, 0, 0))
    vec = pl.BlockSpec((1, RET_W), lambda b, t: (0, 0))
    return pl.pallas_call(
        functools.partial(_ret_kernel, nchunk=T // C),
        name="retention",
        out_shape=jax.ShapeDtypeStruct((B * S, RET_W), BF16),
        grid=(B, nt),
        in_specs=[pl.BlockSpec((T, RET_COLS), lambda b, t: (b * nt + t, 0)),
                  tab, tab, per_head(C), per_head(C), per_head(C), per_head(8), vec, vec],
        out_specs=pl.BlockSpec((T, RET_W), lambda b, t: (b * nt + t, 0)),
        scratch_shapes=[pltpu.VMEM((H, RET_HEAD_DIM, RET_HEAD_DIM), F32)],
        compiler_params=_cparams(("parallel", "arbitrary")),
    )(h, cos2, sin2, dmat, qd, kd, cd, gn_w, gn_b)


def _outproj_kernel(x_ref, yf_ref, yr_ref, yt_ref, w_ref, g_ref, b_ref, o_ref):
    r0, r1 = FOX_W, FOX_W + RWKV_W
    y = (_dot(yf_ref[...], w_ref[:r0, :]) + _dot(yr_ref[...], w_ref[r0:r1, :])
         + _dot(yt_ref[...], w_ref[r1:, :]))
    o_ref[...] = _layer_norm_rows(ALPHA * x_ref[...] + y, g_ref[...], b_ref[...])


def _out_proj_ln(x2d, y_fox, y_rwkv, y_ret, w_out, l, g, b):
    M, D = x2d.shape
    tm = min(512, M)
    row = lambda i: (i, 0)
    const = lambda i: (0, 0)
    return pl.pallas_call(
        _outproj_kernel,
        name="out_proj_ln",
        out_shape=jax.ShapeDtypeStruct((M, D), F32),
        grid=(M // tm,),
        in_specs=[pl.BlockSpec((tm, D), row), pl.BlockSpec((tm, FOX_W), row),
                  pl.BlockSpec((tm, RWKV_W), row), pl.BlockSpec((tm, RET_W), row),
                  pl.BlockSpec((None, D, D), lambda i: (l, 0, 0)),
                  pl.BlockSpec((1, D), const), pl.BlockSpec((1, D), const)],
        out_specs=pl.BlockSpec((tm, D), row),
        compiler_params=_cparams(("parallel",)),
    )(x2d, y_fox, y_rwkv, y_ret, w_out, g, b)


def _ffn_kernel(x_ref, wu_ref, wd_ref, g_ref, b_ref, o_ref, ob_ref, acc_ref, xb_ref):
    f = pl.program_id(1)

    @pl.when(f == 0)
    def _():
        x = x_ref[...]
        xb_ref[...] = x.astype(BF16)
        acc_ref[...] = ALPHA * x

    hid = jnp.maximum(_dot(xb_ref[...], wu_ref[...]), 0.0)
    acc_ref[...] += _dot((hid * hid).astype(BF16), wd_ref[...])

    @pl.when(f == pl.num_programs(1) - 1)
    def _():
        y = _layer_norm_rows(acc_ref[...], g_ref[...], b_ref[...])
        o_ref[...] = y
        ob_ref[...] = y.astype(BF16)


def _ffn_ln(x2d, w_up, w_down, l, g, b):
    M, D = x2d.shape
    F = w_up.shape[2]
    tm = min(512, M)
    tf = 1024
    return pl.pallas_call(
        _ffn_kernel,
        name="ffn_ln",
        out_shape=(jax.ShapeDtypeStruct((M, D), F32), jax.ShapeDtypeStruct((M, D), BF16)),
        grid=(M // tm, F // tf),
        in_specs=[pl.BlockSpec((tm, D), lambda i, f: (i, 0)),
                  pl.BlockSpec((None, D, tf), lambda i, f: (l, 0, f)),
                  pl.BlockSpec((None, tf, D), lambda i, f: (l, f, 0)),
                  pl.BlockSpec((1, D), lambda i, f: (0, 0)),
                  pl.BlockSpec((1, D), lambda i, f: (0, 0))],
        out_specs=(pl.BlockSpec((tm, D), lambda i, f: (i, 0)),
                   pl.BlockSpec((tm, D), lambda i, f: (i, 0))),
        scratch_shapes=[pltpu.VMEM((tm, D), F32), pltpu.VMEM((tm, D), BF16)],
        compiler_params=_cparams(("parallel", "arbitrary")),
    )(x2d, w_up, w_down, g, b)


def _permute_w_in(w_in):
    L, D, _ = w_in.shape
    w_in = w_in.astype(BF16)
    w_fox = w_in[:, :, :FOX_COLS]
    w_rwkv = w_in[:, :, FOX_COLS:FOX_COLS + RWKV_COLS]
    w_ret = w_in[:, :, FOX_COLS + RWKV_COLS:]
    w_ff = w_fox[:, :, 3 * FOX_W:]
    pad = jnp.zeros((L, D, LANE - FOX_HEADS), BF16)
    w_q = w_fox[:, :, :3 * FOX_W]
    w_rw = jnp.concatenate([w_rwkv, w_ff, pad], axis=-1)
    return w_q, w_rw, w_ret


def _layer(x2d, xb, l, p, tables, B, S):
    hq = _in_proj(xb, p['w_q'], l, p['q_scale'])
    h = _in_proj(xb, p['w_rw'], l)
    ht = _in_proj(xb, p['w_ret'], l)
    c = _fox_c(h, p['fox_bias'], B, S)
    c3 = c[:, :FOX_HEADS, :].reshape(B * FOX_HEADS, 1, S)
    y_fox = _fox_attention(hq, c3, B, S)
    y_rwkv = _rwkv_mix(h, p['mu'], p['w0'], p['w_up_lora'], p['a0'], p['a_up'], p['g_up'],
                       p['k_k'], p['k_a'], p['r_k'], p['rgn_w'], p['rgn_b'], B, S)
    y_ret = _retention(ht, tables, p['tgn_w'], p['tgn_b'], B, S)
    x2d = _out_proj_ln(x2d, y_fox, y_rwkv, y_ret, p['w_out'], l, p['ln1_g'], p['ln1_b'])
    return _ffn_ln(x2d, p['w_up'], p['w_down'], l, p['ln2_g'], p['ln2_b'])


def kernel(x, w_in, fox_forget_bias, rwkv_mu, rwkv_w0, rwkv_w_up, rwkv_a0, rwkv_a_up, rwkv_g_up,
           rwkv_k_k, rwkv_k_a, rwkv_r_k, rwkv_gn_w, rwkv_gn_b, ret_gn_w, ret_gn_b, w_out, ln1_g,
           ln1_b, w_up, w_down, ln2_g, ln2_b):
    B, S, D = x.shape
    L = w_in.shape[0]
    w_q, w_rw, w_ret = _permute_w_in(w_in)
    q_scale = jnp.concatenate([jnp.full((1, FOX_W), FOX_Q_SCALE, F32),
                               jnp.ones((1, 2 * FOX_W), F32)], axis=1)
    w_out_b = w_out.astype(BF16)
    w_up_b = w_up.astype(BF16)
    w_down_b = w_down.astype(BF16)
    bias_pad = jnp.pad(fox_forget_bias, ((0, 0), (0, LANE - FOX_HEADS)))
    tables = _ret_tables(S)
    row = lambda t: t.reshape(1, -1)
    x2d = x.reshape(B * S, D)
    xb = x2d
    for l in range(L):
        p = {
            'w_rw': w_rw, 'w_ret': w_ret, 'w_q': w_q, 'q_scale': q_scale,
            'fox_bias': row(bias_pad[l]),
            'mu': row(rwkv_mu[l]), 'w0': row(rwkv_w0[l]), 'w_up_lora': rwkv_w_up[l].astype(BF16),
            'a0': row(rwkv_a0[l]), 'a_up': rwkv_a_up[l].astype(BF16),
            'g_up': rwkv_g_up[l].astype(BF16),
            'k_k': row(rwkv_k_k[l]), 'k_a': row(rwkv_k_a[l]), 'r_k': row(rwkv_r_k[l]),
            'rgn_w': row(rwkv_gn_w[l]), 'rgn_b': row(rwkv_gn_b[l]),
            'tgn_w': row(ret_gn_w[l]), 'tgn_b': row(ret_gn_b[l]),
            'w_out': w_out_b,
            'ln1_g': row(ln1_g[l]), 'ln1_b': row(ln1_b[l]),
            'w_up': w_up_b, 'w_down': w_down_b,
            'ln2_g': row(ln2_g[l]), 'ln2_b': row(ln2_b[l]),
        }
        x2d, xb = _layer(x2d, xb, l, p, tables, B, S)
    return x2d.reshape(B, S, D)
```

```python
import functools

import jax
import jax.numpy as jnp
from jax import lax
from jax.experimental import pallas as pl
from jax.experimental.pallas import tpu as pltpu

F32 = jnp.float32
BF16 = jnp.bfloat16

D_MODEL = 2048
DEPTH = 4
FOX_W = 768
RWKV_W = 640
RET_W = 640
FOX_HEAD_DIM = 128
FOX_HEADS = 6
RWKV_HEAD_DIM = 64
RWKV_HEADS = 10
RET_HEAD_DIM = 128
RET_HEADS = 5
DECAY_LORA = 64
ICLR_LORA = 64
GATE_LORA = 128
D_FF = 4 * D_MODEL
RET_CHUNK = 128
ROPE_BASE = 10000.0
LN_EPS = 1e-5
RWKV_GN_EPS = 64e-5
RET_GN_EPS = 1e-5
ALPHA = (2 * DEPTH) ** 0.25

FOX_COLS = 3 * FOX_W + FOX_HEADS
RWKV_COLS = 3 * RWKV_W + DECAY_LORA + ICLR_LORA + GATE_LORA
RET_COLS = 4 * RET_W

LANE = 128
H_FF0 = RWKV_COLS
LOG2E = 1.4426950408889634
FOX_Q_SCALE = FOX_HEAD_DIM ** -0.5 * LOG2E

FOX_BLOCKS_PER_TRIP = 4
FOX_Q_ROWS = 1024
RWKV_CHUNK = 64
RWKV_CHUNKS_PER_STEP = 4
NEG_BIG = -1e30

VMEM_LIMIT = 56 * 1024 * 1024


def _cparams(sem):
    return pltpu.CompilerParams(dimension_semantics=sem, vmem_limit_bytes=VMEM_LIMIT)


def _dot(a, b):
    return jnp.dot(a, b, preferred_element_type=F32)


def _dot_nt(a, b):
    return lax.dot_general(a, b, (((1,), (1,)), ((), ())), preferred_element_type=F32)


def _dot_tn(a, b):
    return lax.dot_general(a, b, (((0,), (0,)), ((), ())), preferred_element_type=F32)


def _split3(x):
    hi = x.astype(BF16)
    r1 = x - hi.astype(F32)
    mid = r1.astype(BF16)
    lo = (r1 - mid.astype(F32)).astype(BF16)
    return hi, mid, lo


def _split2(x):
    hi = x.astype(BF16)
    lo = (x - hi.astype(F32)).astype(BF16)
    return hi, lo


def _mask_dot(mask_bf16, x):
    hi, mid, lo = _split3(x)
    return _dot(mask_bf16, hi) + _dot(mask_bf16, mid) + _dot(mask_bf16, lo)


def _softplus(z):
    return jnp.maximum(z, 0.0) + jnp.log(1.0 + jnp.exp(-jnp.abs(z)))


def _sigmoid(z):
    return 1.0 / (1.0 + jnp.exp(-z))


def _layer_norm_rows(y, g, b):
    mu = jnp.mean(y, axis=-1, keepdims=True)
    yc = y - mu
    var = jnp.mean(yc * yc, axis=-1, keepdims=True)
    return yc * lax.rsqrt(var + LN_EPS) * g + b


def _inproj_kernel(x_ref, w_ref, o_ref):
    o_ref[...] = _dot(x_ref[...].astype(BF16), w_ref[...])


def _inproj_scaled_kernel(x_ref, w_ref, s_ref, o_ref):
    o_ref[...] = (_dot(x_ref[...].astype(BF16), w_ref[...]) * s_ref[...]).astype(o_ref.dtype)


def _in_proj(xb, w, l, col_scale=None):
    M, K = xb.shape
    N = w.shape[2]
    tm = min(1024 if xb.dtype == BF16 else 512, M)
    tn = N
    in_specs = [pl.BlockSpec((tm, K), lambda j, i: (i, 0)),
                pl.BlockSpec((None, K, tn), lambda j, i: (l, 0, j))]
    args = [xb, w]
    if col_scale is not None:
        in_specs.append(pl.BlockSpec((1, tn), lambda j, i: (0, j)))
        args.append(col_scale)
    return pl.pallas_call(
        _inproj_kernel if col_scale is None else _inproj_scaled_kernel,
        name="in_proj",
        out_shape=jax.ShapeDtypeStruct((M, N), F32 if col_scale is None else BF16),
        grid=(N // tn, M // tm),
        in_specs=in_specs,
        out_specs=pl.BlockSpec((tm, tn), lambda j, i: (i, j)),
        compiler_params=_cparams(("parallel", "parallel")),
    )(*args)


def _fox_c_kernel(f_ref, bias_ref, c_ref, *, rows):
    S = f_ref.shape[0]
    r = lax.broadcasted_iota(jnp.int32, (rows, rows), 0)
    c = lax.broadcasted_iota(jnp.int32, (rows, rows), 1)
    tri = (c <= r).astype(BF16)

    bias = bias_ref[...]
    local = [_mask_dot(tri, -_softplus(-(f_ref[s:s + rows, :] + bias))) for s in range(0, S, rows)]
    carry = jnp.zeros((1, LANE), F32)
    for j, cs in enumerate(local):
        c_ref[:, j * rows:(j + 1) * rows] = ((cs + carry) * LOG2E).T[0:8, :]
        carry = carry + cs[rows - 1:rows, :]


def _fox_c(h, bias_pad, B, S):
    rows = min(256, S)
    return pl.pallas_call(
        functools.partial(_fox_c_kernel, rows=rows),
        name="fox_c",
        out_shape=jax.ShapeDtypeStruct((B, 8, S), F32),
        grid=(B,),
        in_specs=[pl.BlockSpec((S, LANE), lambda b: (b, H_FF0 // LANE)),
                  pl.BlockSpec((1, LANE), lambda b: (0, 0))],
        out_specs=pl.BlockSpec((None, 8, S), lambda b: (b, 0, 0)),
        compiler_params=_cparams(("parallel",)),
    )(h, bias_pad)


def _fox_attn_kernel(q_ref, qn_ref, k_ref, v_ref, c_ref, o_ref, s0_ref, *, tq, tk):
    i = pl.program_id(2)
    d = q_ref.shape[1]
    q0 = pl.multiple_of(i * tq, tq)
    c0 = c_ref[:, pl.ds(q0, tq)][:, 0:1]
    q = q_ref[...]

    def scores(start, row_lo=0):
        bias = c0 - c_ref[:, pl.ds(start, tk)]
        return _dot_nt(q[row_lo:], k_ref[pl.ds(start, tk), :]) + bias

    def next_step_scores():
        qn0 = pl.multiple_of(jnp.minimum(i + 1, pl.num_programs(2) - 1) * tq, tq)
        cn = c_ref[:, pl.ds(qn0, tq)][:, 0:1]
        return _dot_nt(qn_ref[...], k_ref[0:tk, :]) + (cn - c_ref[:, 0:tk])

    ones = jnp.ones((tk, d), BF16)

    def update(start, sc, carry, masked, row_lo=0):
        m_all, acc_all = carry
        m, acc = m_all[row_lo:], acc_all[row_lo:]
        if masked:
            row = lax.broadcasted_iota(jnp.int32, sc.shape, 0) + (q0 + row_lo)
            col = lax.broadcasted_iota(jnp.int32, sc.shape, 1) + start
            sc = jnp.where(col <= row, sc, NEG_BIG)
        m_new = jnp.maximum(m, jnp.max(sc, axis=1, keepdims=True))
        alpha = jnp.exp2(m - m_new)
        p = jnp.exp2((sc - m_new).astype(BF16))
        vb = jnp.concatenate([v_ref[pl.ds(start, tk), :], ones], axis=1)
        acc = alpha * acc + _dot(p, vb)
        if row_lo:
            m_new = jnp.concatenate([m_all[:row_lo], m_new], axis=0)
            acc = jnp.concatenate([acc_all[:row_lo], acc], axis=0)
        return m_new, acc

    U = FOX_BLOCKS_PER_TRIP

    D = tq // tk

    def walk(base, n, carry, last):
        skip = lambda u: max(u - (n - D), 0) * tk if last else 0
        s_cur = s0_ref[...]
        for u in range(n):
            start = pl.multiple_of(base + u * tk, tk)
            final = last and u == n - 1
            s_next = (next_step_scores() if final
                      else scores(pl.multiple_of(start + tk, tk), skip(u + 1)))
            carry = update(start, s_cur, carry, last and u >= n - D, skip(u))
            s_cur = s_next
        s0_ref[...] = s_cur
        return carry

    @pl.when(i == 0)
    def _():
        s0_ref[...] = scores(0)

    init = (jnp.full((tq, 1), NEG_BIG, F32), jnp.zeros((tq, 2 * d), F32))
    n_full = i * D
    trips = n_full // U
    carry = lax.fori_loop(
        0, trips, lambda t, c: walk(pl.multiple_of(t * (U * tk), tk), U, c, False), init)

    base = pl.multiple_of(trips * (U * tk), tk)
    rems = list(range(0, U, D))
    tails = [functools.partial(walk, base, r + D, last=True) for r in rems]
    m, acc = lax.switch((n_full - trips * U) // D, tails, carry)
    o_ref[...] = (acc[:, :d] / acc[:, d:]).astype(o_ref.dtype)


def _fox_attention(hq, c3, B, S):
    tk = min(512, S)
    tq = min(FOX_Q_ROWS, S)
    nq = S // tq
    nh = FOX_HEADS
    return pl.pallas_call(
        functools.partial(_fox_attn_kernel, tq=tq, tk=tk),
        name="fox_attn",
        scratch_shapes=[pltpu.VMEM((tq, tk), F32)],
        out_shape=jax.ShapeDtypeStruct((B * S, FOX_W), BF16),
        grid=(B, FOX_HEADS, nq),
        in_specs=[pl.BlockSpec((tq, LANE), lambda b, hd, i: (b * nq + i, hd)),
                  pl.BlockSpec((tq, LANE),
                               lambda b, hd, i: (b * nq + jnp.minimum(i + 1, nq - 1), hd)),
                  pl.BlockSpec((S, LANE), lambda b, hd, i: (b, nh + hd)),
                  pl.BlockSpec((S, LANE), lambda b, hd, i: (b, 2 * nh + hd)),
                  pl.BlockSpec((None, 1, S), lambda b, hd, i: (b * nh + hd, 0, 0))],
        out_specs=pl.BlockSpec((tq, LANE), lambda b, hd, i: (b * nq + i, hd)),
        compiler_params=_cparams(("parallel", "parallel", "arbitrary")),
    )(hq, hq, hq, hq, c3)


def _rwkv_premix(h, prev, mu, w0, wup, a0, aup, gup):
    rows = lax.broadcasted_iota(jnp.int32, h.shape, 0)
    hp = jnp.where(rows == 0, prev, pltpu.roll(h, 1, 0))
    hs = h + (hp - h) * mu
    W3 = 3 * RWKV_W
    r = hs[:, :RWKV_W]
    k = hs[:, RWKV_W:2 * RWKV_W]
    v = hs[:, 2 * RWKV_W:W3]
    wd = hs[:, W3:W3 + DECAY_LORA]
    ad = hs[:, W3 + DECAY_LORA:W3 + DECAY_LORA + ICLR_LORA]
    gd = hs[:, W3 + DECAY_LORA + ICLR_LORA:]
    z = w0 + _dot(jnp.tanh(wd).astype(BF16), wup)
    w = -_softplus(-z) - 0.5
    lw = -jnp.exp(w)
    a = _sigmoid(a0 + _dot(ad.astype(BF16), aup))
    g = _dot(_sigmoid(gd).astype(BF16), gup)
    return r, k, v, lw, a, g


def _head_sums(x, e_ref):
    hi, lo = _split2(x)
    e = e_ref[...]
    e_last = e[:LANE, :LANE]
    outs = []
    for c0 in range(0, RWKV_W, 2 * LANE):
        w = min(2 * LANE, RWKV_W - c0)
        ee = e if w == 2 * LANE else e_last
        outs.append(_dot(hi[:, c0:c0 + w], ee) + _dot(lo[:, c0:c0 + w], ee))
    return jnp.concatenate(outs, axis=1)


def _rwkv_scan_kernel(h_ref, hprev_ref, mu_ref, w0_ref, wup_ref, a0_ref, aup_ref, gup_ref,
                      kk_ref, ka_ref, rk_ref, gw_ref, gb_ref, e_ref, o_ref, st_ref, *, nchunk):
    C = RWKV_CHUNK
    N = RWKV_HEAD_DIM
    T = nchunk * C
    first = pl.program_id(1) == 0

    @pl.when(first)
    def _():
        st_ref[...] = jnp.zeros(st_ref.shape, F32)

    prev = jnp.where(first, 0.0, hprev_ref[7:8, :])
    r, k, v, lw, a, g = _rwkv_premix(h_ref[...], prev, mu_ref[...], w0_ref[...], wup_ref[...],
                                     a0_ref[...], aup_ref[...], gup_ref[...])

    ri = lax.broadcasted_iota(jnp.int32, (C, C), 0)
    ci = lax.broadcasted_iota(jnp.int32, (C, C), 1)
    incl = ci <= ri
    strict = ci < ri

    rt = lax.broadcasted_iota(jnp.int32, (T, T), 0)
    ct = lax.broadcasted_iota(jnp.int32, (T, T), 1)
    tri = ((ct <= rt) & (ct >= (rt // C) * C)).astype(BF16)
    lw_hi, lw_lo = _split2(lw)
    cum = _dot(tri, lw_hi) + _dot(tri, lw_lo)
    cls = [cum[(c + 1) * C - 1:(c + 1) * C, :] for c in range(nchunk)]
    cl_rows = jnp.concatenate([jnp.broadcast_to(t, (C, RWKV_W)) for t in cls], axis=0)
    e_in = jnp.exp(cum)
    e_ex = jnp.exp(cum - lw)
    e_neg = jnp.exp(-cum)
    e_tail = jnp.exp(cl_rows - cum)
    pcs = [jnp.exp(t) for t in cls]

    kk = k * kk_ref[...]
    kkn = kk * lax.rsqrt(jnp.maximum(_head_sums(kk * kk, e_ref), 1e-24))
    k2 = k * (1.0 + (a - 1.0) * ka_ref[...])
    bvec = kkn * a
    At_f = -kkn * e_ex
    Rt_f = r * e_in
    AtB = At_f.astype(BF16)
    RtB = Rt_f.astype(BF16)
    BtB = (bvec * e_neg).astype(BF16)
    KtB = (k2 * e_neg).astype(BF16)
    BpB = (bvec * e_tail).astype(BF16)
    KpB = (k2 * e_tail).astype(BF16)
    vB = v.astype(BF16)

    HD = range(RWKV_HEADS)
    hsl = [slice(hd * N, (hd + 1) * N) for hd in HD]
    sls = [(slice(c * C, (c + 1) * C), hs) for c in range(nchunk) for hs in hsl]
    At = [At_f[rs, hs] for rs, hs in sls]
    Rt = [Rt_f[rs, hs] for rs, hs in sls]
    vb = [vB[rs, hs] for rs, hs in sls]
    Bt = [BtB[rs, hs] for rs, hs in sls]
    Kt = [KtB[rs, hs] for rs, hs in sls]
    Bp = [BpB[rs, hs] for rs, hs in sls]
    Kp = [KpB[rs, hs] for rs, hs in sls]

    AR = [jnp.concatenate([AtB[rs, hs], RtB[rs, hs]], axis=0) for rs, hs in sls]
    BK = [jnp.concatenate([x, y], axis=0) for x, y in zip(Bt, Kt)]
    G = [_dot_nt(x, y) for x, y in zip(AR, BK)]
    r2 = lax.broadcasted_iota(jnp.int32, (C, 2 * C), 0)
    c2 = lax.broadcasted_iota(jnp.int32, (C, 2 * C), 1)
    incl2 = jnp.where(c2 >= C, c2 - C, c2) <= r2
    P = [jnp.where(strict, t[:C, :C], 0.0) for t in G]
    Lak = [jnp.where(strict, t[:C, C:], 0.0).astype(BF16) for t in G]
    L2 = [jnp.where(incl2, t[C:], 0.0).astype(BF16) for t in G]

    X = [jnp.concatenate([x, _dot(y, z)], axis=1) for x, y, z in zip(At, Lak, vb)]
    steps = C.bit_length() - 1
    for it in range(steps):
        Pb = [t.astype(BF16) for t in P]
        Xb = [x.astype(BF16) for x in X]
        if it + 1 < steps:
            PX = [_dot(p, jnp.concatenate([x, p], axis=1)) for p, x in zip(Pb, Xb)]
            X = [x + t[:, :2 * N] for x, t in zip(X, PX)]
            P = [t[:, 2 * N:] for t in PX]
        else:
            X = [x + _dot(p, xb) for x, p, xb in zip(X, Pb, Xb)]
    Xb = [x.astype(BF16) for x in X]

    zpad = jnp.zeros((C, N), BF16)
    RHS = [jnp.concatenate([x, jnp.concatenate([zpad, z], axis=1)], axis=0)
           for x, z in zip(Xb, vb)]
    LX = [_dot(x, y) for x, y in zip(L2, RHS)]
    Qp = [(x + y[:, :N]).astype(BF16) for x, y in zip(Rt, LX)]
    Y0 = [y[:, N:] for y in LX]

    XB = [_dot_tn(x, b) for x, b in zip(Xb, Bp)]
    Mw = [t[:N].astype(BF16) for t in XB]
    Nn = [t[N:] + _dot_tn(x, y) for t, x, y in zip(XB, vb, Kp)]

    S = [st_ref[hd] for hd in HD]
    y_rows = []
    for c in range(nchunk):
        ch = [c * RWKV_HEADS + hd for hd in HD]
        Ssp = [_split2(s) for s in S]
        ys = [_dot_nt(Qp[j], sp[0]) + Y0[j] for j, sp in zip(ch, Ssp)]
        y_rows.append(jnp.concatenate(ys, axis=1))
        SM = [_dot(jnp.concatenate(sp, axis=0), Mw[j]) for sp, j in zip(Ssp, ch)]
        S = [s * pcs[c][:, hs] + sm[:N] + sm[N:] + Nn[j]
             for s, sm, j, hs in zip(S, SM, ch, hsl)]
    for hd in HD:
        st_ref[hd] = S[hd]

    y = jnp.concatenate(y_rows, axis=0)
    inv_n = 1.0 / N
    yc = y - _head_sums(y, e_ref) * inv_n
    var = _head_sums(yc * yc, e_ref) * inv_n
    yn = yc * lax.rsqrt(var + RWKV_GN_EPS) * gw_ref[...] + gb_ref[...]
    bonus = _head_sums(r * k2 * rk_ref[...], e_ref) * v
    o_ref[...] = ((yn + bonus) * g).astype(o_ref.dtype)


def _rwkv_mix(h, mu, w0, wup, a0, aup, gup, k_k, k_a, r_k, gn_w, gn_b, B, S):
    nchunk = RWKV_CHUNKS_PER_STEP
    T = nchunk * RWKV_CHUNK
    nc = S // T
    row = lambda b, c: (b * nc + c, 0)
    prev_row = lambda b, c: (jnp.maximum((b * nc + c) * (T // 8) - 1, 0), 0)
    const = lambda b, c: (0, 0)
    vec = pl.BlockSpec((1, RWKV_W), const)
    hid = jnp.arange(2 * LANE, dtype=jnp.int32) // RWKV_HEAD_DIM
    head_ones = (hid[:, None] == hid[None, :]).astype(BF16)
    return pl.pallas_call(
        functools.partial(_rwkv_scan_kernel, nchunk=nchunk),
        name="rwkv_scan",
        out_shape=jax.ShapeDtypeStruct((B * S, RWKV_W), BF16),
        grid=(B, nc),
        in_specs=[pl.BlockSpec((T, RWKV_COLS), row),
                  pl.BlockSpec((8, RWKV_COLS), prev_row),
                  pl.BlockSpec((1, RWKV_COLS), const),
                  vec,
                  pl.BlockSpec((DECAY_LORA, RWKV_W), const),
                  vec,
                  pl.BlockSpec((ICLR_LORA, RWKV_W), const),
                  pl.BlockSpec((GATE_LORA, RWKV_W), const),
                  vec, vec, vec, vec, vec,
                  pl.BlockSpec((2 * LANE, 2 * LANE), const)],
        out_specs=pl.BlockSpec((T, RWKV_W), row),
        scratch_shapes=[pltpu.VMEM((RWKV_HEADS, RWKV_HEAD_DIM, RWKV_HEAD_DIM), F32)],
        compiler_params=_cparams(("parallel", "arbitrary")),
    )(h, h, mu, w0, wup, a0, aup, gup, k_k, k_a, r_k, gn_w, gn_b, head_ones)


def _ret_kernel(h_ref, cos_ref, sin_ref, dmat_ref, qd_ref, kd_ref, cd_ref, gw_ref, gb_ref,
                o_ref, R_ref, *, nchunk):
    C = RET_CHUNK
    d = RET_HEAD_DIM
    H = RET_HEADS
    W = RET_W

    @pl.when(pl.program_id(1) == 0)
    def _():
        R_ref[...] = jnp.zeros(R_ref.shape, F32)

    pairs = [(c, hd) for c in range(nchunk) for hd in range(H)]
    rows = lambda c: slice(c * C, (c + 1) * C)
    cols = lambda part, hd: slice(part * W + hd * d, part * W + (hd + 1) * d)
    cos = [cos_ref[rows(c), :] for c in range(nchunk)]
    sin = [sin_ref[rows(c), :] for c in range(nchunk)]
    q = [h_ref[rows(c), cols(0, hd)] for c, hd in pairs]
    k = [h_ref[rows(c), cols(1, hd)] for c, hd in pairs]
    qr = [x * cos[c] + pltpu.roll(x, d // 2, 1) * sin[c] for x, (c, hd) in zip(q, pairs)]
    kr = [(x * cos[c] + pltpu.roll(x, d // 2, 1) * sin[c]) * (d ** -0.5)
          for x, (c, hd) in zip(k, pairs)]
    vb = [h_ref[rows(c), cols(2, hd)].astype(BF16) for c, hd in pairs]
    qb = [x.astype(BF16) for x in qr]
    kb = [x.astype(BF16) for x in kr]
    kdb = [(x * kd_ref[hd]).astype(BF16) for x, (c, hd) in zip(kr, pairs)]
    inner = [(_dot_nt(x, y) * dmat_ref[hd]).astype(BF16) for x, y, (c, hd) in zip(qb, kb, pairs)]
    o_in = [_dot(x, y) for x, y in zip(inner, vb)]
    kv = [_dot_tn(x, y) for x, y in zip(kdb, vb)]

    R = [R_ref[hd] for hd in range(H)]
    outs = []
    for c in range(nchunk):
        ch = [c * H + hd for hd in range(H)]
        Rsp = [_split2(x) for x in R]
        outs += [o_in[j] + (_dot(qb[j], sp[0]) + _dot(qb[j], sp[1])) * qd_ref[hd]
                 for hd, (j, sp) in enumerate(zip(ch, Rsp))]
        R = [cd_ref[hd][0:1, :] * x + kv[j] for hd, (x, j) in enumerate(zip(R, ch))]
    for hd in range(H):
        R_ref[hd] = R[hd]

    mus = [jnp.mean(o, axis=1, keepdims=True) for o in outs]
    ocs = [o - mu for o, mu in zip(outs, mus)]
    rstd = [lax.rsqrt(jnp.mean(oc * oc, axis=1, keepdims=True) + RET_GN_EPS) for oc in ocs]
    for oc, rs, (c, hd) in zip(ocs, rstd, pairs):
        gsl = slice(hd * d, (hd + 1) * d)
        on = oc * rs * gw_ref[:, gsl] + gb_ref[:, gsl]
        gate = h_ref[rows(c), cols(3, hd)]
        o_ref[rows(c), gsl] = (gate * _sigmoid(gate) * on).astype(o_ref.dtype)


def _ret_tables(S):
    C, d, H = RET_CHUNK, RET_HEAD_DIM, RET_HEADS
    half = d // 2
    inv = 1.0 / (ROPE_BASE ** (jnp.arange(half, dtype=F32) / half))
    ang = jnp.arange(S, dtype=F32)[:, None] * inv[None, :]
    cos = jnp.cos(ang)
    sin = jnp.sin(ang)
    cos2 = jnp.concatenate([cos, cos], axis=1)
    sin2 = jnp.concatenate([-sin, sin], axis=1)
    log_g = jnp.log(1.0 - 2.0 ** (-5.0 - jnp.arange(H, dtype=F32)))
    pos = jnp.arange(C, dtype=F32)
    rel = pos[:, None] - pos[None, :]
    dmat = jnp.where(rel >= 0, jnp.exp(log_g[:, None, None] * jnp.maximum(rel, 0.0)), 0.0)
    qd = jnp.broadcast_to(jnp.exp(log_g[:, None] * (pos + 1.0))[..., None], (H, C, d))
    kd = jnp.broadcast_to(jnp.exp(log_g[:, None] * (C - 1.0 - pos))[..., None], (H, C, d))
    cd = jnp.broadcast_to(jnp.exp(log_g * C)[:, None, None], (H, 8, d))
    return cos2, sin2, dmat, qd, kd, cd


def _retention(h, tables, gn_w, gn_b, B, S):
    cos2, sin2, dmat, qd, kd, cd = tables
    C = RET_CHUNK
    T = min(512, S)
    nt = S // T
    H = RET_HEADS
    tab = pl.BlockSpec((T, LANE), lambda b, t: (t, 0))
    per_head = lambda rows: pl.BlockSpec((H, rows, LANE), lambda b, t: (0, 0, 0))
    vec = pl.BlockSpec((1, RET_W), lambda b, t: (0, 0))
    return pl.pallas_call(
        functools.partial(_ret_kernel, nchunk=T // C),
        name="retention",
        out_shape=jax.ShapeDtypeStruct((B * S, RET_W), BF16),
        grid=(B, nt),
        in_specs=[pl.BlockSpec((T, RET_COLS), lambda b, t: (b * nt + t, 0)),
                  tab, tab, per_head(C), per_head(C), per_head(C), per_head(8), vec, vec],
        out_specs=pl.BlockSpec((T, RET_W), lambda b, t: (b * nt + t, 0)),
        scratch_shapes=[pltpu.VMEM((H, RET_HEAD_DIM, RET_HEAD_DIM), F32)],
        compiler_params=_cparams(("parallel", "arbitrary")),
    )(h, cos2, sin2, dmat, qd, kd, cd, gn_w, gn_b)


def _outproj_kernel(x_ref, yf_ref, yr_ref, yt_ref, w_ref, g_ref, b_ref, o_ref):
    r0, r1 = FOX_W, FOX_W + RWKV_W
    y = (_dot(yf_ref[...], w_ref[:r0, :]) + _dot(yr_ref[...], w_ref[r0:r1, :])
         + _dot(yt_ref[...], w_ref[r1:, :]))
    o_ref[...] = _layer_norm_rows(ALPHA * x_ref[...] + y, g_ref[...], b_ref[...])


def _out_proj_ln(x2d, y_fox, y_rwkv, y_ret, w_out, l, g, b):
    M, D = x2d.shape
    tm = min(512, M)
    row = lambda i: (i, 0)
    const = lambda i: (0, 0)
    return pl.pallas_call(
        _outproj_kernel,
        name="out_proj_ln",
        out_shape=jax.ShapeDtypeStruct((M, D), F32),
        grid=(M // tm,),
        in_specs=[pl.BlockSpec((tm, D), row), pl.BlockSpec((tm, FOX_W), row),
                  pl.BlockSpec((tm, RWKV_W), row), pl.BlockSpec((tm, RET_W), row),
                  pl.BlockSpec((None, D, D), lambda i: (l, 0, 0)),
                  pl.BlockSpec((1, D), const), pl.BlockSpec((1, D), const)],
        out_specs=pl.BlockSpec((tm, D), row),
        compiler_params=_cparams(("parallel",)),
    )(x2d, y_fox, y_rwkv, y_ret, w_out, g, b)


def _ffn_kernel(x_ref, wu_ref, wd_ref, g_ref, b_ref, o_ref, ob_ref, acc_ref, xb_ref):
    f = pl.program_id(1)

    @pl.when(f == 0)
    def _():
        x = x_ref[...]
        xb_ref[...] = x.astype(BF16)
        acc_ref[...] = ALPHA * x

    hid = jnp.maximum(_dot(xb_ref[...], wu_ref[...]), 0.0)
    acc_ref[...] += _dot((hid * hid).astype(BF16), wd_ref[...])

    @pl.when(f == pl.num_programs(1) - 1)
    def _():
        y = _layer_norm_rows(acc_ref[...], g_ref[...], b_ref[...])
        o_ref[...] = y
        ob_ref[...] = y.astype(BF16)


def _ffn_ln(x2d, w_up, w_down, l, g, b):
    M, D = x2d.shape
    F = w_up.shape[2]
    tm = min(512, M)
    tf = 1024
    return pl.pallas_call(
        _ffn_kernel,
        name="ffn_ln",
        out_shape=(jax.ShapeDtypeStruct((M, D), F32), jax.ShapeDtypeStruct((M, D), BF16)),
        grid=(M // tm, F // tf),
        in_specs=[pl.BlockSpec((tm, D), lambda i, f: (i, 0)),
                  pl.BlockSpec((None, D, tf), lambda i, f: (l, 0, f)),
                  pl.BlockSpec((None, tf, D), lambda i, f: (l, f, 0)),
                  pl.BlockSpec((1, D), lambda i, f: (0, 0)),
                  pl.BlockSpec((1, D), lambda i, f: (0, 0))],
        out_specs=(pl.BlockSpec((tm, D), lambda i, f: (i, 0)),
                   pl.BlockSpec((tm, D), lambda i, f: (i, 0))),
        scratch_shapes=[pltpu.VMEM((tm, D), F32), pltpu.VMEM((tm, D), BF16)],
        compiler_params=_cparams(("parallel", "arbitrary")),
    )(x2d, w_up, w_down, g, b)


def _permute_w_in(w_in):
    L, D, _ = w_in.shape
    w_in = w_in.astype(BF16)
    w_fox = w_in[:, :, :FOX_COLS]
    w_rwkv = w_in[:, :, FOX_COLS:FOX_COLS + RWKV_COLS]
    w_ret = w_in[:, :, FOX_COLS + RWKV_COLS:]
    w_ff = w_fox[:, :, 3 * FOX_W:]
    pad = jnp.zeros((L, D, LANE - FOX_HEADS), BF16)
    w_q = w_fox[:, :, :3 * FOX_W]
    w_rw = jnp.concatenate([w_rwkv, w_ff, pad], axis=-1)
    return w_q, w_rw, w_ret


def _layer(x2d, xb, l, p, tables, B, S):
    hq = _in_proj(xb, p['w_q'], l, p['q_scale'])
    h = _in_proj(xb, p['w_rw'], l)
    ht = _in_proj(xb, p['w_ret'], l)
    c = _fox_c(h, p['fox_bias'], B, S)
    c3 = c[:, :FOX_HEADS, :].reshape(B * FOX_HEADS, 1, S)
    y_fox = _fox_attention(hq, c3, B, S)
    y_rwkv = _rwkv_mix(h, p['mu'], p['w0'], p['w_up_lora'], p['a0'], p['a_up'], p['g_up'],
                       p['k_k'], p['k_a'], p['r_k'], p['rgn_w'], p['rgn_b'], B, S)
    y_ret = _retention(ht, tables, p['tgn_w'], p['tgn_b'], B, S)
    x2d = _out_proj_ln(x2d, y_fox, y_rwkv, y_ret, p['w_out'], l, p['ln1_g'], p['ln1_b'])
    return _ffn_ln(x2d, p['w_up'], p['w_down'], l, p['ln2_g'], p['ln2_b'])


def kernel(x, w_in, fox_forget_bias, rwkv_mu, rwkv_w0, rwkv_w_up, rwkv_a0, rwkv_a_up, rwkv_g_up,
           rwkv_k_k, rwkv_k_a, rwkv_r_k, rwkv_gn_w, rwkv_gn_b, ret_gn_w, ret_gn_b, w_out, ln1_g,
           ln1_b, w_up, w_down, ln2_g, ln2_b):
    B, S, D = x.shape
    L = w_in.shape[0]
    w_q, w_rw, w_ret = _permute_w_in(w_in)
    q_scale = jnp.concatenate([jnp.full((1, FOX_W), FOX_Q_SCALE, F32),
                               jnp.ones((1, 2 * FOX_W), F32)], axis=1)
    w_out_b = w_out.astype(BF16)
    w_up_b = w_up.astype(BF16)
    w_down_b = w_down.astype(BF16)
    bias_pad = jnp.pad(fox_forget_bias, ((0, 0), (0, LANE - FOX_HEADS)))
    tables = _ret_tables(S)
    row = lambda t: t.reshape(1, -1)
    x2d = x.reshape(B * S, D)
    xb = x2d
    for l in range(L):
        p = {
            'w_rw': w_rw, 'w_ret': w_ret, 'w_q': w_q, 'q_scale': q_scale,
            'fox_bias': row(bias_pad[l]),
            'mu': row(rwkv_mu[l]), 'w0': row(rwkv_w0[l]), 'w_up_lora': rwkv_w_up[l].astype(BF16),
            'a0': row(rwkv_a0[l]), 'a_up': rwkv_a_up[l].astype(BF16),
            'g_up': rwkv_g_up[l].astype(BF16),
            'k_k': row(rwkv_k_k[l]), 'k_a': row(rwkv_k_a[l]), 'r_k': row(rwkv_r_k[l]),
            'rgn_w': row(rwkv_gn_w[l]), 'rgn_b': row(rwkv_gn_b[l]),
            'tgn_w': row(ret_gn_w[l]), 'tgn_b': row(ret_gn_b[l]),
            'w_out': w_out_b,
            'ln1_g': row(ln1_g[l]), 'ln1_b': row(ln1_b[l]),
            'w_up': w_up_b, 'w_down': w_down_b,
            'ln2_g': row(ln2_g[l]), 'ln2_b': row(ln2_b[l]),
        }
        x2d, xb = _layer(x2d, xb, l, p, tables, B, S)
    return x2d.reshape(B, S, D)
```

```python
import functools

import jax
import jax.numpy as jnp
from jax import lax
from jax.experimental import pallas as pl
from jax.experimental.pallas import tpu as pltpu

F32 = jnp.float32
BF16 = jnp.bfloat16

D_MODEL = 2048
DEPTH = 4
FOX_W = 768
RWKV_W = 640
RET_W = 640
FOX_HEAD_DIM = 128
FOX_HEADS = 6
RWKV_HEAD_DIM = 64
RWKV_HEADS = 10
RET_HEAD_DIM = 128
RET_HEADS = 5
DECAY_LORA = 64
ICLR_LORA = 64
GATE_LORA = 128
D_FF = 4 * D_MODEL
RET_CHUNK = 128
ROPE_BASE = 10000.0
LN_EPS = 1e-5
RWKV_GN_EPS = 64e-5
RET_GN_EPS = 1e-5
ALPHA = (2 * DEPTH) ** 0.25

FOX_COLS = 3 * FOX_W + FOX_HEADS
RWKV_COLS = 3 * RWKV_W + DECAY_LORA + ICLR_LORA + GATE_LORA
RET_COLS = 4 * RET_W

LANE = 128
H_FF0 = RWKV_COLS
LOG2E = 1.4426950408889634
FOX_Q_SCALE = FOX_HEAD_DIM ** -0.5 * LOG2E

FOX_BLOCKS_PER_TRIP = 4
FOX_Q_ROWS = 1024
RWKV_DECAY_SCALE = 0.6065306597126334
RWKV_CHUNK = 64
RWKV_CHUNKS_PER_STEP = 4
NEG_BIG = -1e30

VMEM_LIMIT = 56 * 1024 * 1024


def _cparams(sem):
    return pltpu.CompilerParams(dimension_semantics=sem, vmem_limit_bytes=VMEM_LIMIT)


def _dot(a, b):
    return jnp.dot(a, b, preferred_element_type=F32)


def _dot_nt(a, b):
    return lax.dot_general(a, b, (((1,), (1,)), ((), ())), preferred_element_type=F32)


def _dot_tn(a, b):
    return lax.dot_general(a, b, (((0,), (0,)), ((), ())), preferred_element_type=F32)


def _split3(x):
    hi = x.astype(BF16)
    r1 = x - hi.astype(F32)
    mid = r1.astype(BF16)
    lo = (r1 - mid.astype(F32)).astype(BF16)
    return hi, mid, lo


def _split2(x):
    hi = x.astype(BF16)
    lo = (x - hi.astype(F32)).astype(BF16)
    return hi, lo


def _mask_dot(mask_bf16, x):
    hi, mid, lo = _split3(x)
    return _dot(mask_bf16, hi) + _dot(mask_bf16, mid) + _dot(mask_bf16, lo)


def _softplus(z):
    return jnp.maximum(z, 0.0) + jnp.log(1.0 + jnp.exp(-jnp.abs(z)))


def _sigmoid(z):
    return 1.0 / (1.0 + jnp.exp(-z))


def _layer_norm_rows(y, g, b):
    mu = jnp.mean(y, axis=-1, keepdims=True)
    yc = y - mu
    var = jnp.mean(yc * yc, axis=-1, keepdims=True)
    return yc * lax.rsqrt(var + LN_EPS) * g + b


def _inproj_kernel(x_ref, w_ref, o_ref):
    o_ref[...] = _dot(x_ref[...].astype(BF16), w_ref[...])


def _inproj_scaled_kernel(x_ref, w_ref, s_ref, o_ref):
    o_ref[...] = (_dot(x_ref[...].astype(BF16), w_ref[...]) * s_ref[...]).astype(o_ref.dtype)


def _in_proj(xb, w, l, col_scale=None):
    M, K = xb.shape
    N = w.shape[2]
    tm = min(1024 if xb.dtype == BF16 else 512, M)
    tn = N
    in_specs = [pl.BlockSpec((tm, K), lambda j, i: (i, 0)),
                pl.BlockSpec((None, K, tn), lambda j, i: (l, 0, j))]
    args = [xb, w]
    if col_scale is not None:
        in_specs.append(pl.BlockSpec((1, tn), lambda j, i: (0, j)))
        args.append(col_scale)
    return pl.pallas_call(
        _inproj_kernel if col_scale is None else _inproj_scaled_kernel,
        name="in_proj",
        out_shape=jax.ShapeDtypeStruct((M, N), F32 if col_scale is None else BF16),
        grid=(N // tn, M // tm),
        in_specs=in_specs,
        out_specs=pl.BlockSpec((tm, tn), lambda j, i: (i, j)),
        compiler_params=_cparams(("parallel", "parallel")),
    )(*args)


def _fox_c_kernel(f_ref, bias_ref, c_ref, *, rows):
    S = f_ref.shape[0]
    r = lax.broadcasted_iota(jnp.int32, (rows, rows), 0)
    c = lax.broadcasted_iota(jnp.int32, (rows, rows), 1)
    tri = (c <= r).astype(BF16)

    bias = bias_ref[...]
    local = [_mask_dot(tri, -_softplus(-(f_ref[s:s + rows, :] + bias))) for s in range(0, S, rows)]
    carry = jnp.zeros((1, LANE), F32)
    for j, cs in enumerate(local):
        c_ref[:, j * rows:(j + 1) * rows] = ((cs + carry) * LOG2E).T[0:8, :]
        carry = carry + cs[rows - 1:rows, :]


def _fox_c(h, bias_pad, B, S):
    rows = min(256, S)
    return pl.pallas_call(
        functools.partial(_fox_c_kernel, rows=rows),
        name="fox_c",
        out_shape=jax.ShapeDtypeStruct((B, 8, S), F32),
        grid=(B,),
        in_specs=[pl.BlockSpec((S, LANE), lambda b: (b, H_FF0 // LANE)),
                  pl.BlockSpec((1, LANE), lambda b: (0, 0))],
        out_specs=pl.BlockSpec((None, 8, S), lambda b: (b, 0, 0)),
        compiler_params=_cparams(("parallel",)),
    )(h, bias_pad)


def _fox_attn_kernel(q_ref, qn_ref, k_ref, v_ref, c_ref, o_ref, s0_ref, *, tq, tk):
    i = pl.program_id(2)
    d = q_ref.shape[1]
    q0 = pl.multiple_of(i * tq, tq)
    c0 = c_ref[:, pl.ds(q0, tq)][:, 0:1]
    q = q_ref[...]

    def scores(start, row_lo=0):
        bias = c0 - c_ref[:, pl.ds(start, tk)]
        return _dot_nt(q[row_lo:], k_ref[pl.ds(start, tk), :]) + bias

    def next_step_scores():
        qn0 = pl.multiple_of(jnp.minimum(i + 1, pl.num_programs(2) - 1) * tq, tq)
        cn = c_ref[:, pl.ds(qn0, tq)][:, 0:1]
        return _dot_nt(qn_ref[...], k_ref[0:tk, :]) + (cn - c_ref[:, 0:tk])

    ones = jnp.ones((tk, d), BF16)

    def update(start, sc, carry, masked, row_lo=0):
        m_all, acc_all = carry
        m, acc = m_all[row_lo:], acc_all[row_lo:]
        if masked:
            row = lax.broadcasted_iota(jnp.int32, sc.shape, 0) + (q0 + row_lo)
            col = lax.broadcasted_iota(jnp.int32, sc.shape, 1) + start
            sc = jnp.where(col <= row, sc, NEG_BIG)
        m_new = jnp.maximum(m, jnp.max(sc, axis=1, keepdims=True))
        alpha = jnp.exp2(m - m_new)
        p = jnp.exp2((sc - m_new).astype(BF16))
        vb = jnp.concatenate([v_ref[pl.ds(start, tk), :], ones], axis=1)
        acc = alpha * acc + _dot(p, vb)
        if row_lo:
            m_new = jnp.concatenate([m_all[:row_lo], m_new], axis=0)
            acc = jnp.concatenate([acc_all[:row_lo], acc], axis=0)
        return m_new, acc

    U = FOX_BLOCKS_PER_TRIP

    D = tq // tk

    def walk(base, n, carry, last):
        skip = lambda u: max(u - (n - D), 0) * tk if last else 0
        s_cur = s0_ref[...]
        for u in range(n):
            start = pl.multiple_of(base + u * tk, tk)
            final = last and u == n - 1
            s_next = (next_step_scores() if final
                      else scores(pl.multiple_of(start + tk, tk), skip(u + 1)))
            carry = update(start, s_cur, carry, last and u >= n - D, skip(u))
            s_cur = s_next
        s0_ref[...] = s_cur
        return carry

    @pl.when(i == 0)
    def _():
        s0_ref[...] = scores(0)

    init = (jnp.full((tq, 1), NEG_BIG, F32), jnp.zeros((tq, 2 * d), F32))
    n_full = i * D
    trips = n_full // U
    carry = lax.fori_loop(
        0, trips, lambda t, c: walk(pl.multiple_of(t * (U * tk), tk), U, c, False), init)

    base = pl.multiple_of(trips * (U * tk), tk)
    rems = list(range(0, U, D))
    tails = [functools.partial(walk, base, r + D, last=True) for r in rems]
    m, acc = lax.switch((n_full - trips * U) // D, tails, carry)
    o_ref[...] = (acc[:, :d] / acc[:, d:]).astype(o_ref.dtype)


def _fox_attention(hq, c3, B, S):
    tk = min(512, S)
    tq = min(FOX_Q_ROWS, S)
    nq = S // tq
    nh = FOX_HEADS
    return pl.pallas_call(
        functools.partial(_fox_attn_kernel, tq=tq, tk=tk),
        name="fox_attn",
        scratch_shapes=[pltpu.VMEM((tq, tk), F32)],
        out_shape=jax.ShapeDtypeStruct((B * S, FOX_W), BF16),
        grid=(B, FOX_HEADS, nq),
        in_specs=[pl.BlockSpec((tq, LANE), lambda b, hd, i: (b * nq + i, hd)),
                  pl.BlockSpec((tq, LANE),
                               lambda b, hd, i: (b * nq + jnp.minimum(i + 1, nq - 1), hd)),
                  pl.BlockSpec((S, LANE), lambda b, hd, i: (b, nh + hd)),
                  pl.BlockSpec((S, LANE), lambda b, hd, i: (b, 2 * nh + hd)),
                  pl.BlockSpec((None, 1, S), lambda b, hd, i: (b * nh + hd, 0, 0))],
        out_specs=pl.BlockSpec((tq, LANE), lambda b, hd, i: (b * nq + i, hd)),
        compiler_params=_cparams(("parallel", "parallel", "arbitrary")),
    )(hq, hq, hq, hq, c3)


def _rwkv_premix(h, prev, mu, w0, wup, a0, aup, gup):
    rows = lax.broadcasted_iota(jnp.int32, h.shape, 0)
    hp = jnp.where(rows == 0, prev, pltpu.roll(h, 1, 0))
    hs = h + (hp - h) * mu
    W3 = 3 * RWKV_W
    r = hs[:, :RWKV_W]
    k = hs[:, RWKV_W:2 * RWKV_W]
    v = hs[:, 2 * RWKV_W:W3]
    wd = hs[:, W3:W3 + DECAY_LORA]
    ad = hs[:, W3 + DECAY_LORA:W3 + DECAY_LORA + ICLR_LORA]
    gd = hs[:, W3 + DECAY_LORA + ICLR_LORA:]
    z = w0 + _dot(jnp.tanh(wd).astype(BF16), wup)
    lw = -RWKV_DECAY_SCALE * _sigmoid(z)
    a = _sigmoid(a0 + _dot(ad.astype(BF16), aup))
    g = _dot(_sigmoid(gd).astype(BF16), gup)
    return r, k, v, lw, a, g


def _head_sums(x, e_ref):
    hi, lo = _split2(x)
    e = e_ref[...]
    e_last = e[:LANE, :LANE]
    outs = []
    for c0 in range(0, RWKV_W, 2 * LANE):
        w = min(2 * LANE, RWKV_W - c0)
        ee = e if w == 2 * LANE else e_last
        outs.append(_dot(hi[:, c0:c0 + w], ee) + _dot(lo[:, c0:c0 + w], ee))
    return jnp.concatenate(outs, axis=1)


def _rwkv_scan_kernel(h_ref, hprev_ref, mu_ref, w0_ref, wup_ref, a0_ref, aup_ref, gup_ref,
                      kk_ref, ka_ref, rk_ref, gw_ref, gb_ref, e_ref, o_ref, st_ref, *, nchunk):
    C = RWKV_CHUNK
    N = RWKV_HEAD_DIM
    T = nchunk * C
    first = pl.program_id(1) == 0

    @pl.when(first)
    def _():
        st_ref[...] = jnp.zeros(st_ref.shape, F32)

    prev = jnp.where(first, 0.0, hprev_ref[7:8, :])
    r, k, v, lw, a, g = _rwkv_premix(h_ref[...], prev, mu_ref[...], w0_ref[...], wup_ref[...],
                                     a0_ref[...], aup_ref[...], gup_ref[...])

    ri = lax.broadcasted_iota(jnp.int32, (C, C), 0)
    ci = lax.broadcasted_iota(jnp.int32, (C, C), 1)
    incl = ci <= ri
    strict = ci < ri

    rt = lax.broadcasted_iota(jnp.int32, (T, T), 0)
    ct = lax.broadcasted_iota(jnp.int32, (T, T), 1)
    tri = ((ct <= rt) & (ct >= (rt // C) * C)).astype(BF16)
    lw_hi, lw_lo = _split2(lw)
    cum = _dot(tri, lw_hi) + _dot(tri, lw_lo)
    cls = [cum[(c + 1) * C - 1:(c + 1) * C, :] for c in range(nchunk)]
    cl_rows = jnp.concatenate([jnp.broadcast_to(t, (C, RWKV_W)) for t in cls], axis=0)
    e_in = jnp.exp(cum)
    e_ex = jnp.exp(cum - lw)
    e_neg = jnp.exp(-cum)
    e_tail = jnp.exp(cl_rows - cum)
    pcs = [jnp.exp(t) for t in cls]

    kk = k * kk_ref[...]
    kkn = kk * lax.rsqrt(jnp.maximum(_head_sums(kk * kk, e_ref), 1e-24))
    k2 = k * (1.0 + (a - 1.0) * ka_ref[...])
    bvec = kkn * a
    At_f = -kkn * e_ex
    Rt_f = r * e_in
    AtB = At_f.astype(BF16)
    RtB = Rt_f.astype(BF16)
    BtB = (bvec * e_neg).astype(BF16)
    KtB = (k2 * e_neg).astype(BF16)
    BpB = (bvec * e_tail).astype(BF16)
    KpB = (k2 * e_tail).astype(BF16)
    vB = v.astype(BF16)

    HD = range(RWKV_HEADS)
    hsl = [slice(hd * N, (hd + 1) * N) for hd in HD]
    sls = [(slice(c * C, (c + 1) * C), hs) for c in range(nchunk) for hs in hsl]
    At = [At_f[rs, hs] for rs, hs in sls]
    Rt = [Rt_f[rs, hs] for rs, hs in sls]
    vb = [vB[rs, hs] for rs, hs in sls]
    Bt = [BtB[rs, hs] for rs, hs in sls]
    Kt = [KtB[rs, hs] for rs, hs in sls]
    Bp = [BpB[rs, hs] for rs, hs in sls]
    Kp = [KpB[rs, hs] for rs, hs in sls]

    AR = [jnp.concatenate([AtB[rs, hs], RtB[rs, hs]], axis=0) for rs, hs in sls]
    BK = [jnp.concatenate([x, y], axis=0) for x, y in zip(Bt, Kt)]
    G = [_dot_nt(x, y) for x, y in zip(AR, BK)]
    r2 = lax.broadcasted_iota(jnp.int32, (C, 2 * C), 0)
    c2 = lax.broadcasted_iota(jnp.int32, (C, 2 * C), 1)
    incl2 = jnp.where(c2 >= C, c2 - C, c2) <= r2
    P = [jnp.where(strict, t[:C, :C], 0.0) for t in G]
    Lak = [jnp.where(strict, t[:C, C:], 0.0).astype(BF16) for t in G]
    L2 = [jnp.where(incl2, t[C:], 0.0).astype(BF16) for t in G]

    X = [jnp.concatenate([x, _dot(y, z)], axis=1) for x, y, z in zip(At, Lak, vb)]
    steps = C.bit_length() - 1
    for it in range(steps):
        Pb = [t.astype(BF16) for t in P]
        Xb = [x.astype(BF16) for x in X]
        if it + 1 < steps:
            PX = [_dot(p, jnp.concatenate([x, p], axis=1)) for p, x in zip(Pb, Xb)]
            X = [x + t[:, :2 * N] for x, t in zip(X, PX)]
            P = [t[:, 2 * N:] for t in PX]
        else:
            X = [x + _dot(p, xb) for x, p, xb in zip(X, Pb, Xb)]
    Xb = [x.astype(BF16) for x in X]

    zpad = jnp.zeros((C, N), BF16)
    RHS = [jnp.concatenate([x, jnp.concatenate([zpad, z], axis=1)], axis=0)
           for x, z in zip(Xb, vb)]
    LX = [_dot(x, y) for x, y in zip(L2, RHS)]
    Qp = [(x + y[:, :N]).astype(BF16) for x, y in zip(Rt, LX)]
    Y0 = [y[:, N:] for y in LX]

    XB = [_dot_tn(x, b) for x, b in zip(Xb, Bp)]
    Mw = [t[:N].astype(BF16) for t in XB]
    Nn = [t[N:] + _dot_tn(x, y) for t, x, y in zip(XB, vb, Kp)]

    S = [st_ref[hd] for hd in HD]
    y_rows = []
    for c in range(nchunk):
        ch = [c * RWKV_HEADS + hd for hd in HD]
        Ssp = [_split2(s) for s in S]
        ys = [_dot_nt(Qp[j], sp[0]) + Y0[j] for j, sp in zip(ch, Ssp)]
        y_rows.append(jnp.concatenate(ys, axis=1))
        SM = [_dot(jnp.concatenate(sp, axis=0), Mw[j]) for sp, j in zip(Ssp, ch)]
        S = [s * pcs[c][:, hs] + sm[:N] + sm[N:] + Nn[j]
             for s, sm, j, hs in zip(S, SM, ch, hsl)]
    for hd in HD:
        st_ref[hd] = S[hd]

    y = jnp.concatenate(y_rows, axis=0)
    inv_n = 1.0 / N
    yc = y - _head_sums(y, e_ref) * inv_n
    var = _head_sums(yc * yc, e_ref) * inv_n
    yn = yc * lax.rsqrt(var + RWKV_GN_EPS) * gw_ref[...] + gb_ref[...]
    bonus = _head_sums(r * k2 * rk_ref[...], e_ref) * v
    o_ref[...] = ((yn + bonus) * g).astype(o_ref.dtype)


def _rwkv_mix(h, mu, w0, wup, a0, aup, gup, k_k, k_a, r_k, gn_w, gn_b, B, S):
    nchunk = RWKV_CHUNKS_PER_STEP
    T = nchunk * RWKV_CHUNK
    nc = S // T
    row = lambda b, c: (b * nc + c, 0)
    prev_row = lambda b, c: (jnp.maximum((b * nc + c) * (T // 8) - 1, 0), 0)
    const = lambda b, c: (0, 0)
    vec = pl.BlockSpec((1, RWKV_W), const)
    hid = jnp.arange(2 * LANE, dtype=jnp.int32) // RWKV_HEAD_DIM
    head_ones = (hid[:, None] == hid[None, :]).astype(BF16)
    return pl.pallas_call(
        functools.partial(_rwkv_scan_kernel, nchunk=nchunk),
        name="rwkv_scan",
        out_shape=jax.ShapeDtypeStruct((B * S, RWKV_W), BF16),
        grid=(B, nc),
        in_specs=[pl.BlockSpec((T, RWKV_COLS), row),
                  pl.BlockSpec((8, RWKV_COLS), prev_row),
                  pl.BlockSpec((1, RWKV_COLS), const),
                  vec,
                  pl.BlockSpec((DECAY_LORA, RWKV_W), const),
                  vec,
                  pl.BlockSpec((ICLR_LORA, RWKV_W), const),
                  pl.BlockSpec((GATE_LORA, RWKV_W), const),
                  vec, vec, vec, vec, vec,
                  pl.BlockSpec((2 * LANE, 2 * LANE), const)],
        out_specs=pl.BlockSpec((T, RWKV_W), row),
        scratch_shapes=[pltpu.VMEM((RWKV_HEADS, RWKV_HEAD_DIM, RWKV_HEAD_DIM), F32)],
        compiler_params=_cparams(("parallel", "arbitrary")),
    )(h, h, mu, w0, wup, a0, aup, gup, k_k, k_a, r_k, gn_w, gn_b, head_ones)


def _ret_kernel(h_ref, cos_ref, sin_ref, dmat_ref, qd_ref, kd_ref, cd_ref, gw_ref, gb_ref,
                o_ref, R_ref, *, nchunk):
    C = RET_CHUNK
    d = RET_HEAD_DIM
    H = RET_HEADS
    W = RET_W

    @pl.when(pl.program_id(1) == 0)
    def _():
        R_ref[...] = jnp.zeros(R_ref.shape, F32)

    pairs = [(c, hd) for c in range(nchunk) for hd in range(H)]
    rows = lambda c: slice(c * C, (c + 1) * C)
    cols = lambda part, hd: slice(part * W + hd * d, part * W + (hd + 1) * d)
    cos = [cos_ref[rows(c), :] for c in range(nchunk)]
    sin = [sin_ref[rows(c), :] for c in range(nchunk)]
    q = [h_ref[rows(c), cols(0, hd)] for c, hd in pairs]
    k = [h_ref[rows(c), cols(1, hd)] for c, hd in pairs]
    qr = [x * cos[c] + pltpu.roll(x, d // 2, 1) * sin[c] for x, (c, hd) in zip(q, pairs)]
    kr = [(x * cos[c] + pltpu.roll(x, d // 2, 1) * sin[c]) * (d ** -0.5)
          for x, (c, hd) in zip(k, pairs)]
    vb = [h_ref[rows(c), cols(2, hd)].astype(BF16) for c, hd in pairs]
    qb = [x.astype(BF16) for x in qr]
    kb = [x.astype(BF16) for x in kr]
    kdb = [(x * kd_ref[hd]).astype(BF16) for x, (c, hd) in zip(kr, pairs)]
    inner = [(_dot_nt(x, y) * dmat_ref[hd]).astype(BF16) for x, y, (c, hd) in zip(qb, kb, pairs)]
    o_in = [_dot(x, y) for x, y in zip(inner, vb)]
    kv = [_dot_tn(x, y) for x, y in zip(kdb, vb)]

    R = [R_ref[hd] for hd in range(H)]
    outs = []
    for c in range(nchunk):
        ch = [c * H + hd for hd in range(H)]
        Rsp = [_split2(x) for x in R]
        outs += [o_in[j] + (_dot(qb[j], sp[0]) + _dot(qb[j], sp[1])) * qd_ref[hd]
                 for hd, (j, sp) in enumerate(zip(ch, Rsp))]
        R = [cd_ref[hd][0:1, :] * x + kv[j] for hd, (x, j) in enumerate(zip(R, ch))]
    for hd in range(H):
        R_ref[hd] = R[hd]

    mus = [jnp.mean(o, axis=1, keepdims=True) for o in outs]
    ocs = [o - mu for o, mu in zip(outs, mus)]
    rstd = [lax.rsqrt(jnp.mean(oc * oc, axis=1, keepdims=True) + RET_GN_EPS) for oc in ocs]
    for oc, rs, (c, hd) in zip(ocs, rstd, pairs):
        gsl = slice(hd * d, (hd + 1) * d)
        on = oc * rs * gw_ref[:, gsl] + gb_ref[:, gsl]
        gate = h_ref[rows(c), cols(3, hd)]
        o_ref[rows(c), gsl] = (gate * _sigmoid(gate) * on).astype(o_ref.dtype)


def _ret_tables(S):
    C, d, H = RET_CHUNK, RET_HEAD_DIM, RET_HEADS
    half = d // 2
    inv = 1.0 / (ROPE_BASE ** (jnp.arange(half, dtype=F32) / half))
    ang = jnp.arange(S, dtype=F32)[:, None] * inv[None, :]
    cos = jnp.cos(ang)
    sin = jnp.sin(ang)
    cos2 = jnp.concatenate([cos, cos], axis=1)
    sin2 = jnp.concatenate([-sin, sin], axis=1)
    log_g = jnp.log(1.0 - 2.0 ** (-5.0 - jnp.arange(H, dtype=F32)))
    pos = jnp.arange(C, dtype=F32)
    rel = pos[:, None] - pos[None, :]
    dmat = jnp.where(rel >= 0, jnp.exp(log_g[:, None, None] * jnp.maximum(rel, 0.0)), 0.0)
    qd = jnp.broadcast_to(jnp.exp(log_g[:, None] * (pos + 1.0))[..., None], (H, C, d))
    kd = jnp.broadcast_to(jnp.exp(log_g[:, None] * (C - 1.0 - pos))[..., None], (H, C, d))
    cd = jnp.broadcast_to(jnp.exp(log_g * C)[:, None, None], (H, 8, d))
    return cos2, sin2, dmat, qd, kd, cd


def _retention(h, tables, gn_w, gn_b, B, S):
    cos2, sin2, dmat, qd, kd, cd = tables
    C = RET_CHUNK
    T = min(512, S)
    nt = S // T
    H = RET_HEADS
    tab = pl.BlockSpec((T, LANE), lambda b, t: (t, 0))
    per_head = lambda rows: pl.BlockSpec((H, rows, LANE), lambda b, t: (0, 0, 0))
    vec = pl.BlockSpec((1, RET_W), lambda b, t: (0, 0))
    return pl.pallas_call(
        functools.partial(_ret_kernel, nchunk=T // C),
        name="retention",
        out_shape=jax.ShapeDtypeStruct((B * S, RET_W), BF16),
        grid=(B, nt),
        in_specs=[pl.BlockSpec((T, RET_COLS), lambda b, t: (b * nt + t, 0)),
                  tab, tab, per_head(C), per_head(C), per_head(C), per_head(8), vec, vec],
        out_specs=pl.BlockSpec((T, RET_W), lambda b, t: (b * nt + t, 0)),
        scratch_shapes=[pltpu.VMEM((H, RET_HEAD_DIM, RET_HEAD_DIM), F32)],
        compiler_params=_cparams(("parallel", "arbitrary")),
    )(h, cos2, sin2, dmat, qd, kd, cd, gn_w, gn_b)


def _outproj_kernel(x_ref, yf_ref, yr_ref, yt_ref, w_ref, g_ref, b_ref, o_ref):
    r0, r1 = FOX_W, FOX_W + RWKV_W
    y = (_dot(yf_ref[...], w_ref[:r0, :]) + _dot(yr_ref[...], w_ref[r0:r1, :])
         + _dot(yt_ref[...], w_ref[r1:, :]))
    o_ref[...] = _layer_norm_rows(ALPHA * x_ref[...] + y, g_ref[...], b_ref[...])


def _out_proj_ln(x2d, y_fox, y_rwkv, y_ret, w_out, l, g, b):
    M, D = x2d.shape
    tm = min(512, M)
    row = lambda i: (i, 0)
    const = lambda i: (0, 0)
    return pl.pallas_call(
        _outproj_kernel,
        name="out_proj_ln",
        out_shape=jax.ShapeDtypeStruct((M, D), F32),
        grid=(M // tm,),
        in_specs=[pl.BlockSpec((tm, D), row), pl.BlockSpec((tm, FOX_W), row),
                  pl.BlockSpec((tm, RWKV_W), row), pl.BlockSpec((tm, RET_W), row),
                  pl.BlockSpec((None, D, D), lambda i: (l, 0, 0)),
                  pl.BlockSpec((1, D), const), pl.BlockSpec((1, D), const)],
        out_specs=pl.BlockSpec((tm, D), row),
        compiler_params=_cparams(("parallel",)),
    )(x2d, y_fox, y_rwkv, y_ret, w_out, g, b)


def _ffn_kernel(x_ref, wu_ref, wd_ref, g_ref, b_ref, o_ref, ob_ref, acc_ref, xb_ref):
    f = pl.program_id(1)

    @pl.when(f == 0)
    def _():
        x = x_ref[...]
        xb_ref[...] = x.astype(BF16)
        acc_ref[...] = ALPHA * x

    hid = jnp.maximum(_dot(xb_ref[...], wu_ref[...]), 0.0)
    acc_ref[...] += _dot((hid * hid).astype(BF16), wd_ref[...])

    @pl.when(f == pl.num_programs(1) - 1)
    def _():
        y = _layer_norm_rows(acc_ref[...], g_ref[...], b_ref[...])
        o_ref[...] = y
        ob_ref[...] = y.astype(BF16)


def _ffn_ln(x2d, w_up, w_down, l, g, b):
    M, D = x2d.shape
    F = w_up.shape[2]
    tm = min(512, M)
    tf = 1024
    return pl.pallas_call(
        _ffn_kernel,
        name="ffn_ln",
        out_shape=(jax.ShapeDtypeStruct((M, D), F32), jax.ShapeDtypeStruct((M, D), BF16)),
        grid=(M // tm, F // tf),
        in_specs=[pl.BlockSpec((tm, D), lambda i, f: (i, 0)),
                  pl.BlockSpec((None, D, tf), lambda i, f: (l, 0, f)),
                  pl.BlockSpec((None, tf, D), lambda i, f: (l, f, 0)),
                  pl.BlockSpec((1, D), lambda i, f: (0, 0)),
                  pl.BlockSpec((1, D), lambda i, f: (0, 0))],
        out_specs=(pl.BlockSpec((tm, D), lambda i, f: (i, 0)),
                   pl.BlockSpec((tm, D), lambda i, f: (i, 0))),
        scratch_shapes=[pltpu.VMEM((tm, D), F32), pltpu.VMEM((tm, D), BF16)],
        compiler_params=_cparams(("parallel", "arbitrary")),
    )(x2d, w_up, w_down, g, b)


def _permute_w_in(w_in):
    L, D, _ = w_in.shape
    w_in = w_in.astype(BF16)
    w_fox = w_in[:, :, :FOX_COLS]
    w_rwkv = w_in[:, :, FOX_COLS:FOX_COLS + RWKV_COLS]
    w_ret = w_in[:, :, FOX_COLS + RWKV_COLS:]
    w_ff = w_fox[:, :, 3 * FOX_W:]
    pad = jnp.zeros((L, D, LANE - FOX_HEADS), BF16)
    w_q = w_fox[:, :, :3 * FOX_W]
    w_rw = jnp.concatenate([w_rwkv, w_ff, pad], axis=-1)
    return w_q, w_rw, w_ret


def _layer(x2d, xb, l, p, tables, B, S):
    hq = _in_proj(xb, p['w_q'], l, p['q_scale'])
    h = _in_proj(xb, p['w_rw'], l)
    ht = _in_proj(xb, p['w_ret'], l)
    c = _fox_c(h, p['fox_bias'], B, S)
    c3 = c[:, :FOX_HEADS, :].reshape(B * FOX_HEADS, 1, S)
    y_fox = _fox_attention(hq, c3, B, S)
    y_rwkv = _rwkv_mix(h, p['mu'], p['w0'], p['w_up_lora'], p['a0'], p['a_up'], p['g_up'],
                       p['k_k'], p['k_a'], p['r_k'], p['rgn_w'], p['rgn_b'], B, S)
    y_ret = _retention(ht, tables, p['tgn_w'], p['tgn_b'], B, S)
    x2d = _out_proj_ln(x2d, y_fox, y_rwkv, y_ret, p['w_out'], l, p['ln1_g'], p['ln1_b'])
    return _ffn_ln(x2d, p['w_up'], p['w_down'], l, p['ln2_g'], p['ln2_b'])


def kernel(x, w_in, fox_forget_bias, rwkv_mu, rwkv_w0, rwkv_w_up, rwkv_a0, rwkv_a_up, rwkv_g_up,
           rwkv_k_k, rwkv_k_a, rwkv_r_k, rwkv_gn_w, rwkv_gn_b, ret_gn_w, ret_gn_b, w_out, ln1_g,
           ln1_b, w_up, w_down, ln2_g, ln2_b):
    B, S, D = x.shape
    L = w_in.shape[0]
    w_q, w_rw, w_ret = _permute_w_in(w_in)
    q_scale = jnp.concatenate([jnp.full((1, FOX_W), FOX_Q_SCALE, F32),
                               jnp.ones((1, 2 * FOX_W), F32)], axis=1)
    w_out_b = w_out.astype(BF16)
    w_up_b = w_up.astype(BF16)
    w_down_b = w_down.astype(BF16)
    bias_pad = jnp.pad(fox_forget_bias, ((0, 0), (0, LANE - FOX_HEADS)))
    tables = _ret_tables(S)
    row = lambda t: t.reshape(1, -1)
    x2d = x.reshape(B * S, D)
    xb = x2d
    for l in range(L):
        p = {
            'w_rw': w_rw, 'w_ret': w_ret, 'w_q': w_q, 'q_scale': q_scale,
            'fox_bias': row(bias_pad[l]),
            'mu': row(rwkv_mu[l]), 'w0': row(rwkv_w0[l]), 'w_up_lora': rwkv_w_up[l].astype(BF16),
            'a0': row(rwkv_a0[l]), 'a_up': rwkv_a_up[l].astype(BF16),
            'g_up': rwkv_g_up[l].astype(BF16),
            'k_k': row(rwkv_k_k[l]), 'k_a': row(rwkv_k_a[l]), 'r_k': row(rwkv_r_k[l]),
            'rgn_w': row(rwkv_gn_w[l]), 'rgn_b': row(rwkv_gn_b[l]),
            'tgn_w': row(ret_gn_w[l]), 'tgn_b': row(ret_gn_b[l]),
            'w_out': w_out_b,
            'ln1_g': row(ln1_g[l]), 'ln1_b': row(ln1_b[l]),
            'w_up': w_up_b, 'w_down': w_down_b,
            'ln2_g': row(ln2_g[l]), 'ln2_b': row(ln2_b[l]),
        }
        x2d, xb = _layer(x2d, xb, l, p, tables, B, S)
    return x2d.reshape(B, S, D)
```

```python
import functools

import jax
import jax.numpy as jnp
from jax import lax
from jax.experimental import pallas as pl
from jax.experimental.pallas import tpu as pltpu

F32 = jnp.float32
BF16 = jnp.bfloat16

D_MODEL = 2048
DEPTH = 4
FOX_W = 768
RWKV_W = 640
RET_W = 640
FOX_HEAD_DIM = 128
FOX_HEADS = 6
RWKV_HEAD_DIM = 64
RWKV_HEADS = 10
RET_HEAD_DIM = 128
RET_HEADS = 5
DECAY_LORA = 64
ICLR_LORA = 64
GATE_LORA = 128
D_FF = 4 * D_MODEL
RET_CHUNK = 128
ROPE_BASE = 10000.0
LN_EPS = 1e-5
RWKV_GN_EPS = 64e-5
RET_GN_EPS = 1e-5
ALPHA = (2 * DEPTH) ** 0.25

FOX_COLS = 3 * FOX_W + FOX_HEADS
RWKV_COLS = 3 * RWKV_W + DECAY_LORA + ICLR_LORA + GATE_LORA
RET_COLS = 4 * RET_W

LANE = 128
H_FF0 = RWKV_COLS
LOG2E = 1.4426950408889634
FOX_Q_SCALE = FOX_HEAD_DIM ** -0.5 * LOG2E

FOX_BLOCKS_PER_TRIP = 8
FOX_Q_ROWS = 1024
RWKV_DECAY_SCALE = 0.6065306597126334
RWKV_CHUNK = 64
RWKV_CHUNKS_PER_STEP = 4
NEG_BIG = -1e30

VMEM_LIMIT = 56 * 1024 * 1024


def _cparams(sem):
    return pltpu.CompilerParams(dimension_semantics=sem, vmem_limit_bytes=VMEM_LIMIT)


def _dot(a, b):
    return jnp.dot(a, b, preferred_element_type=F32)


def _dot_nt(a, b):
    return lax.dot_general(a, b, (((1,), (1,)), ((), ())), preferred_element_type=F32)


def _dot_tn(a, b):
    return lax.dot_general(a, b, (((0,), (0,)), ((), ())), preferred_element_type=F32)


def _split3(x):
    hi = x.astype(BF16)
    r1 = x - hi.astype(F32)
    mid = r1.astype(BF16)
    lo = (r1 - mid.astype(F32)).astype(BF16)
    return hi, mid, lo


def _split2(x):
    hi = x.astype(BF16)
    lo = (x - hi.astype(F32)).astype(BF16)
    return hi, lo


def _mask_dot(mask_bf16, x):
    hi, mid, lo = _split3(x)
    return _dot(mask_bf16, hi) + _dot(mask_bf16, mid) + _dot(mask_bf16, lo)


def _softplus(z):
    return jnp.maximum(z, 0.0) + jnp.log(1.0 + jnp.exp(-jnp.abs(z)))


def _sigmoid(z):
    return 1.0 / (1.0 + jnp.exp(-z))


def _layer_norm_rows(y, g, b):
    mu = jnp.mean(y, axis=-1, keepdims=True)
    yc = y - mu
    var = jnp.mean(yc * yc, axis=-1, keepdims=True)
    return yc * lax.rsqrt(var + LN_EPS) * g + b


def _inproj_kernel(x_ref, w_ref, o_ref):
    o_ref[...] = _dot(x_ref[...].astype(BF16), w_ref[...])


def _inproj_scaled_kernel(x_ref, w_ref, s_ref, o_ref):
    o_ref[...] = (_dot(x_ref[...].astype(BF16), w_ref[...]) * s_ref[...]).astype(o_ref.dtype)


def _in_proj(xb, w, l, col_scale=None):
    M, K = xb.shape
    N = w.shape[2]
    tm = min(1024 if xb.dtype == BF16 else 512, M)
    tn = N
    in_specs = [pl.BlockSpec((tm, K), lambda j, i: (i, 0)),
                pl.BlockSpec((None, K, tn), lambda j, i: (l, 0, j))]
    args = [xb, w]
    if col_scale is not None:
        in_specs.append(pl.BlockSpec((1, tn), lambda j, i: (0, j)))
        args.append(col_scale)
    return pl.pallas_call(
        _inproj_kernel if col_scale is None else _inproj_scaled_kernel,
        name="in_proj",
        out_shape=jax.ShapeDtypeStruct((M, N), F32 if col_scale is None else BF16),
        grid=(N // tn, M // tm),
        in_specs=in_specs,
        out_specs=pl.BlockSpec((tm, tn), lambda j, i: (i, j)),
        compiler_params=_cparams(("parallel", "parallel")),
    )(*args)


def _fox_c_kernel(f_ref, bias_ref, c_ref, *, rows):
    S = f_ref.shape[0]
    r = lax.broadcasted_iota(jnp.int32, (rows, rows), 0)
    c = lax.broadcasted_iota(jnp.int32, (rows, rows), 1)
    tri = (c <= r).astype(BF16)

    bias = bias_ref[...]
    local = [_mask_dot(tri, -_softplus(-(f_ref[s:s + rows, :] + bias))) for s in range(0, S, rows)]
    carry = jnp.zeros((1, LANE), F32)
    for j, cs in enumerate(local):
        c_ref[:, j * rows:(j + 1) * rows] = ((cs + carry) * LOG2E).T[0:8, :]
        carry = carry + cs[rows - 1:rows, :]


def _fox_c(h, bias_pad, B, S):
    rows = min(256, S)
    return pl.pallas_call(
        functools.partial(_fox_c_kernel, rows=rows),
        name="fox_c",
        out_shape=jax.ShapeDtypeStruct((B, 8, S), F32),
        grid=(B,),
        in_specs=[pl.BlockSpec((S, LANE), lambda b: (b, H_FF0 // LANE)),
                  pl.BlockSpec((1, LANE), lambda b: (0, 0))],
        out_specs=pl.BlockSpec((None, 8, S), lambda b: (b, 0, 0)),
        compiler_params=_cparams(("parallel",)),
    )(h, bias_pad)


def _fox_attn_kernel(q_ref, qn_ref, k_ref, v_ref, c_ref, o_ref, s0_ref, *, tq, tk):
    i = pl.program_id(2)
    d = q_ref.shape[1]
    q0 = pl.multiple_of(i * tq, tq)
    c0 = c_ref[:, pl.ds(q0, tq)][:, 0:1]
    q = q_ref[...]

    def scores(start, row_lo=0):
        bias = c0 - c_ref[:, pl.ds(start, tk)]
        return _dot_nt(q[row_lo:], k_ref[pl.ds(start, tk), :]) + bias

    def next_step_scores():
        qn0 = pl.multiple_of(jnp.minimum(i + 1, pl.num_programs(2) - 1) * tq, tq)
        cn = c_ref[:, pl.ds(qn0, tq)][:, 0:1]
        return _dot_nt(qn_ref[...], k_ref[0:tk, :]) + (cn - c_ref[:, 0:tk])

    ones = jnp.ones((tk, d), BF16)

    def update(start, sc, carry, masked, row_lo=0):
        m_all, acc_all = carry
        m, acc = m_all[row_lo:], acc_all[row_lo:]
        if masked:
            row = lax.broadcasted_iota(jnp.int32, sc.shape, 0) + (q0 + row_lo)
            col = lax.broadcasted_iota(jnp.int32, sc.shape, 1) + start
            sc = jnp.where(col <= row, sc, NEG_BIG)
        m_new = jnp.maximum(m, jnp.max(sc, axis=1, keepdims=True))
        alpha = jnp.exp2(m - m_new)
        p = jnp.exp2((sc - m_new).astype(BF16))
        vb = jnp.concatenate([v_ref[pl.ds(start, tk), :], ones], axis=1)
        acc = alpha * acc + _dot(p, vb)
        if row_lo:
            m_new = jnp.concatenate([m_all[:row_lo], m_new], axis=0)
            acc = jnp.concatenate([acc_all[:row_lo], acc], axis=0)
        return m_new, acc

    U = FOX_BLOCKS_PER_TRIP

    D = tq // tk

    def walk(base, n, carry, last):
        skip = lambda u: max(u - (n - D), 0) * tk if last else 0
        s_cur = s0_ref[...]
        for u in range(n):
            start = pl.multiple_of(base + u * tk, tk)
            final = last and u == n - 1
            s_next = (next_step_scores() if final
                      else scores(pl.multiple_of(start + tk, tk), skip(u + 1)))
            carry = update(start, s_cur, carry, last and u >= n - D, skip(u))
            s_cur = s_next
        s0_ref[...] = s_cur
        return carry

    @pl.when(i == 0)
    def _():
        s0_ref[...] = scores(0)

    init = (jnp.full((tq, 1), NEG_BIG, F32), jnp.zeros((tq, 2 * d), F32))
    n_full = i * D
    trips = n_full // U
    carry = lax.fori_loop(
        0, trips, lambda t, c: walk(pl.multiple_of(t * (U * tk), tk), U, c, False), init)

    base = pl.multiple_of(trips * (U * tk), tk)
    rems = list(range(0, U, D))
    tails = [functools.partial(walk, base, r + D, last=True) for r in rems]
    m, acc = lax.switch((n_full - trips * U) // D, tails, carry)
    o_ref[...] = (acc[:, :d] / acc[:, d:]).astype(o_ref.dtype)


def _fox_attention(hq, c3, B, S):
    tk = min(512, S)
    tq = min(FOX_Q_ROWS, S)
    nq = S // tq
    nh = FOX_HEADS
    return pl.pallas_call(
        functools.partial(_fox_attn_kernel, tq=tq, tk=tk),
        name="fox_attn",
        scratch_shapes=[pltpu.VMEM((tq, tk), F32)],
        out_shape=jax.ShapeDtypeStruct((B * S, FOX_W), BF16),
        grid=(B, FOX_HEADS, nq),
        in_specs=[pl.BlockSpec((tq, LANE), lambda b, hd, i: (b * nq + i, hd)),
                  pl.BlockSpec((tq, LANE),
                               lambda b, hd, i: (b * nq + jnp.minimum(i + 1, nq - 1), hd)),
                  pl.BlockSpec((S, LANE), lambda b, hd, i: (b, nh + hd)),
                  pl.BlockSpec((S, LANE), lambda b, hd, i: (b, 2 * nh + hd)),
                  pl.BlockSpec((None, 1, S), lambda b, hd, i: (b * nh + hd, 0, 0))],
        out_specs=pl.BlockSpec((tq, LANE), lambda b, hd, i: (b * nq + i, hd)),
        compiler_params=_cparams(("parallel", "parallel", "arbitrary")),
    )(hq, hq, hq, hq, c3)


def _rwkv_premix(h, prev, mu, w0, wup, a0, aup, gup):
    rows = lax.broadcasted_iota(jnp.int32, h.shape, 0)
    hp = jnp.where(rows == 0, prev, pltpu.roll(h, 1, 0))
    hs = h + (hp - h) * mu
    W3 = 3 * RWKV_W
    r = hs[:, :RWKV_W]
    k = hs[:, RWKV_W:2 * RWKV_W]
    v = hs[:, 2 * RWKV_W:W3]
    wd = hs[:, W3:W3 + DECAY_LORA]
    ad = hs[:, W3 + DECAY_LORA:W3 + DECAY_LORA + ICLR_LORA]
    gd = hs[:, W3 + DECAY_LORA + ICLR_LORA:]
    z = w0 + _dot(jnp.tanh(wd).astype(BF16), wup)
    lw = -RWKV_DECAY_SCALE * _sigmoid(z)
    a = _sigmoid(a0 + _dot(ad.astype(BF16), aup))
    g = _dot(_sigmoid(gd).astype(BF16), gup)
    return r, k, v, lw, a, g


def _head_sums(x, e_ref):
    hi, lo = _split2(x)
    e = e_ref[...]
    e_last = e[:LANE, :LANE]
    outs = []
    for c0 in range(0, RWKV_W, 2 * LANE):
        w = min(2 * LANE, RWKV_W - c0)
        ee = e if w == 2 * LANE else e_last
        outs.append(_dot(hi[:, c0:c0 + w], ee) + _dot(lo[:, c0:c0 + w], ee))
    return jnp.concatenate(outs, axis=1)


def _rwkv_scan_kernel(h_ref, hprev_ref, mu_ref, w0_ref, wup_ref, a0_ref, aup_ref, gup_ref,
                      kk_ref, ka_ref, rk_ref, gw_ref, gb_ref, e_ref, o_ref, st_ref, *, nchunk):
    C = RWKV_CHUNK
    N = RWKV_HEAD_DIM
    T = nchunk * C
    first = pl.program_id(1) == 0

    @pl.when(first)
    def _():
        st_ref[...] = jnp.zeros(st_ref.shape, F32)

    prev = jnp.where(first, 0.0, hprev_ref[7:8, :])
    r, k, v, lw, a, g = _rwkv_premix(h_ref[...], prev, mu_ref[...], w0_ref[...], wup_ref[...],
                                     a0_ref[...], aup_ref[...], gup_ref[...])

    ri = lax.broadcasted_iota(jnp.int32, (C, C), 0)
    ci = lax.broadcasted_iota(jnp.int32, (C, C), 1)
    incl = ci <= ri
    strict = ci < ri

    rt = lax.broadcasted_iota(jnp.int32, (T, T), 0)
    ct = lax.broadcasted_iota(jnp.int32, (T, T), 1)
    tri = ((ct <= rt) & (ct >= (rt // C) * C)).astype(BF16)
    lw_hi, lw_lo = _split2(lw)
    cum = _dot(tri, lw_hi) + _dot(tri, lw_lo)
    cls = [cum[(c + 1) * C - 1:(c + 1) * C, :] for c in range(nchunk)]
    cl_rows = jnp.concatenate([jnp.broadcast_to(t, (C, RWKV_W)) for t in cls], axis=0)
    e_in = jnp.exp(cum)
    e_ex = jnp.exp(cum - lw)
    e_neg = jnp.exp(-cum)
    e_tail = jnp.exp(cl_rows - cum)
    pcs = [jnp.exp(t) for t in cls]

    kk = k * kk_ref[...]
    kkn = kk * lax.rsqrt(jnp.maximum(_head_sums(kk * kk, e_ref), 1e-24))
    k2 = k * (1.0 + (a - 1.0) * ka_ref[...])
    bvec = kkn * a
    At_f = -kkn * e_ex
    Rt_f = r * e_in
    AtB = At_f.astype(BF16)
    RtB = Rt_f.astype(BF16)
    BtB = (bvec * e_neg).astype(BF16)
    KtB = (k2 * e_neg).astype(BF16)
    BpB = (bvec * e_tail).astype(BF16)
    KpB = (k2 * e_tail).astype(BF16)
    vB = v.astype(BF16)

    HD = range(RWKV_HEADS)
    hsl = [slice(hd * N, (hd + 1) * N) for hd in HD]
    sls = [(slice(c * C, (c + 1) * C), hs) for c in range(nchunk) for hs in hsl]
    At = [At_f[rs, hs] for rs, hs in sls]
    Rt = [Rt_f[rs, hs] for rs, hs in sls]
    vb = [vB[rs, hs] for rs, hs in sls]
    Bt = [BtB[rs, hs] for rs, hs in sls]
    Kt = [KtB[rs, hs] for rs, hs in sls]
    Bp = [BpB[rs, hs] for rs, hs in sls]
    Kp = [KpB[rs, hs] for rs, hs in sls]

    AR = [jnp.concatenate([AtB[rs, hs], RtB[rs, hs]], axis=0) for rs, hs in sls]
    BK = [jnp.concatenate([x, y], axis=0) for x, y in zip(Bt, Kt)]
    G = [_dot_nt(x, y) for x, y in zip(AR, BK)]
    r2 = lax.broadcasted_iota(jnp.int32, (C, 2 * C), 0)
    c2 = lax.broadcasted_iota(jnp.int32, (C, 2 * C), 1)
    incl2 = jnp.where(c2 >= C, c2 - C, c2) <= r2
    P = [jnp.where(strict, t[:C, :C], 0.0) for t in G]
    Lak = [jnp.where(strict, t[:C, C:], 0.0).astype(BF16) for t in G]
    L2 = [jnp.where(incl2, t[C:], 0.0).astype(BF16) for t in G]

    X = [jnp.concatenate([x, _dot(y, z)], axis=1) for x, y, z in zip(At, Lak, vb)]
    steps = C.bit_length() - 1
    for it in range(steps):
        Pb = [t.astype(BF16) for t in P]
        Xb = [x.astype(BF16) for x in X]
        if it + 1 < steps:
            PX = [_dot(p, jnp.concatenate([x, p], axis=1)) for p, x in zip(Pb, Xb)]
            X = [x + t[:, :2 * N] for x, t in zip(X, PX)]
            P = [t[:, 2 * N:] for t in PX]
        else:
            X = [x + _dot(p, xb) for x, p, xb in zip(X, Pb, Xb)]
    Xb = [x.astype(BF16) for x in X]

    zpad = jnp.zeros((C, N), BF16)
    RHS = [jnp.concatenate([x, jnp.concatenate([zpad, z], axis=1)], axis=0)
           for x, z in zip(Xb, vb)]
    LX = [_dot(x, y) for x, y in zip(L2, RHS)]
    Qp = [(x + y[:, :N]).astype(BF16) for x, y in zip(Rt, LX)]
    Y0 = [y[:, N:] for y in LX]

    XB = [_dot_tn(x, b) for x, b in zip(Xb, Bp)]
    Mw = [t[:N].astype(BF16) for t in XB]
    Nn = [t[N:] + _dot_tn(x, y) for t, x, y in zip(XB, vb, Kp)]

    S = [st_ref[hd] for hd in HD]
    y_rows = []
    for c in range(nchunk):
        ch = [c * RWKV_HEADS + hd for hd in HD]
        Ssp = [_split2(s) for s in S]
        ys = [_dot_nt(Qp[j], sp[0]) + Y0[j] for j, sp in zip(ch, Ssp)]
        y_rows.append(jnp.concatenate(ys, axis=1))
        SM = [_dot(jnp.concatenate(sp, axis=0), Mw[j]) for sp, j in zip(Ssp, ch)]
        S = [s * pcs[c][:, hs] + sm[:N] + sm[N:] + Nn[j]
             for s, sm, j, hs in zip(S, SM, ch, hsl)]
    for hd in HD:
        st_ref[hd] = S[hd]

    y = jnp.concatenate(y_rows, axis=0)
    inv_n = 1.0 / N
    yc = y - _head_sums(y, e_ref) * inv_n
    var = _head_sums(yc * yc, e_ref) * inv_n
    yn = yc * lax.rsqrt(var + RWKV_GN_EPS) * gw_ref[...] + gb_ref[...]
    bonus = _head_sums(r * k2 * rk_ref[...], e_ref) * v
    o_ref[...] = ((yn + bonus) * g).astype(o_ref.dtype)


def _rwkv_mix(h, mu, w0, wup, a0, aup, gup, k_k, k_a, r_k, gn_w, gn_b, B, S):
    nchunk = RWKV_CHUNKS_PER_STEP
    T = nchunk * RWKV_CHUNK
    nc = S // T
    row = lambda b, c: (b * nc + c, 0)
    prev_row = lambda b, c: (jnp.maximum((b * nc + c) * (T // 8) - 1, 0), 0)
    const = lambda b, c: (0, 0)
    vec = pl.BlockSpec((1, RWKV_W), const)
    hid = jnp.arange(2 * LANE, dtype=jnp.int32) // RWKV_HEAD_DIM
    head_ones = (hid[:, None] == hid[None, :]).astype(BF16)
    return pl.pallas_call(
        functools.partial(_rwkv_scan_kernel, nchunk=nchunk),
        name="rwkv_scan",
        out_shape=jax.ShapeDtypeStruct((B * S, RWKV_W), BF16),
        grid=(B, nc),
        in_specs=[pl.BlockSpec((T, RWKV_COLS), row),
                  pl.BlockSpec((8, RWKV_COLS), prev_row),
                  pl.BlockSpec((1, RWKV_COLS), const),
                  vec,
                  pl.BlockSpec((DECAY_LORA, RWKV_W), const),
                  vec,
                  pl.BlockSpec((ICLR_LORA, RWKV_W), const),
                  pl.BlockSpec((GATE_LORA, RWKV_W), const),
                  vec, vec, vec, vec, vec,
                  pl.BlockSpec((2 * LANE, 2 * LANE), const)],
        out_specs=pl.BlockSpec((T, RWKV_W), row),
        scratch_shapes=[pltpu.VMEM((RWKV_HEADS, RWKV_HEAD_DIM, RWKV_HEAD_DIM), F32)],
        compiler_params=_cparams(("parallel", "arbitrary")),
    )(h, h, mu, w0, wup, a0, aup, gup, k_k, k_a, r_k, gn_w, gn_b, head_ones)


def _ret_kernel(h_ref, cos_ref, sin_ref, dmat_ref, qd_ref, kd_ref, cd_ref, gw_ref, gb_ref,
                o_ref, R_ref, *, nchunk):
    C = RET_CHUNK
    d = RET_HEAD_DIM
    H = RET_HEADS
    W = RET_W

    @pl.when(pl.program_id(1) == 0)
    def _():
        R_ref[...] = jnp.zeros(R_ref.shape, F32)

    pairs = [(c, hd) for c in range(nchunk) for hd in range(H)]
    rows = lambda c: slice(c * C, (c + 1) * C)
    cols = lambda part, hd: slice(part * W + hd * d, part * W + (hd + 1) * d)
    cos = [cos_ref[rows(c), :] for c in range(nchunk)]
    sin = [sin_ref[rows(c), :] for c in range(nchunk)]
    q = [h_ref[rows(c), cols(0, hd)] for c, hd in pairs]
    k = [h_ref[rows(c), cols(1, hd)] for c, hd in pairs]
    qr = [x * cos[c] + pltpu.roll(x, d // 2, 1) * sin[c] for x, (c, hd) in zip(q, pairs)]
    kr = [(x * cos[c] + pltpu.roll(x, d // 2, 1) * sin[c]) * (d ** -0.5)
          for x, (c, hd) in zip(k, pairs)]
    vb = [h_ref[rows(c), cols(2, hd)].astype(BF16) for c, hd in pairs]
    qb = [x.astype(BF16) for x in qr]
    kb = [x.astype(BF16) for x in kr]
    kdb = [(x * kd_ref[hd]).astype(BF16) for x, (c, hd) in zip(kr, pairs)]
    inner = [(_dot_nt(x, y) * dmat_ref[hd]).astype(BF16) for x, y, (c, hd) in zip(qb, kb, pairs)]
    o_in = [_dot(x, y) for x, y in zip(inner, vb)]
    kv = [_dot_tn(x, y) for x, y in zip(kdb, vb)]

    R = [R_ref[hd] for hd in range(H)]
    outs = []
    for c in range(nchunk):
        ch = [c * H + hd for hd in range(H)]
        Rsp = [_split2(x) for x in R]
        outs += [o_in[j] + (_dot(qb[j], sp[0]) + _dot(qb[j], sp[1])) * qd_ref[hd]
                 for hd, (j, sp) in enumerate(zip(ch, Rsp))]
        R = [cd_ref[hd][0:1, :] * x + kv[j] for hd, (x, j) in enumerate(zip(R, ch))]
    for hd in range(H):
        R_ref[hd] = R[hd]

    mus = [jnp.mean(o, axis=1, keepdims=True) for o in outs]
    ocs = [o - mu for o, mu in zip(outs, mus)]
    rstd = [lax.rsqrt(jnp.mean(oc * oc, axis=1, keepdims=True) + RET_GN_EPS) for oc in ocs]
    for oc, rs, (c, hd) in zip(ocs, rstd, pairs):
        gsl = slice(hd * d, (hd + 1) * d)
        on = oc * rs * gw_ref[:, gsl] + gb_ref[:, gsl]
        gate = h_ref[rows(c), cols(3, hd)]
        o_ref[rows(c), gsl] = (gate * _sigmoid(gate) * on).astype(o_ref.dtype)


def _ret_tables(S):
    C, d, H = RET_CHUNK, RET_HEAD_DIM, RET_HEADS
    half = d // 2
    inv = 1.0 / (ROPE_BASE ** (jnp.arange(half, dtype=F32) / half))
    ang = jnp.arange(S, dtype=F32)[:, None] * inv[None, :]
    cos = jnp.cos(ang)
    sin = jnp.sin(ang)
    cos2 = jnp.concatenate([cos, cos], axis=1)
    sin2 = jnp.concatenate([-sin, sin], axis=1)
    log_g = jnp.log(1.0 - 2.0 ** (-5.0 - jnp.arange(H, dtype=F32)))
    pos = jnp.arange(C, dtype=F32)
    rel = pos[:, None] - pos[None, :]
    dmat = jnp.where(rel >= 0, jnp.exp(log_g[:, None, None] * jnp.maximum(rel, 0.0)), 0.0)
    qd = jnp.broadcast_to(jnp.exp(log_g[:, None] * (pos + 1.0))[..., None], (H, C, d))
    kd = jnp.broadcast_to(jnp.exp(log_g[:, None] * (C - 1.0 - pos))[..., None], (H, C, d))
    cd = jnp.broadcast_to(jnp.exp(log_g * C)[:, None, None], (H, 8, d))
    return cos2, sin2, dmat, qd, kd, cd


def _retention(h, tables, gn_w, gn_b, B, S):
    cos2, sin2, dmat, qd, kd, cd = tables
    C = RET_CHUNK
    T = min(512, S)
    nt = S // T
    H = RET_HEADS
    tab = pl.BlockSpec((T, LANE), lambda b, t: (t, 0))
    per_head = lambda rows: pl.BlockSpec((H, rows, LANE), lambda b, t: (0, 0, 0))
    vec = pl.BlockSpec((1, RET_W), lambda b, t: (0, 0))
    return pl.pallas_call(
        functools.partial(_ret_kernel, nchunk=T // C),
        name="retention",
        out_shape=jax.ShapeDtypeStruct((B * S, RET_W), BF16),
        grid=(B, nt),
        in_specs=[pl.BlockSpec((T, RET_COLS), lambda b, t: (b * nt + t, 0)),
                  tab, tab, per_head(C), per_head(C), per_head(C), per_head(8), vec, vec],
        out_specs=pl.BlockSpec((T, RET_W), lambda b, t: (b * nt + t, 0)),
        scratch_shapes=[pltpu.VMEM((H, RET_HEAD_DIM, RET_HEAD_DIM), F32)],
        compiler_params=_cparams(("parallel", "arbitrary")),
    )(h, cos2, sin2, dmat, qd, kd, cd, gn_w, gn_b)


def _outproj_kernel(x_ref, yf_ref, yr_ref, yt_ref, w_ref, g_ref, b_ref, o_ref):
    r0, r1 = FOX_W, FOX_W + RWKV_W
    y = (_dot(yf_ref[...], w_ref[:r0, :]) + _dot(yr_ref[...], w_ref[r0:r1, :])
         + _dot(yt_ref[...], w_ref[r1:, :]))
    o_ref[...] = _layer_norm_rows(ALPHA * x_ref[...] + y, g_ref[...], b_ref[...])


def _out_proj_ln(x2d, y_fox, y_rwkv, y_ret, w_out, l, g, b):
    M, D = x2d.shape
    tm = min(512, M)
    row = lambda i: (i, 0)
    const = lambda i: (0, 0)
    return pl.pallas_call(
        _outproj_kernel,
        name="out_proj_ln",
        out_shape=jax.ShapeDtypeStruct((M, D), F32),
        grid=(M // tm,),
        in_specs=[pl.BlockSpec((tm, D), row), pl.BlockSpec((tm, FOX_W), row),
                  pl.BlockSpec((tm, RWKV_W), row), pl.BlockSpec((tm, RET_W), row),
                  pl.BlockSpec((None, D, D), lambda i: (l, 0, 0)),
                  pl.BlockSpec((1, D), const), pl.BlockSpec((1, D), const)],
        out_specs=pl.BlockSpec((tm, D), row),
        compiler_params=_cparams(("parallel",)),
    )(x2d, y_fox, y_rwkv, y_ret, w_out, g, b)


def _ffn_kernel(x_ref, wu_ref, wd_ref, g_ref, b_ref, o_ref, ob_ref, acc_ref, xb_ref):
    f = pl.program_id(1)

    @pl.when(f == 0)
    def _():
        x = x_ref[...]
        xb_ref[...] = x.astype(BF16)
        acc_ref[...] = ALPHA * x

    hid = jnp.maximum(_dot(xb_ref[...], wu_ref[...]), 0.0)
    acc_ref[...] += _dot((hid * hid).astype(BF16), wd_ref[...])

    @pl.when(f == pl.num_programs(1) - 1)
    def _():
        y = _layer_norm_rows(acc_ref[...], g_ref[...], b_ref[...])
        o_ref[...] = y
        ob_ref[...] = y.astype(BF16)


def _ffn_ln(x2d, w_up, w_down, l, g, b):
    M, D = x2d.shape
    F = w_up.shape[2]
    tm = min(512, M)
    tf = 1024
    return pl.pallas_call(
        _ffn_kernel,
        name="ffn_ln",
        out_shape=(jax.ShapeDtypeStruct((M, D), F32), jax.ShapeDtypeStruct((M, D), BF16)),
        grid=(M // tm, F // tf),
        in_specs=[pl.BlockSpec((tm, D), lambda i, f: (i, 0)),
                  pl.BlockSpec((None, D, tf), lambda i, f: (l, 0, f)),
                  pl.BlockSpec((None, tf, D), lambda i, f: (l, f, 0)),
                  pl.BlockSpec((1, D), lambda i, f: (0, 0)),
                  pl.BlockSpec((1, D), lambda i, f: (0, 0))],
        out_specs=(pl.BlockSpec((tm, D), lambda i, f: (i, 0)),
                   pl.BlockSpec((tm, D), lambda i, f: (i, 0))),
        scratch_shapes=[pltpu.VMEM((tm, D), F32), pltpu.VMEM((tm, D), BF16)],
        compiler_params=_cparams(("parallel", "arbitrary")),
    )(x2d, w_up, w_down, g, b)


def _permute_w_in(w_in):
    L, D, _ = w_in.shape
    w_in = w_in.astype(BF16)
    w_fox = w_in[:, :, :FOX_COLS]
    w_rwkv = w_in[:, :, FOX_COLS:FOX_COLS + RWKV_COLS]
    w_ret = w_in[:, :, FOX_COLS + RWKV_COLS:]
    w_ff = w_fox[:, :, 3 * FOX_W:]
    pad = jnp.zeros((L, D, LANE - FOX_HEADS), BF16)
    w_q = w_fox[:, :, :3 * FOX_W]
    w_rw = jnp.concatenate([w_rwkv, w_ff, pad], axis=-1)
    return w_q, w_rw, w_ret


def _layer(x2d, xb, l, p, tables, B, S):
    hq = _in_proj(xb, p['w_q'], l, p['q_scale'])
    h = _in_proj(xb, p['w_rw'], l)
    ht = _in_proj(xb, p['w_ret'], l)
    c = _fox_c(h, p['fox_bias'], B, S)
    c3 = c[:, :FOX_HEADS, :].reshape(B * FOX_HEADS, 1, S)
    y_fox = _fox_attention(hq, c3, B, S)
    y_rwkv = _rwkv_mix(h, p['mu'], p['w0'], p['w_up_lora'], p['a0'], p['a_up'], p['g_up'],
                       p['k_k'], p['k_a'], p['r_k'], p['rgn_w'], p['rgn_b'], B, S)
    y_ret = _retention(ht, tables, p['tgn_w'], p['tgn_b'], B, S)
    x2d = _out_proj_ln(x2d, y_fox, y_rwkv, y_ret, p['w_out'], l, p['ln1_g'], p['ln1_b'])
    return _ffn_ln(x2d, p['w_up'], p['w_down'], l, p['ln2_g'], p['ln2_b'])


def kernel(x, w_in, fox_forget_bias, rwkv_mu, rwkv_w0, rwkv_w_up, rwkv_a0, rwkv_a_up, rwkv_g_up,
           rwkv_k_k, rwkv_k_a, rwkv_r_k, rwkv_gn_w, rwkv_gn_b, ret_gn_w, ret_gn_b, w_out, ln1_g,
           ln1_b, w_up, w_down, ln2_g, ln2_b):
    B, S, D = x.shape
    L = w_in.shape[0]
    w_q, w_rw, w_ret = _permute_w_in(w_in)
    q_scale = jnp.concatenate([jnp.full((1, FOX_W), FOX_Q_SCALE, F32),
                               jnp.ones((1, 2 * FOX_W), F32)], axis=1)
    w_out_b = w_out.astype(BF16)
    w_up_b = w_up.astype(BF16)
    w_down_b = w_down.astype(BF16)
    bias_pad = jnp.pad(fox_forget_bias, ((0, 0), (0, LANE - FOX_HEADS)))
    tables = _ret_tables(S)
    row = lambda t: t.reshape(1, -1)
    x2d = x.reshape(B * S, D)
    xb = x2d
    for l in range(L):
        p = {
            'w_rw': w_rw, 'w_ret': w_ret, 'w_q': w_q, 'q_scale': q_scale,
            'fox_bias': row(bias_pad[l]),
            'mu': row(rwkv_mu[l]), 'w0': row(rwkv_w0[l]), 'w_up_lora': rwkv_w_up[l].astype(BF16),
            'a0': row(rwkv_a0[l]), 'a_up': rwkv_a_up[l].astype(BF16),
            'g_up': rwkv_g_up[l].astype(BF16),
            'k_k': row(rwkv_k_k[l]), 'k_a': row(rwkv_k_a[l]), 'r_k': row(rwkv_r_k[l]),
            'rgn_w': row(rwkv_gn_w[l]), 'rgn_b': row(rwkv_gn_b[l]),
            'tgn_w': row(ret_gn_w[l]), 'tgn_b': row(ret_gn_b[l]),
            'w_out': w_out_b,
            'ln1_g': row(ln1_g[l]), 'ln1_b': row(ln1_b[l]),
            'w_up': w_up_b, 'w_down': w_down_b,
            'ln2_g': row(ln2_g[l]), 'ln2_b': row(ln2_b[l]),
        }
        x2d, xb = _layer(x2d, xb, l, p, tables, B, S)
    return x2d.reshape(B, S, D)
```

```python
import functools

import jax
import jax.numpy as jnp
from jax import lax
from jax.experimental import pallas as pl
from jax.experimental.pallas import tpu as pltpu

F32 = jnp.float32
BF16 = jnp.bfloat16

D_MODEL = 2048
DEPTH = 4
FOX_W = 768
RWKV_W = 640
RET_W = 640
FOX_HEAD_DIM = 128
FOX_HEADS = 6
RWKV_HEAD_DIM = 64
RWKV_HEADS = 10
RET_HEAD_DIM = 128
RET_HEADS = 5
DECAY_LORA = 64
ICLR_LORA = 64
GATE_LORA = 128
D_FF = 4 * D_MODEL
RET_CHUNK = 128
ROPE_BASE = 10000.0
LN_EPS = 1e-5
RWKV_GN_EPS = 64e-5
RET_GN_EPS = 1e-5
ALPHA = (2 * DEPTH) ** 0.25

FOX_COLS = 3 * FOX_W + FOX_HEADS
RWKV_COLS = 3 * RWKV_W + DECAY_LORA + ICLR_LORA + GATE_LORA
RET_COLS = 4 * RET_W

LANE = 128
H_FF0 = RWKV_COLS
LOG2E = 1.4426950408889634
FOX_Q_SCALE = FOX_HEAD_DIM ** -0.5 * LOG2E

FOX_BLOCKS_PER_TRIP = 8
FOX_Q_ROWS = 1024
RWKV_DECAY_SCALE = 0.6065306597126334
RWKV_CHUNK = 64
RWKV_CHUNKS_PER_STEP = 4
NEG_BIG = -1e30

VMEM_LIMIT = 56 * 1024 * 1024


def _cparams(sem):
    return pltpu.CompilerParams(dimension_semantics=sem, vmem_limit_bytes=VMEM_LIMIT)


def _dot(a, b):
    return jnp.dot(a, b, preferred_element_type=F32)


def _dot_nt(a, b):
    return lax.dot_general(a, b, (((1,), (1,)), ((), ())), preferred_element_type=F32)


def _dot_tn(a, b):
    return lax.dot_general(a, b, (((0,), (0,)), ((), ())), preferred_element_type=F32)


def _split3(x):
    hi = x.astype(BF16)
    r1 = x - hi.astype(F32)
    mid = r1.astype(BF16)
    lo = (r1 - mid.astype(F32)).astype(BF16)
    return hi, mid, lo


def _split2(x):
    hi = x.astype(BF16)
    lo = (x - hi.astype(F32)).astype(BF16)
    return hi, lo


def _mask_dot(mask_bf16, x):
    hi, mid, lo = _split3(x)
    return _dot(mask_bf16, hi) + _dot(mask_bf16, mid) + _dot(mask_bf16, lo)


def _softplus(z):
    return jnp.maximum(z, 0.0) + jnp.log(1.0 + jnp.exp(-jnp.abs(z)))


def _sigmoid(z):
    return 1.0 / (1.0 + jnp.exp(-z))


def _layer_norm_rows(y, g, b):
    mu = jnp.mean(y, axis=-1, keepdims=True)
    yc = y - mu
    var = jnp.mean(yc * yc, axis=-1, keepdims=True)
    return yc * lax.rsqrt(var + LN_EPS) * g + b


def _inproj_kernel(x_ref, w_ref, o_ref):
    o_ref[...] = _dot(x_ref[...].astype(BF16), w_ref[...])


def _inproj_scaled_kernel(x_ref, w_ref, s_ref, o_ref):
    o_ref[...] = (_dot(x_ref[...].astype(BF16), w_ref[...]) * s_ref[...]).astype(o_ref.dtype)


def _in_proj(xb, w, l, col_scale=None):
    M, K = xb.shape
    N = w.shape[2]
    tm = min(1024, M)
    tn = N
    in_specs = [pl.BlockSpec((tm, K), lambda j, i: (i, 0)),
                pl.BlockSpec((None, K, tn), lambda j, i: (l, 0, j),
                             pipeline_mode=pl.Buffered(1))]
    args = [xb, w]
    if col_scale is not None:
        in_specs.append(pl.BlockSpec((1, tn), lambda j, i: (0, j)))
        args.append(col_scale)
    return pl.pallas_call(
        _inproj_kernel if col_scale is None else _inproj_scaled_kernel,
        name="in_proj",
        out_shape=jax.ShapeDtypeStruct((M, N), F32 if col_scale is None else BF16),
        grid=(N // tn, M // tm),
        in_specs=in_specs,
        out_specs=pl.BlockSpec((tm, tn), lambda j, i: (i, j)),
        compiler_params=_cparams(("parallel", "parallel")),
    )(*args)


def _fox_c_kernel(f_ref, bias_ref, c_ref, *, rows):
    S = f_ref.shape[0]
    r = lax.broadcasted_iota(jnp.int32, (rows, rows), 0)
    c = lax.broadcasted_iota(jnp.int32, (rows, rows), 1)
    tri = (c <= r).astype(BF16)

    bias = bias_ref[...]
    local = [_mask_dot(tri, -_softplus(-(f_ref[s:s + rows, :] + bias))) for s in range(0, S, rows)]
    carry = jnp.zeros((1, LANE), F32)
    for j, cs in enumerate(local):
        c_ref[:, j * rows:(j + 1) * rows] = ((cs + carry) * LOG2E).T[0:8, :]
        carry = carry + cs[rows - 1:rows, :]


def _fox_c(h, bias_pad, B, S):
    rows = min(256, S)
    return pl.pallas_call(
        functools.partial(_fox_c_kernel, rows=rows),
        name="fox_c",
        out_shape=jax.ShapeDtypeStruct((B, 8, S), F32),
        grid=(B,),
        in_specs=[pl.BlockSpec((S, LANE), lambda b: (b, H_FF0 // LANE)),
                  pl.BlockSpec((1, LANE), lambda b: (0, 0))],
        out_specs=pl.BlockSpec((None, 8, S), lambda b: (b, 0, 0)),
        compiler_params=_cparams(("parallel",)),
    )(h, bias_pad)


def _fox_attn_kernel(q_ref, qn_ref, k_ref, v_ref, c_ref, o_ref, s0_ref, *, tq, tk):
    i = pl.program_id(2)
    d = q_ref.shape[1]
    q0 = pl.multiple_of(i * tq, tq)
    c0 = c_ref[:, pl.ds(q0, tq)][:, 0:1]
    q = q_ref[...]

    def scores(start, row_lo=0):
        bias = c0 - c_ref[:, pl.ds(start, tk)]
        return _dot_nt(q[row_lo:], k_ref[pl.ds(start, tk), :]) + bias

    def next_step_scores():
        qn0 = pl.multiple_of(jnp.minimum(i + 1, pl.num_programs(2) - 1) * tq, tq)
        cn = c_ref[:, pl.ds(qn0, tq)][:, 0:1]
        return _dot_nt(qn_ref[...], k_ref[0:tk, :]) + (cn - c_ref[:, 0:tk])

    ones = jnp.ones((tk, d), BF16)

    def update(start, sc, carry, masked, row_lo=0):
        m_all, acc_all = carry
        m, acc = m_all[row_lo:], acc_all[row_lo:]
        if masked:
            row = lax.broadcasted_iota(jnp.int32, sc.shape, 0) + (q0 + row_lo)
            col = lax.broadcasted_iota(jnp.int32, sc.shape, 1) + start
            sc = jnp.where(col <= row, sc, NEG_BIG)
        m_new = jnp.maximum(m, jnp.max(sc, axis=1, keepdims=True))
        alpha = jnp.exp2(m - m_new)
        p = jnp.exp2((sc - m_new).astype(BF16))
        vb = jnp.concatenate([v_ref[pl.ds(start, tk), :], ones], axis=1)
        acc = alpha * acc + _dot(p, vb)
        if row_lo:
            m_new = jnp.concatenate([m_all[:row_lo], m_new], axis=0)
            acc = jnp.concatenate([acc_all[:row_lo], acc], axis=0)
        return m_new, acc

    U = FOX_BLOCKS_PER_TRIP

    D = tq // tk

    def walk(base, n, carry, last):
        skip = lambda u: max(u - (n - D), 0) * tk if last else 0
        s_cur = s0_ref[...]
        for u in range(n):
            start = pl.multiple_of(base + u * tk, tk)
            final = last and u == n - 1
            s_next = (next_step_scores() if final
                      else scores(pl.multiple_of(start + tk, tk), skip(u + 1)))
            carry = update(start, s_cur, carry, last and u >= n - D, skip(u))
            s_cur = s_next
        s0_ref[...] = s_cur
        return carry

    @pl.when(i == 0)
    def _():
        s0_ref[...] = scores(0)

    init = (jnp.full((tq, 1), NEG_BIG, F32), jnp.zeros((tq, 2 * d), F32))
    n_full = i * D
    trips = n_full // U
    carry = lax.fori_loop(
        0, trips, lambda t, c: walk(pl.multiple_of(t * (U * tk), tk), U, c, False), init)

    base = pl.multiple_of(trips * (U * tk), tk)
    rems = list(range(0, U, D))
    tails = [functools.partial(walk, base, r + D, last=True) for r in rems]
    m, acc = lax.switch((n_full - trips * U) // D, tails, carry)
    o_ref[...] = (acc[:, :d] / acc[:, d:]).astype(o_ref.dtype)


def _fox_attention(hq, c3, B, S):
    tk = min(512, S)
    tq = min(FOX_Q_ROWS, S)
    nq = S // tq
    nh = FOX_HEADS
    return pl.pallas_call(
        functools.partial(_fox_attn_kernel, tq=tq, tk=tk),
        name="fox_attn",
        scratch_shapes=[pltpu.VMEM((tq, tk), F32)],
        out_shape=jax.ShapeDtypeStruct((B * S, FOX_W), BF16),
        grid=(B, FOX_HEADS, nq),
        in_specs=[pl.BlockSpec((tq, LANE), lambda b, hd, i: (b * nq + i, hd)),
                  pl.BlockSpec((tq, LANE),
                               lambda b, hd, i: (b * nq + jnp.minimum(i + 1, nq - 1), hd)),
                  pl.BlockSpec((S, LANE), lambda b, hd, i: (b, nh + hd)),
                  pl.BlockSpec((S, LANE), lambda b, hd, i: (b, 2 * nh + hd)),
                  pl.BlockSpec((None, 1, S), lambda b, hd, i: (b * nh + hd, 0, 0))],
        out_specs=pl.BlockSpec((tq, LANE), lambda b, hd, i: (b * nq + i, hd)),
        compiler_params=_cparams(("parallel", "parallel", "arbitrary")),
    )(hq, hq, hq, hq, c3)


def _rwkv_premix(h, prev, mu, w0, wup, a0, aup, gup):
    rows = lax.broadcasted_iota(jnp.int32, h.shape, 0)
    hp = jnp.where(rows == 0, prev, pltpu.roll(h, 1, 0))
    hs = h + (hp - h) * mu
    W3 = 3 * RWKV_W
    r = hs[:, :RWKV_W]
    k = hs[:, RWKV_W:2 * RWKV_W]
    v = hs[:, 2 * RWKV_W:W3]
    wd = hs[:, W3:W3 + DECAY_LORA]
    ad = hs[:, W3 + DECAY_LORA:W3 + DECAY_LORA + ICLR_LORA]
    gd = hs[:, W3 + DECAY_LORA + ICLR_LORA:]
    z = w0 + _dot(jnp.tanh(wd).astype(BF16), wup)
    lw = -RWKV_DECAY_SCALE * _sigmoid(z)
    a = _sigmoid(a0 + _dot(ad.astype(BF16), aup))
    g = _dot(_sigmoid(gd).astype(BF16), gup)
    return r, k, v, lw, a, g


def _head_sums(x, e_ref):
    hi, lo = _split2(x)
    e = e_ref[...]
    e_last = e[:LANE, :LANE]
    outs = []
    for c0 in range(0, RWKV_W, 2 * LANE):
        w = min(2 * LANE, RWKV_W - c0)
        ee = e if w == 2 * LANE else e_last
        outs.append(_dot(hi[:, c0:c0 + w], ee) + _dot(lo[:, c0:c0 + w], ee))
    return jnp.concatenate(outs, axis=1)


def _rwkv_scan_kernel(h_ref, hprev_ref, mu_ref, w0_ref, wup_ref, a0_ref, aup_ref, gup_ref,
                      kk_ref, ka_ref, rk_ref, gw_ref, gb_ref, e_ref, o_ref, st_ref, *, nchunk):
    C = RWKV_CHUNK
    N = RWKV_HEAD_DIM
    T = nchunk * C
    first = pl.program_id(1) == 0

    @pl.when(first)
    def _():
        st_ref[...] = jnp.zeros(st_ref.shape, F32)

    prev = jnp.where(first, 0.0, hprev_ref[7:8, :])
    r, k, v, lw, a, g = _rwkv_premix(h_ref[...], prev, mu_ref[...], w0_ref[...], wup_ref[...],
                                     a0_ref[...], aup_ref[...], gup_ref[...])

    ri = lax.broadcasted_iota(jnp.int32, (C, C), 0)
    ci = lax.broadcasted_iota(jnp.int32, (C, C), 1)
    incl = ci <= ri
    strict = ci < ri

    rt = lax.broadcasted_iota(jnp.int32, (T, T), 0)
    ct = lax.broadcasted_iota(jnp.int32, (T, T), 1)
    tri = ((ct <= rt) & (ct >= (rt // C) * C)).astype(BF16)
    lw_hi, lw_lo = _split2(lw)
    cum = _dot(tri, lw_hi) + _dot(tri, lw_lo)
    cls = [cum[(c + 1) * C - 1:(c + 1) * C, :] for c in range(nchunk)]
    cl_rows = jnp.concatenate([jnp.broadcast_to(t, (C, RWKV_W)) for t in cls], axis=0)
    e_in = jnp.exp(cum)
    e_ex = jnp.exp(cum - lw)
    e_neg = jnp.exp(-cum)
    e_tail = jnp.exp(cl_rows - cum)
    pcs = [jnp.exp(t) for t in cls]

    kk = k * kk_ref[...]
    kkn = kk * lax.rsqrt(jnp.maximum(_head_sums(kk * kk, e_ref), 1e-24))
    k2 = k * (1.0 + (a - 1.0) * ka_ref[...])
    bvec = kkn * a
    At_f = -kkn * e_ex
    Rt_f = r * e_in
    AtB = At_f.astype(BF16)
    RtB = Rt_f.astype(BF16)
    BtB = (bvec * e_neg).astype(BF16)
    KtB = (k2 * e_neg).astype(BF16)
    BpB = (bvec * e_tail).astype(BF16)
    KpB = (k2 * e_tail).astype(BF16)
    vB = v.astype(BF16)

    HD = range(RWKV_HEADS)
    hsl = [slice(hd * N, (hd + 1) * N) for hd in HD]
    sls = [(slice(c * C, (c + 1) * C), hs) for c in range(nchunk) for hs in hsl]
    At = [At_f[rs, hs] for rs, hs in sls]
    Rt = [Rt_f[rs, hs] for rs, hs in sls]
    vb = [vB[rs, hs] for rs, hs in sls]
    Bt = [BtB[rs, hs] for rs, hs in sls]
    Kt = [KtB[rs, hs] for rs, hs in sls]
    Bp = [BpB[rs, hs] for rs, hs in sls]
    Kp = [KpB[rs, hs] for rs, hs in sls]

    AR = [jnp.concatenate([AtB[rs, hs], RtB[rs, hs]], axis=0) for rs, hs in sls]
    BK = [jnp.concatenate([x, y], axis=0) for x, y in zip(Bt, Kt)]
    G = [_dot_nt(x, y) for x, y in zip(AR, BK)]
    r2 = lax.broadcasted_iota(jnp.int32, (C, 2 * C), 0)
    c2 = lax.broadcasted_iota(jnp.int32, (C, 2 * C), 1)
    incl2 = jnp.where(c2 >= C, c2 - C, c2) <= r2
    P = [jnp.where(strict, t[:C, :C], 0.0) for t in G]
    Lak = [jnp.where(strict, t[:C, C:], 0.0).astype(BF16) for t in G]
    L2 = [jnp.where(incl2, t[C:], 0.0).astype(BF16) for t in G]

    X = [jnp.concatenate([x, _dot(y, z)], axis=1) for x, y, z in zip(At, Lak, vb)]
    steps = C.bit_length() - 1
    for it in range(steps):
        Pb = [t.astype(BF16) for t in P]
        Xb = [x.astype(BF16) for x in X]
        if it + 1 < steps:
            PX = [_dot(p, jnp.concatenate([x, p], axis=1)) for p, x in zip(Pb, Xb)]
            X = [x + t[:, :2 * N] for x, t in zip(X, PX)]
            P = [t[:, 2 * N:] for t in PX]
        else:
            X = [x + _dot(p, xb) for x, p, xb in zip(X, Pb, Xb)]
    Xb = [x.astype(BF16) for x in X]

    zpad = jnp.zeros((C, N), BF16)
    RHS = [jnp.concatenate([x, jnp.concatenate([zpad, z], axis=1)], axis=0)
           for x, z in zip(Xb, vb)]
    LX = [_dot(x, y) for x, y in zip(L2, RHS)]
    Qp = [(x + y[:, :N]).astype(BF16) for x, y in zip(Rt, LX)]
    Y0 = [y[:, N:] for y in LX]

    XB = [_dot_tn(x, b) for x, b in zip(Xb, Bp)]
    Mw = [t[:N].astype(BF16) for t in XB]
    Nn = [t[N:] + _dot_tn(x, y) for t, x, y in zip(XB, vb, Kp)]

    S = [st_ref[hd] for hd in HD]
    y_rows = []
    for c in range(nchunk):
        ch = [c * RWKV_HEADS + hd for hd in HD]
        Ssp = [_split2(s) for s in S]
        ys = [_dot_nt(Qp[j], sp[0]) + Y0[j] for j, sp in zip(ch, Ssp)]
        y_rows.append(jnp.concatenate(ys, axis=1))
        SM = [_dot(jnp.concatenate(sp, axis=0), Mw[j]) for sp, j in zip(Ssp, ch)]
        S = [s * pcs[c][:, hs] + sm[:N] + sm[N:] + Nn[j]
             for s, sm, j, hs in zip(S, SM, ch, hsl)]
    for hd in HD:
        st_ref[hd] = S[hd]

    y = jnp.concatenate(y_rows, axis=0)
    inv_n = 1.0 / N
    yc = y - _head_sums(y, e_ref) * inv_n
    var = _head_sums(yc * yc, e_ref) * inv_n
    yn = yc * lax.rsqrt(var + RWKV_GN_EPS) * gw_ref[...] + gb_ref[...]
    bonus = _head_sums(r * k2 * rk_ref[...], e_ref) * v
    o_ref[...] = ((yn + bonus) * g).astype(o_ref.dtype)


def _rwkv_mix(h, mu, w0, wup, a0, aup, gup, k_k, k_a, r_k, gn_w, gn_b, B, S):
    nchunk = RWKV_CHUNKS_PER_STEP
    T = nchunk * RWKV_CHUNK
    nc = S // T
    row = lambda b, c: (b * nc + c, 0)
    prev_row = lambda b, c: (jnp.maximum((b * nc + c) * (T // 8) - 1, 0), 0)
    const = lambda b, c: (0, 0)
    vec = pl.BlockSpec((1, RWKV_W), const)
    hid = jnp.arange(2 * LANE, dtype=jnp.int32) // RWKV_HEAD_DIM
    head_ones = (hid[:, None] == hid[None, :]).astype(BF16)
    return pl.pallas_call(
        functools.partial(_rwkv_scan_kernel, nchunk=nchunk),
        name="rwkv_scan",
        out_shape=jax.ShapeDtypeStruct((B * S, RWKV_W), BF16),
        grid=(B, nc),
        in_specs=[pl.BlockSpec((T, RWKV_COLS), row),
                  pl.BlockSpec((8, RWKV_COLS), prev_row),
                  pl.BlockSpec((1, RWKV_COLS), const),
                  vec,
                  pl.BlockSpec((DECAY_LORA, RWKV_W), const),
                  vec,
                  pl.BlockSpec((ICLR_LORA, RWKV_W), const),
                  pl.BlockSpec((GATE_LORA, RWKV_W), const),
                  vec, vec, vec, vec, vec,
                  pl.BlockSpec((2 * LANE, 2 * LANE), const)],
        out_specs=pl.BlockSpec((T, RWKV_W), row),
        scratch_shapes=[pltpu.VMEM((RWKV_HEADS, RWKV_HEAD_DIM, RWKV_HEAD_DIM), F32)],
        compiler_params=_cparams(("parallel", "arbitrary")),
    )(h, h, mu, w0, wup, a0, aup, gup, k_k, k_a, r_k, gn_w, gn_b, head_ones)


def _ret_kernel(h_ref, cos_ref, sin_ref, dmat_ref, qd_ref, kd_ref, cd_ref, gw_ref, gb_ref,
                o_ref, R_ref, *, nchunk):
    C = RET_CHUNK
    d = RET_HEAD_DIM
    H = RET_HEADS
    W = RET_W

    @pl.when(pl.program_id(1) == 0)
    def _():
        R_ref[...] = jnp.zeros(R_ref.shape, F32)

    pairs = [(c, hd) for c in range(nchunk) for hd in range(H)]
    rows = lambda c: slice(c * C, (c + 1) * C)
    cols = lambda part, hd: slice(part * W + hd * d, part * W + (hd + 1) * d)
    cos = [cos_ref[rows(c), :] for c in range(nchunk)]
    sin = [sin_ref[rows(c), :] for c in range(nchunk)]
    q = [h_ref[rows(c), cols(0, hd)] for c, hd in pairs]
    k = [h_ref[rows(c), cols(1, hd)] for c, hd in pairs]
    qr = [x * cos[c] + pltpu.roll(x, d // 2, 1) * sin[c] for x, (c, hd) in zip(q, pairs)]
    kr = [(x * cos[c] + pltpu.roll(x, d // 2, 1) * sin[c]) * (d ** -0.5)
          for x, (c, hd) in zip(k, pairs)]
    vb = [h_ref[rows(c), cols(2, hd)].astype(BF16) for c, hd in pairs]
    qb = [x.astype(BF16) for x in qr]
    kb = [x.astype(BF16) for x in kr]
    kdb = [(x * kd_ref[hd]).astype(BF16) for x, (c, hd) in zip(kr, pairs)]
    inner = [(_dot_nt(x, y) * dmat_ref[hd]).astype(BF16) for x, y, (c, hd) in zip(qb, kb, pairs)]
    o_in = [_dot(x, y) for x, y in zip(inner, vb)]
    kv = [_dot_tn(x, y) for x, y in zip(kdb, vb)]

    R = [R_ref[hd] for hd in range(H)]
    outs = []
    for c in range(nchunk):
        ch = [c * H + hd for hd in range(H)]
        Rsp = [_split2(x) for x in R]
        outs += [o_in[j] + (_dot(qb[j], sp[0]) + _dot(qb[j], sp[1])) * qd_ref[hd]
                 for hd, (j, sp) in enumerate(zip(ch, Rsp))]
        R = [cd_ref[hd][0:1, :] * x + kv[j] for hd, (x, j) in enumerate(zip(R, ch))]
    for hd in range(H):
        R_ref[hd] = R[hd]

    mus = [jnp.mean(o, axis=1, keepdims=True) for o in outs]
    ocs = [o - mu for o, mu in zip(outs, mus)]
    rstd = [lax.rsqrt(jnp.mean(oc * oc, axis=1, keepdims=True) + RET_GN_EPS) for oc in ocs]
    for oc, rs, (c, hd) in zip(ocs, rstd, pairs):
        gsl = slice(hd * d, (hd + 1) * d)
        on = oc * rs * gw_ref[:, gsl] + gb_ref[:, gsl]
        gate = h_ref[rows(c), cols(3, hd)]
        o_ref[rows(c), gsl] = (gate * _sigmoid(gate) * on).astype(o_ref.dtype)


def _ret_tables(S):
    C, d, H = RET_CHUNK, RET_HEAD_DIM, RET_HEADS
    half = d // 2
    inv = 1.0 / (ROPE_BASE ** (jnp.arange(half, dtype=F32) / half))
    ang = jnp.arange(S, dtype=F32)[:, None] * inv[None, :]
    cos = jnp.cos(ang)
    sin = jnp.sin(ang)
    cos2 = jnp.concatenate([cos, cos], axis=1)
    sin2 = jnp.concatenate([-sin, sin], axis=1)
    log_g = jnp.log(1.0 - 2.0 ** (-5.0 - jnp.arange(H, dtype=F32)))
    pos = jnp.arange(C, dtype=F32)
    rel = pos[:, None] - pos[None, :]
    dmat = jnp.where(rel >= 0, jnp.exp(log_g[:, None, None] * jnp.maximum(rel, 0.0)), 0.0)
    qd = jnp.broadcast_to(jnp.exp(log_g[:, None] * (pos + 1.0))[..., None], (H, C, d))
    kd = jnp.broadcast_to(jnp.exp(log_g[:, None] * (C - 1.0 - pos))[..., None], (H, C, d))
    cd = jnp.broadcast_to(jnp.exp(log_g * C)[:, None, None], (H, 8, d))
    return cos2, sin2, dmat, qd, kd, cd


def _retention(h, tables, gn_w, gn_b, B, S):
    cos2, sin2, dmat, qd, kd, cd = tables
    C = RET_CHUNK
    T = min(512, S)
    nt = S // T
    H = RET_HEADS
    tab = pl.BlockSpec((T, LANE), lambda b, t: (t, 0))
    per_head = lambda rows: pl.BlockSpec((H, rows, LANE), lambda b, t: (0, 0, 0))
    vec = pl.BlockSpec((1, RET_W), lambda b, t: (0, 0))
    return pl.pallas_call(
        functools.partial(_ret_kernel, nchunk=T // C),
        name="retention",
        out_shape=jax.ShapeDtypeStruct((B * S, RET_W), BF16),
        grid=(B, nt),
        in_specs=[pl.BlockSpec((T, RET_COLS), lambda b, t: (b * nt + t, 0)),
                  tab, tab, per_head(C), per_head(C), per_head(C), per_head(8), vec, vec],
        out_specs=pl.BlockSpec((T, RET_W), lambda b, t: (b * nt + t, 0)),
        scratch_shapes=[pltpu.VMEM((H, RET_HEAD_DIM, RET_HEAD_DIM), F32)],
        compiler_params=_cparams(("parallel", "arbitrary")),
    )(h, cos2, sin2, dmat, qd, kd, cd, gn_w, gn_b)


def _outproj_kernel(x_ref, yf_ref, yr_ref, yt_ref, w_ref, g_ref, b_ref, o_ref):
    r0, r1 = FOX_W, FOX_W + RWKV_W
    y = (_dot(yf_ref[...], w_ref[:r0, :]) + _dot(yr_ref[...], w_ref[r0:r1, :])
         + _dot(yt_ref[...], w_ref[r1:, :]))
    o_ref[...] = _layer_norm_rows(ALPHA * x_ref[...] + y, g_ref[...], b_ref[...])


def _out_proj_ln(x2d, y_fox, y_rwkv, y_ret, w_out, l, g, b):
    M, D = x2d.shape
    tm = min(512, M)
    row = lambda i: (i, 0)
    const = lambda i: (0, 0)
    return pl.pallas_call(
        _outproj_kernel,
        name="out_proj_ln",
        out_shape=jax.ShapeDtypeStruct((M, D), F32),
        grid=(M // tm,),
        in_specs=[pl.BlockSpec((tm, D), row), pl.BlockSpec((tm, FOX_W), row),
                  pl.BlockSpec((tm, RWKV_W), row), pl.BlockSpec((tm, RET_W), row),
                  pl.BlockSpec((None, D, D), lambda i: (l, 0, 0)),
                  pl.BlockSpec((1, D), const), pl.BlockSpec((1, D), const)],
        out_specs=pl.BlockSpec((tm, D), row),
        compiler_params=_cparams(("parallel",)),
    )(x2d, y_fox, y_rwkv, y_ret, w_out, g, b)


def _ffn_kernel(x_ref, wu_ref, wd_ref, g_ref, b_ref, o_ref, ob_ref, acc_ref, xb_ref):
    f = pl.program_id(1)

    @pl.when(f == 0)
    def _():
        x = x_ref[...]
        xb_ref[...] = x.astype(BF16)
        acc_ref[...] = ALPHA * x

    hid = jnp.maximum(_dot(xb_ref[...], wu_ref[...]), 0.0)
    acc_ref[...] += _dot((hid * hid).astype(BF16), wd_ref[...])

    @pl.when(f == pl.num_programs(1) - 1)
    def _():
        y = _layer_norm_rows(acc_ref[...], g_ref[...], b_ref[...])
        o_ref[...] = y
        ob_ref[...] = y.astype(BF16)


def _ffn_ln(x2d, w_up, w_down, l, g, b):
    M, D = x2d.shape
    F = w_up.shape[2]
    tm = min(512, M)
    tf = 1024
    return pl.pallas_call(
        _ffn_kernel,
        name="ffn_ln",
        out_shape=(jax.ShapeDtypeStruct((M, D), F32), jax.ShapeDtypeStruct((M, D), BF16)),
        grid=(M // tm, F // tf),
        in_specs=[pl.BlockSpec((tm, D), lambda i, f: (i, 0)),
                  pl.BlockSpec((None, D, tf), lambda i, f: (l, 0, f)),
                  pl.BlockSpec((None, tf, D), lambda i, f: (l, f, 0)),
                  pl.BlockSpec((1, D), lambda i, f: (0, 0)),
                  pl.BlockSpec((1, D), lambda i, f: (0, 0))],
        out_specs=(pl.BlockSpec((tm, D), lambda i, f: (i, 0)),
                   pl.BlockSpec((tm, D), lambda i, f: (i, 0))),
        scratch_shapes=[pltpu.VMEM((tm, D), F32), pltpu.VMEM((tm, D), BF16)],
        compiler_params=_cparams(("parallel", "arbitrary")),
    )(x2d, w_up, w_down, g, b)


def _permute_w_in(w_in):
    L, D, _ = w_in.shape
    w_in = w_in.astype(BF16)
    w_fox = w_in[:, :, :FOX_COLS]
    w_rwkv = w_in[:, :, FOX_COLS:FOX_COLS + RWKV_COLS]
    w_ret = w_in[:, :, FOX_COLS + RWKV_COLS:]
    w_ff = w_fox[:, :, 3 * FOX_W:]
    pad = jnp.zeros((L, D, LANE - FOX_HEADS), BF16)
    w_q = w_fox[:, :, :3 * FOX_W]
    w_rw = jnp.concatenate([w_rwkv, w_ff, pad], axis=-1)
    return w_q, w_rw, w_ret


def _layer(x2d, xb, l, p, tables, B, S):
    hq = _in_proj(xb, p['w_q'], l, p['q_scale'])
    h = _in_proj(xb, p['w_rw'], l)
    ht = _in_proj(xb, p['w_ret'], l)
    c = _fox_c(h, p['fox_bias'], B, S)
    c3 = c[:, :FOX_HEADS, :].reshape(B * FOX_HEADS, 1, S)
    y_fox = _fox_attention(hq, c3, B, S)
    y_rwkv = _rwkv_mix(h, p['mu'], p['w0'], p['w_up_lora'], p['a0'], p['a_up'], p['g_up'],
                       p['k_k'], p['k_a'], p['r_k'], p['rgn_w'], p['rgn_b'], B, S)
    y_ret = _retention(ht, tables, p['tgn_w'], p['tgn_b'], B, S)
    x2d = _out_proj_ln(x2d, y_fox, y_rwkv, y_ret, p['w_out'], l, p['ln1_g'], p['ln1_b'])
    return _ffn_ln(x2d, p['w_up'], p['w_down'], l, p['ln2_g'], p['ln2_b'])


def kernel(x, w_in, fox_forget_bias, rwkv_mu, rwkv_w0, rwkv_w_up, rwkv_a0, rwkv_a_up, rwkv_g_up,
           rwkv_k_k, rwkv_k_a, rwkv_r_k, rwkv_gn_w, rwkv_gn_b, ret_gn_w, ret_gn_b, w_out, ln1_g,
           ln1_b, w_up, w_down, ln2_g, ln2_b):
    B, S, D = x.shape
    L = w_in.shape[0]
    w_q, w_rw, w_ret = _permute_w_in(w_in)
    q_scale = jnp.concatenate([jnp.full((1, FOX_W), FOX_Q_SCALE, F32),
                               jnp.ones((1, 2 * FOX_W), F32)], axis=1)
    w_out_b = w_out.astype(BF16)
    w_up_b = w_up.astype(BF16)
    w_down_b = w_down.astype(BF16)
    bias_pad = jnp.pad(fox_forget_bias, ((0, 0), (0, LANE - FOX_HEADS)))
    tables = _ret_tables(S)
    row = lambda t: t.reshape(1, -1)
    x2d = x.reshape(B * S, D)
    xb = x2d
    for l in range(L):
        p = {
            'w_rw': w_rw, 'w_ret': w_ret, 'w_q': w_q, 'q_scale': q_scale,
            'fox_bias': row(bias_pad[l]),
            'mu': row(rwkv_mu[l]), 'w0': row(rwkv_w0[l]), 'w_up_lora': rwkv_w_up[l].astype(BF16),
            'a0': row(rwkv_a0[l]), 'a_up': rwkv_a_up[l].astype(BF16),
            'g_up': rwkv_g_up[l].astype(BF16),
            'k_k': row(rwkv_k_k[l]), 'k_a': row(rwkv_k_a[l]), 'r_k': row(rwkv_r_k[l]),
            'rgn_w': row(rwkv_gn_w[l]), 'rgn_b': row(rwkv_gn_b[l]),
            'tgn_w': row(ret_gn_w[l]), 'tgn_b': row(ret_gn_b[l]),
            'w_out': w_out_b,
            'ln1_g': row(ln1_g[l]), 'ln1_b': row(ln1_b[l]),
            'w_up': w_up_b, 'w_down': w_down_b,
            'ln2_g': row(ln2_g[l]), 'ln2_b': row(ln2_b[l]),
        }
        x2d, xb = _layer(x2d, xb, l, p, tables, B, S)
    return x2d.reshape(B, S, D)
```
